```python
import math
import jax, jax.numpy as jnp
from jax import lax
import numpy as np


D_MODEL = 1024
BATCH = 4
SEQ = 8192
DEPTH = 1

N_HEADS_A = 8
HEAD_DIM = 64
WIDTH_A = N_HEADS_A * HEAD_DIM
MOBA_BLOCK = 256
MOBA_TOPK = 3
Q_CHUNK = 64
N_POOL_GROUPS = 4
POOL_WINDOWS = (2, 4, 8, 16)
WIDTH_B = D_MODEL // 2
POOL_GROUP_DIM = WIDTH_B // N_POOL_GROUPS
N_BRANCHES = 2
D_FF = 4 * D_MODEL
REL_BUCKETS = 32
REL_MAX_DIST = 128
DEEPNORM_ALPHA = (2.0 * DEPTH) ** 0.25
DEEPNORM_BETA = (8.0 * DEPTH) ** -0.25
LN_EPS = 1e-5
NEG_INF = -1e30
IN_COLS = 3 * WIDTH_A + WIDTH_B + N_BRANCHES * D_MODEL

kernel_name = 'moba_pool_hybrid_deepnorm'


def layer_norm(x, g, b):
    xf = x.astype(jnp.float32)
    mu = jnp.mean(xf, axis=-1, keepdims=True)
    var = jnp.mean(jnp.square(xf - mu), axis=-1, keepdims=True)
    return ((xf - mu) * lax.rsqrt(var + LN_EPS) * g + b).astype(x.dtype)


def rel_bucket(dist):
    n = jnp.maximum(dist, 0)
    max_exact = REL_BUCKETS // 2
    nf = jnp.maximum(n, 1).astype(jnp.float32)
    large = max_exact + (jnp.log(nf / max_exact) / math.log(REL_MAX_DIST / max_exact)
                         * (REL_BUCKETS - max_exact)).astype(jnp.int32)
    large = jnp.minimum(large, REL_BUCKETS - 1)
    return jnp.where(n < max_exact, n, large)


def moba_attention(q, k, v, rel_table):
    B, H, S, Dh = q.shape
    nb = -(-S // MOBA_BLOCK)
    s_pad = nb * MOBA_BLOCK
    pad = ((0, 0), (0, 0), (0, s_pad - S), (0, 0))
    q = jnp.pad(q, pad)
    k = jnp.pad(k, pad)
    v = jnp.pad(v, pad)
    k_blk = k.reshape(B, H, nb, MOBA_BLOCK, Dh)
    v_blk = v.reshape(B, H, nb, MOBA_BLOCK, Dh)
    k_mean = jnp.mean(k_blk.astype(jnp.float32), axis=3).astype(k.dtype)
    topk = min(MOBA_TOPK, nb)
    n_chunks = s_pad // Q_CHUNK
    q_ch = q.reshape(B, H, n_chunks, Q_CHUNK, Dh).transpose(2, 0, 1, 3, 4)
    table_t = rel_table.T
    scale = HEAD_DIM ** -0.5
    b_ix = jnp.arange(B)[:, None, None, None]
    h_ix = jnp.arange(H)[None, :, None, None]
    offs = jnp.arange(MOBA_BLOCK)
    blk_ids = jnp.arange(nb)

    def one_chunk(args):
        qc, c = args
        q_pos = c * Q_CHUNK + jnp.arange(Q_CHUNK)
        blk = (c * Q_CHUNK) // MOBA_BLOCK
        gate = jnp.einsum('bhqd,bhnd->bhqn', qc, k_mean,
                          preferred_element_type=jnp.float32)
        gate = jnp.where(blk_ids < blk, gate, NEG_INF)
        _, idx = lax.top_k(gate, topk)
        sel_valid = idx < blk
        k_sel = k_blk[b_ix, h_ix, idx]
        v_sel = v_blk[b_ix, h_ix, idx]
        s_sel = jnp.einsum('bhqd,bhqnld->bhqnl', qc, k_sel,
                           preferred_element_type=jnp.float32) * scale
        k_pos_sel = idx[..., None] * MOBA_BLOCK + offs
        bucket_sel = rel_bucket(q_pos[None, None, :, None, None] - k_pos_sel)
        s_sel = s_sel + table_t[h_ix[..., None], bucket_sel]
        s_sel = jnp.where(sel_valid[..., None], s_sel, NEG_INF)
        k_own = lax.dynamic_index_in_dim(k_blk, blk, axis=2, keepdims=False)
        v_own = lax.dynamic_index_in_dim(v_blk, blk, axis=2, keepdims=False)
        dist_own = q_pos[:, None] - (blk * MOBA_BLOCK + offs)[None, :]
        s_own = jnp.einsum('bhqd,bhld->bhql', qc, k_own,
                           preferred_element_type=jnp.float32) * scale
        s_own = s_own + table_t[:, rel_bucket(dist_own)]
        s_own = jnp.where(dist_own >= 0, s_own, NEG_INF)
        logits = jnp.concatenate(
            [s_sel.reshape(B, H, Q_CHUNK, topk * MOBA_BLOCK), s_own], axis=-1)
        p = jax.nn.softmax(logits, axis=-1)
        p_sel = p[..., :topk * MOBA_BLOCK].reshape(B, H, Q_CHUNK, topk, MOBA_BLOCK).astype(v.dtype)
        p_own = p[..., topk * MOBA_BLOCK:].astype(v.dtype)
        return (jnp.einsum('bhqnl,bhqnld->bhqd', p_sel, v_sel)
                + jnp.einsum('bhql,bhld->bhqd', p_own, v_own))

    out = lax.map(one_chunk, (q_ch, jnp.arange(n_chunks)))
    out = out.transpose(1, 2, 0, 3, 4).reshape(B, H, s_pad, Dh)
    return out[:, :, :S]


def pool_mixer(p, w_pool, pool_scale):
    B, S, _ = p.shape
    pf = p.reshape(B, S, N_POOL_GROUPS, POOL_GROUP_DIM).astype(jnp.float32)
    cs = jnp.pad(jnp.cumsum(pf, axis=1), ((0, 0), (1, 0), (0, 0), (0, 0)))
    t = jnp.arange(S)[:, None]
    win = jnp.array(POOL_WINDOWS, dtype=jnp.int32)[None, :]
    lo = jnp.maximum(t + 1 - win, 0)
    g_ix = jnp.arange(N_POOL_GROUPS)[None, :]
    window_sum = cs[:, 1:] - cs[:, lo, g_ix]
    count = jnp.minimum(t + 1, win).astype(jnp.float32)
    pooled = (window_sum / count[None, :, :, None] - pf).astype(p.dtype)
    mixed = jnp.einsum('bsgc,gcd->bsgd', pooled, w_pool)
    return mixed.reshape(B, S, WIDTH_B) * pool_scale


def setup_inputs(seed: int = 0) -> dict:
    key = jax.random.key(seed)
    ks = jax.random.split(key, 20)
    f32 = jnp.float32

    def nrm(k, shape, scale):
        return jax.random.normal(k, shape, f32) * scale

    x = nrm(ks[0], (BATCH, SEQ, D_MODEL), 1.0)
    col_scale = jnp.ones((IN_COLS,), f32).at[2 * WIDTH_A:3 * WIDTH_A].set(DEEPNORM_BETA)
    w_in = nrm(ks[1], (DEPTH, D_MODEL, IN_COLS), D_MODEL ** -0.5) * col_scale
    b_gate = nrm(ks[2], (DEPTH, N_BRANCHES * D_MODEL), 0.1)
    rel_table = nrm(ks[3], (REL_BUCKETS, N_HEADS_A), 0.5)
    w_pool = nrm(ks[4], (DEPTH, N_POOL_GROUPS, POOL_GROUP_DIM, POOL_GROUP_DIM), POOL_GROUP_DIM ** -0.5)
    pool_scale = 1.0 + nrm(ks[5], (DEPTH, WIDTH_B), 0.1)
    w_o_attn = nrm(ks[6], (DEPTH, WIDTH_A, D_MODEL), WIDTH_A ** -0.5 * DEEPNORM_BETA)
    w_o_pool = nrm(ks[7], (DEPTH, WIDTH_B, D_MODEL), WIDTH_B ** -0.5 * DEEPNORM_BETA)
    w_out = nrm(ks[8], (DEPTH, D_MODEL, D_MODEL), D_MODEL ** -0.5 * DEEPNORM_BETA)
    ln1_g = 1.0 + nrm(ks[9], (DEPTH, D_MODEL), 0.05)
    ln1_b = nrm(ks[10], (DEPTH, D_MODEL), 0.05)
    w_ff1 = nrm(ks[11], (DEPTH, D_MODEL, D_FF), D_MODEL ** -0.5 * DEEPNORM_BETA)
    b_ff1 = nrm(ks[12], (DEPTH, D_FF), 0.02)
    w_ff2 = nrm(ks[13], (DEPTH, D_FF, D_MODEL), D_FF ** -0.5 * DEEPNORM_BETA)
    b_ff2 = nrm(ks[14], (DEPTH, D_MODEL), 0.02)
    ln2_g = 1.0 + nrm(ks[15], (DEPTH, D_MODEL), 0.05)
    ln2_b = nrm(ks[16], (DEPTH, D_MODEL), 0.05)
    return {'x': x, 'w_in': w_in, 'b_gate': b_gate, 'rel_table': rel_table,
            'w_pool': w_pool, 'pool_scale': pool_scale, 'w_o_attn': w_o_attn,
            'w_o_pool': w_o_pool, 'w_out': w_out, 'ln1_g': ln1_g, 'ln1_b': ln1_b,
            'w_ff1': w_ff1, 'b_ff1': b_ff1, 'w_ff2': w_ff2, 'b_ff2': b_ff2,
            'ln2_g': ln2_g, 'ln2_b': ln2_b}


def reference(x, w_in, b_gate, rel_table, w_pool, pool_scale, w_o_attn, w_o_pool,
              w_out, ln1_g, ln1_b, w_ff1, b_ff1, w_ff2, b_ff2, ln2_g, ln2_b):
    B, S, _ = x.shape
    h = x
    split_at = [WIDTH_A, 2 * WIDTH_A, 3 * WIDTH_A, 3 * WIDTH_A + WIDTH_B]
    for l in range(DEPTH):
        proj = jnp.einsum('bsd,de->bse', h, w_in[l])
        q, k, v, p, g = jnp.split(proj, split_at, axis=-1)

        def heads(t):
            return t.reshape(B, S, N_HEADS_A, HEAD_DIM).transpose(0, 2, 1, 3)

        a = moba_attention(heads(q), heads(k), heads(v), rel_table)
        a = a.transpose(0, 2, 1, 3).reshape(B, S, WIDTH_A)
        branch_a = a @ w_o_attn[l]
        branch_b = pool_mixer(p, w_pool[l], pool_scale[l]) @ w_o_pool[l]
        gates = jax.nn.sigmoid((g + b_gate[l]).reshape(B, S, N_BRANCHES, D_MODEL))
        mix = (gates[:, :, 0] * branch_a + gates[:, :, 1] * branch_b) @ w_out[l]
        h = layer_norm(DEEPNORM_ALPHA * h + mix, ln1_g[l], ln1_b[l])
        ff = jnp.square(jax.nn.relu(h @ w_ff1[l] + b_ff1[l])) @ w_ff2[l] + b_ff2[l]
        h = layer_norm(DEEPNORM_ALPHA * h + ff, ln2_g[l], ln2_b[l])
    return h
```

```python
import functools
import math

import jax
import jax.numpy as jnp
from jax import lax
from jax.experimental import pallas as pl
from jax.experimental.pallas import tpu as pltpu

D_MODEL = 1024
N_HEADS = 8
HEAD_DIM = 64
WIDTH_A = N_HEADS * HEAD_DIM
MOBA_BLOCK = 256
MOBA_TOPK = 3
N_POOL_GROUPS = 4
POOL_WINDOWS = (2, 4, 8, 16)
WIDTH_B = 512
POOL_GROUP_DIM = WIDTH_B // N_POOL_GROUPS
D_FF = 4 * D_MODEL
REL_BUCKETS = 32
REL_MAX_DIST = 128
DEEPNORM_ALPHA = 2.0 ** 0.25
LN_EPS = 1e-5
NEG_INF = -1e30

LANES = 128
HALO = 16
VMEM_LIMIT = 48 * 1024 * 1024

BF16 = jnp.bfloat16
F32 = jnp.float32


def _layer_norm(y, g, b):
    mu = jnp.mean(y, axis=-1, keepdims=True)
    d = y - mu
    var = jnp.mean(d * d, axis=-1, keepdims=True)
    return d * lax.rsqrt(var + LN_EPS) * g + b


def _qkv_kernel(x_ref, w_ref, q_ref, k_ref, v_ref, kmean_ref):
    blk = pl.program_id(1)
    xb = x_ref[0].astype(BF16)
    acc = jnp.dot(xb, w_ref[...], preferred_element_type=F32)
    t = acc.shape[0]
    lane = lax.broadcasted_iota(jnp.int32, (t, LANES), 1)
    lo = lane < HEAD_DIM
    onehot = jnp.where(lane == HEAD_DIM + blk, 1.0, 0.0)
    scale = HEAD_DIM ** -0.5
    for hp in range(N_HEADS // 2):
        qp = acc[:, hp * LANES:(hp + 1) * LANES] * scale
        kp = acc[:, WIDTH_A + hp * LANES:WIDTH_A + (hp + 1) * LANES]
        vp = acc[:, 2 * WIDTH_A + hp * LANES:2 * WIDTH_A + (hp + 1) * LANES]
        qr = pltpu.roll(qp, HEAD_DIM, 1)
        kr = pltpu.roll(kp, HEAD_DIM, 1)
        q_ref[0, 2 * hp] = jnp.where(lo, qp, 0.0).astype(BF16)
        q_ref[0, 2 * hp + 1] = jnp.where(lo, qr, 0.0).astype(BF16)
        k_ref[0, 2 * hp] = jnp.where(lo, kp, onehot).astype(BF16)
        k_ref[0, 2 * hp + 1] = jnp.where(lo, kr, onehot).astype(BF16)
        v_ref[0, 2 * hp] = jnp.where(lo, vp, 1.0).astype(BF16)
        v_ref[0, 2 * hp + 1] = jnp.where(lo, 1.0, vp).astype(BF16)
    kmean_ref[0, 0] = jnp.mean(acc[:, WIDTH_A:2 * WIDTH_A], axis=0, keepdims=True)


def _qkv_proj(x, w_qkv):
    b, s, d = x.shape
    t = MOBA_BLOCK
    nb = s // t
    slab = jax.ShapeDtypeStruct((b, N_HEADS, s, LANES), BF16)
    slab_spec = pl.BlockSpec((1, N_HEADS, t, LANES), lambda bi, si: (bi, 0, si, 0))
    return pl.pallas_call(
        _qkv_kernel,
        grid=(b, nb),
        in_specs=[
            pl.BlockSpec((1, t, d), lambda bi, si: (bi, si, 0)),
            pl.BlockSpec((d, 3 * WIDTH_A), lambda bi, si: (0, 0)),
        ],
        out_specs=[
            slab_spec, slab_spec, slab_spec,
            pl.BlockSpec((1, 1, 1, WIDTH_A), lambda bi, si: (bi, si, 0, 0)),
        ],
        out_shape=[slab, slab, slab,
                   jax.ShapeDtypeStruct((b, nb, 1, WIDTH_A), F32)],
        compiler_params=pltpu.CompilerParams(
            dimension_semantics=("arbitrary", "arbitrary"),
            vmem_limit_bytes=VMEM_LIMIT),
        name="qkv_proj",
    )(x, w_qkv)


def _attn_kernel(q_ref, k_ref, v_ref, km_ref, bias_ref, o_ref, qa_sc, m_sc, acc_sc):
    i = pl.program_id(2)
    t = MOBA_BLOCK
    lane = lax.broadcasted_iota(jnp.int32, (t, LANES), 1)
    blk_of_lane = lane - HEAD_DIM
    past = (blk_of_lane >= 0) & (blk_of_lane < i)
    nt_dims = (((1,), (1,)), ((), ()))

    def scores(h, j):
        kk = k_ref[0, h, pl.ds(pl.multiple_of(j * t, t), t), :]
        return lax.dot_general(qa_sc[h], kk, nt_dims, preferred_element_type=F32)

    def values(h, j):
        return v_ref[0, h, pl.ds(pl.multiple_of(j * t, t), t), :]

    for h in range(2):
        q = q_ref[0, h]
        gate = jnp.dot(q, km_ref[0, h], preferred_element_type=F32)
        g = jnp.where(past, gate, NEG_INF)
        sel = jnp.zeros((t, LANES), jnp.bool_)
        for _ in range(MOBA_TOPK):
            mx = jnp.max(g, axis=1, keepdims=True)
            first = jnp.min(jnp.where(g == mx, lane, 4 * LANES), axis=1, keepdims=True)
            pick = lane == first
            sel = sel | pick
            g = jnp.where(pick, -jnp.inf, g)
        sel = (sel & past) | (blk_of_lane == i)
        selbias = jnp.where(sel, 0.0, NEG_INF).astype(BF16)
        qa_sc[h] = jnp.where(lane < HEAD_DIM, q, selbias)

        s = scores(h, i) + bias_ref[h, 0]
        m = jnp.max(s, axis=1, keepdims=True)
        p = jnp.exp(s - m)
        m_sc[h] = m
        acc_sc[h] = jnp.dot(p.astype(BF16), values(h, i), preferred_element_type=F32)

    def update(h, j, bias):
        s = scores(h, j)
        if bias is not None:
            s = s + bias
        m_old = m_sc[h]
        m_new = jnp.maximum(m_old, jnp.max(s, axis=1, keepdims=True))
        alpha = jnp.exp(m_old - m_new)
        p = jnp.exp(s - m_new)
        pv = jnp.dot(p.astype(BF16), values(h, j), preferred_element_type=F32)
        acc_sc[h] = alpha * acc_sc[h] + pv
        m_sc[h] = m_new

    @pl.when(i >= 1)
    def _():
        for h in range(2):
            update(h, i - 1, bias_ref[h, 1])

    def far_body(j, carry):
        for h in range(2):
            update(h, j, None)
        return carry

    lax.fori_loop(0, jnp.maximum(i - 1, 0), far_body, 0)

    acc0 = acc_sc[0]
    acc1 = acc_sc[1]
    o0 = acc0 / pltpu.roll(acc0, HEAD_DIM, 1)
    o1 = acc1 / pltpu.roll(acc1, HEAD_DIM, 1)
    o_ref[0] = jnp.where(lane < HEAD_DIM, o0, o1).astype(o_ref.dtype)


def _moba_attention(q_aug, k_aug, v_aug, km, bias_tiles):
    b, h, s, _ = q_aug.shape
    t = MOBA_BLOCK
    pair = lambda bi, hp, qi: (bi, hp, 0, 0)
    return pl.pallas_call(
        _attn_kernel,
        grid=(b, h // 2, s // t),
        in_specs=[
            pl.BlockSpec((1, 2, t, LANES), lambda bi, hp, qi: (bi, hp, qi, 0)),
            pl.BlockSpec((1, 2, s, LANES), pair),
            pl.BlockSpec((1, 2, s, LANES), pair),
            pl.BlockSpec((1, 2, LANES, LANES), pair),
            pl.BlockSpec((2, 2, t, t), lambda bi, hp, qi: (hp, 0, 0, 0)),
        ],
        out_specs=pl.BlockSpec((1, t, LANES), lambda bi, hp, qi: (bi, qi, hp)),
        out_shape=jax.ShapeDtypeStruct((b, s, WIDTH_A), BF16),
        scratch_shapes=[
            pltpu.VMEM((2, t, LANES), BF16),
            pltpu.VMEM((2, t, 1), F32),
            pltpu.VMEM((2, t, LANES), F32),
        ],
        compiler_params=pltpu.CompilerParams(
            dimension_semantics=("arbitrary", "arbitrary", "arbitrary"),
            vmem_limit_bytes=VMEM_LIMIT),
        name="moba_attn",
    )(q_aug, k_aug, v_aug, km, bias_tiles)


def _rel_bucket_np(dist):
    n = jnp.maximum(dist, 0)
    max_exact = REL_BUCKETS // 2
    nf = jnp.maximum(n, 1).astype(F32)
    large = max_exact + (jnp.log(nf / max_exact) / math.log(REL_MAX_DIST / max_exact)
                         * (REL_BUCKETS - max_exact)).astype(jnp.int32)
    large = jnp.minimum(large, REL_BUCKETS - 1)
    return jnp.where(n < max_exact, n, large)


def _bias_tiles(rel_table):
    t = MOBA_BLOCK
    qi = jnp.arange(t)[:, None]
    kj = jnp.arange(t)[None, :]
    table_t = rel_table.T
    far = table_t[:, _rel_bucket_np(jnp.array(2 * t, jnp.int32))]
    own = table_t[:, _rel_bucket_np(qi - kj)] - far[:, None, None]
    own = jnp.where((qi - kj) >= 0, own, NEG_INF)
    prev = table_t[:, _rel_bucket_np(t + qi - kj)] - far[:, None, None]
    return jnp.stack([own, prev], axis=1).astype(F32)


def _mix_kernel(x_ref, a_ref, wpg_ref, bg_ref, wpool_ref, pscale_ref, woa_ref, wob_ref,
                wout_ref, g_ref, b_ref, h_ref, pbuf):
    si = pl.program_id(1)
    x = x_ref[0]
    t = x.shape[0]
    pg = jnp.dot(x.astype(BF16), wpg_ref[...], preferred_element_type=F32)
    p = pg[:, :WIDTH_B]

    @pl.when(si == 0)
    def _():
        pbuf[0:HALO, :] = jnp.zeros((HALO, WIDTH_B), F32)

    @pl.when(si > 0)
    def _():
        pbuf[0:HALO, :] = pbuf[t:t + HALO, :]

    pbuf[HALO:HALO + t, :] = p

    pos = si * t + lax.broadcasted_iota(jnp.int32, (t, 1), 0)
    mixed = []
    for gi, w in enumerate(POOL_WINDOWS):
        c0 = gi * POOL_GROUP_DIM
        pgp = p[:, c0:c0 + POOL_GROUP_DIM]
        ws = pgp
        for dlt in range(1, w):
            ws = ws + pbuf[HALO - dlt:HALO - dlt + t, c0:c0 + POOL_GROUP_DIM]
        cnt = jnp.minimum(pos + 1, w).astype(F32)
        pooled = ws / cnt - pgp
        mixed.append(jnp.dot(pooled.astype(BF16), wpool_ref[gi],
                             preferred_element_type=F32))
    mixed = jnp.concatenate(mixed, axis=1) * pscale_ref[...]
    branch_b = jnp.dot(mixed.astype(BF16), wob_ref[...], preferred_element_type=F32)
    branch_a = jnp.dot(a_ref[0], woa_ref[...], preferred_element_type=F32)
    gates = jax.nn.sigmoid(pg[:, WIDTH_B:] + bg_ref[...])
    mixin = gates[:, :D_MODEL] * branch_a + gates[:, D_MODEL:] * branch_b
    mix = jnp.dot(mixin.astype(BF16), wout_ref[...], preferred_element_type=F32)
    h_ref[0] = _layer_norm(DEEPNORM_ALPHA * x + mix, g_ref[...], b_ref[...])


def _mix_ln1(x, a, w_pg, b_gate, w_pool, pool_scale, w_oa, w_ob, w_out, ln_g, ln_b, t=256):
    b, s, d = x.shape
    const2 = lambda bi, si: (0, 0)
    return pl.pallas_call(
        _mix_kernel,
        grid=(b, s // t),
        in_specs=[
            pl.BlockSpec((1, t, d), lambda bi, si: (bi, si, 0)),
            pl.BlockSpec((1, t, WIDTH_A), lambda bi, si: (bi, si, 0)),
            pl.BlockSpec(w_pg.shape, const2),
            pl.BlockSpec(b_gate.shape, const2),
            pl.BlockSpec(w_pool.shape, lambda bi, si: (0, 0, 0)),
            pl.BlockSpec(pool_scale.shape, const2),
            pl.BlockSpec(w_oa.shape, const2),
            pl.BlockSpec(w_ob.shape, const2),
            pl.BlockSpec(w_out.shape, const2),
            pl.BlockSpec(ln_g.shape, const2),
            pl.BlockSpec(ln_b.shape, const2),
        ],
        out_specs=pl.BlockSpec((1, t, d), lambda bi, si: (bi, si, 0)),
        out_shape=jax.ShapeDtypeStruct((b, s, d), F32),
        scratch_shapes=[pltpu.VMEM((HALO + t, WIDTH_B), F32)],
        compiler_params=pltpu.CompilerParams(
            dimension_semantics=("arbitrary", "arbitrary"),
            vmem_limit_bytes=VMEM_LIMIT),
        name="mix_ln1",
    )(x, a, w_pg, b_gate, w_pool, pool_scale, w_oa, w_ob, w_out, ln_g, ln_b)


FF_CHUNK = 1024


def _ffn_kernel(h_ref, w1_ref, b1_ref, w2_ref, b2_ref, g_ref, b_ref, o_ref):
    h = h_ref[...]
    hb = h.astype(BF16)
    ff = jnp.zeros(h.shape, F32)
    for c in range(D_FF // FF_CHUNK):
        cs = slice(c * FF_CHUNK, (c + 1) * FF_CHUNK)
        u = jnp.dot(hb, w1_ref[:, cs], preferred_element_type=F32) + b1_ref[:, cs]
        u = jnp.square(jnp.maximum(u, 0.0))
        ff = ff + jnp.dot(u.astype(BF16), w2_ref[cs, :], preferred_element_type=F32)
    ff = ff + b2_ref[...]
    o_ref[...] = _layer_norm(DEEPNORM_ALPHA * h + ff, g_ref[...], b_ref[...])


def _ffn_ln2(h, w1, b1, w2, b2, ln_g, ln_b, t=512):
    n, d = h.shape
    const2 = lambda i: (0, 0)
    return pl.pallas_call(
        _ffn_kernel,
        grid=(n // t,),
        in_specs=[
            pl.BlockSpec((t, d), lambda i: (i, 0)),
            pl.BlockSpec(w1.shape, const2),
            pl.BlockSpec(b1.shape, const2),
            pl.BlockSpec(w2.shape, const2),
            pl.BlockSpec(b2.shape, const2),
            pl.BlockSpec(ln_g.shape, const2),
            pl.BlockSpec(ln_b.shape, const2),
        ],
        out_specs=pl.BlockSpec((t, d), lambda i: (i, 0)),
        out_shape=jax.ShapeDtypeStruct((n, d), F32),
        compiler_params=pltpu.CompilerParams(
            dimension_semantics=("arbitrary",),
            vmem_limit_bytes=VMEM_LIMIT),
        name="ffn_ln2",
    )(h, w1, b1, w2, b2, ln_g, ln_b)


def kernel(x, w_in, b_gate, rel_table, w_pool, pool_scale, w_o_attn, w_o_pool, w_out,
           ln1_g, ln1_b, w_ff1, b_ff1, w_ff2, b_ff2, ln2_g, ln2_b):
    b, s, d = x.shape
    nb = s // MOBA_BLOCK
    bias_tiles = _bias_tiles(rel_table)
    h = x
    for l in range(w_in.shape[0]):
        w_qkv = w_in[l, :, :3 * WIDTH_A].astype(BF16)
        w_pg = w_in[l, :, 3 * WIDTH_A:].astype(BF16)
        q_aug, k_aug, v_aug, kmean = _qkv_proj(h, w_qkv)
        km = kmean.reshape(b, nb, N_HEADS, HEAD_DIM).transpose(0, 2, 3, 1)
        km = jnp.pad(km, ((0, 0), (0, 0), (0, LANES - HEAD_DIM),
                          (HEAD_DIM, LANES - HEAD_DIM - nb))).astype(BF16)
        a = _moba_attention(q_aug, k_aug, v_aug, km, bias_tiles)
        h = _mix_ln1(h, a, w_pg, b_gate[l][None, :], w_pool[l].astype(BF16),
                     pool_scale[l][None, :], w_o_attn[l].astype(BF16),
                     w_o_pool[l].astype(BF16), w_out[l].astype(BF16),
                     ln1_g[l][None, :], ln1_b[l][None, :])
        h = _ffn_ln2(h.reshape(b * s, d), w_ff1[l].astype(BF16), b_ff1[l][None, :],
                     w_ff2[l].astype(BF16), b_ff2[l][None, :],
                     ln2_g[l][None, :], ln2_b[l][None, :]).reshape(b, s, d)
    return h
```

```python
import math

import jax
import jax.numpy as jnp
from jax import lax
from jax.experimental import pallas as pl
from jax.experimental.pallas import tpu as pltpu

D_MODEL = 1024
N_HEADS = 8
HEAD_DIM = 64
WIDTH_A = N_HEADS * HEAD_DIM
MOBA_BLOCK = 256
MOBA_TOPK = 3
N_POOL_GROUPS = 4
POOL_WINDOWS = (2, 4, 8, 16)
WIDTH_B = 512
POOL_GROUP_DIM = WIDTH_B // N_POOL_GROUPS
D_FF = 4 * D_MODEL
REL_BUCKETS = 32
REL_MAX_DIST = 128
DEEPNORM_ALPHA = 2.0 ** 0.25
LN_EPS = 1e-5
NEG_INF = -1e30

LANES = 128
MAX_KEY_BLOCKS = 32
HEADS_PER_STEP = 4
HALO = 16
VMEM_LIMIT = 48 * 1024 * 1024

BF16 = jnp.bfloat16
F32 = jnp.float32
NT_DIMS = (((1,), (1,)), ((), ()))


def _layer_norm(y, g, b):
    mu = jnp.mean(y, axis=-1, keepdims=True)
    d = y - mu
    var = jnp.mean(d * d, axis=-1, keepdims=True)
    return d * lax.rsqrt(var + LN_EPS) * g + b


def _qkv_kernel(x_ref, wk_ref, wqvt_ref, q_ref, k_ref, v_ref, kmean_ref):
    blk = pl.program_id(1)
    xb = x_ref[0].astype(BF16)
    t = xb.shape[0]
    acc_k = jnp.dot(xb, wk_ref[...], preferred_element_type=F32)
    acc_t = lax.dot_general(wqvt_ref[...], xb, NT_DIMS,
                            preferred_element_type=F32)
    lane = lax.broadcasted_iota(jnp.int32, (t, LANES), 1)
    lo = lane < HEAD_DIM
    onehot = jnp.where(lane == HEAD_DIM + blk, 1.0, 0.0)
    for hp in range(N_HEADS // 2):
        kp = acc_k[:, hp * LANES:(hp + 1) * LANES]
        kr = pltpu.roll(kp, HEAD_DIM, 1)
        k_ref[0, 2 * hp] = jnp.where(lo, kp, onehot).astype(BF16)
        k_ref[0, 2 * hp + 1] = jnp.where(lo, kr, onehot).astype(BF16)
    pad0 = jnp.zeros((LANES - HEAD_DIM, t), BF16)
    pad1 = jnp.ones((LANES - HEAD_DIM, t), BF16)
    for h in range(N_HEADS):
        q_ref[0, h, 0:HEAD_DIM, :] = acc_t[h * HEAD_DIM:(h + 1) * HEAD_DIM].astype(BF16)
        q_ref[0, h, HEAD_DIM:LANES, :] = pad0
        v_ref[0, h, 0:HEAD_DIM, :] = acc_t[
            WIDTH_A + h * HEAD_DIM:WIDTH_A + (h + 1) * HEAD_DIM].astype(BF16)
        v_ref[0, h, HEAD_DIM:LANES, :] = pad1
    kmean_ref[0, 0] = jnp.mean(acc_k, axis=0, keepdims=True)


def _qkv_proj(x, w_k, w_qvt):
    b, s, d = x.shape
    t = MOBA_BLOCK
    nb = s // t
    return pl.pallas_call(
        _qkv_kernel,
        grid=(b, nb),
        in_specs=[
            pl.BlockSpec((1, t, d), lambda bi, si: (bi, si, 0)),
            pl.BlockSpec(w_k.shape, lambda bi, si: (0, 0)),
            pl.BlockSpec(w_qvt.shape, lambda bi, si: (0, 0)),
        ],
        out_specs=[
            pl.BlockSpec((1, N_HEADS, LANES, t), lambda bi, si: (bi, 0, 0, si)),
            pl.BlockSpec((1, N_HEADS, t, LANES), lambda bi, si: (bi, 0, si, 0)),
            pl.BlockSpec((1, N_HEADS, LANES, t), lambda bi, si: (bi, 0, 0, si)),
            pl.BlockSpec((1, 1, 1, WIDTH_A), lambda bi, si: (bi, si, 0, 0)),
        ],
        out_shape=[
            jax.ShapeDtypeStruct((b, N_HEADS, LANES, s), BF16),
            jax.ShapeDtypeStruct((b, N_HEADS, s, LANES), BF16),
            jax.ShapeDtypeStruct((b, N_HEADS, LANES, s), BF16),
            jax.ShapeDtypeStruct((b, nb, 1, WIDTH_A), F32),
        ],
        compiler_params=pltpu.CompilerParams(
            dimension_semantics=("arbitrary", "arbitrary"),
            vmem_limit_bytes=VMEM_LIMIT),
        name="qkv_proj",
    )(x, w_k, w_qvt)


def _attn_kernel(q_ref, k_ref, v_ref, km_ref, bias_ref, o_ref, qa_sc, m_sc, acc_sc):
    i = pl.program_id(2)
    t = MOBA_BLOCK
    heads = range(HEADS_PER_STEP)
    blk_row = lax.broadcasted_iota(jnp.int32, (MAX_KEY_BLOCKS, t), 0)
    past = blk_row < i

    def scores(h, j):
        kk = k_ref[0, h, pl.ds(pl.multiple_of(j * t, t), t), :]
        return jnp.dot(kk, qa_sc[h], preferred_element_type=F32)

    def weighted_values(h, j, p):
        vv = v_ref[0, h, :, pl.ds(pl.multiple_of(j * t, t), t)]
        return jnp.dot(vv, p.astype(BF16), preferred_element_type=F32)

    for h in heads:
        qt = q_ref[0, h]
        gate = jnp.dot(km_ref[0, h], qt, preferred_element_type=F32)
        g = jnp.where(past, gate[HEAD_DIM:HEAD_DIM + MAX_KEY_BLOCKS], NEG_INF)
        sel = jnp.zeros(g.shape, jnp.bool_)
        for _ in range(MOBA_TOPK):
            mx = jnp.max(g, axis=0, keepdims=True)
            first = jnp.min(jnp.where(g == mx, blk_row, MAX_KEY_BLOCKS), axis=0, keepdims=True)
            pick = blk_row == first
            sel = sel | pick
            g = jnp.where(pick, -jnp.inf, g)
        sel = (sel & past) | (blk_row == i)
        qa_sc[h, 0:HEAD_DIM, :] = qt[0:HEAD_DIM]
        qa_sc[h, HEAD_DIM:HEAD_DIM + MAX_KEY_BLOCKS, :] = jnp.where(sel, 0.0, NEG_INF).astype(BF16)
        qa_sc[h, HEAD_DIM + MAX_KEY_BLOCKS:LANES, :] = jnp.zeros(
            (LANES - HEAD_DIM - MAX_KEY_BLOCKS, t), BF16)

    for h in heads:
        s = scores(h, i) + bias_ref[h, 0]
        m = jnp.max(s, axis=0, keepdims=True)
        m_sc[h] = m
        acc_sc[h] = weighted_values(h, i, jnp.exp(s - m))

    def update(j, ms, bias_idx):
        ss = [scores(h, j) for h in heads]
        if bias_idx is not None:
            ss = [ss[h] + bias_ref[h, bias_idx] for h in heads]
        new_ms, alphas, ps = [], [], []
        for h in heads:
            m_new = jnp.maximum(ms[h], jnp.max(ss[h], axis=0, keepdims=True))
            alphas.append(jnp.exp(ms[h] - m_new))
            ps.append(jnp.exp(ss[h] - m_new))
            new_ms.append(m_new)
        for h in heads:
            acc_sc[h] = alphas[h] * acc_sc[h] + weighted_values(h, j, ps[h])
        return new_ms

    @pl.when(i >= 1)
    def _():
        new_ms = update(i - 1, [m_sc[h] for h in heads], 1)
        for h in heads:
            m_sc[h] = new_ms[h]

    lax.fori_loop(0, jnp.maximum(i - 1, 0),
                  lambda j, ms: tuple(update(j, list(ms), None)),
                  tuple(m_sc[h] for h in heads))

    for hp in range(HEADS_PER_STEP // 2):
        outs = []
        for h in (2 * hp, 2 * hp + 1):
            acc = acc_sc[h]
            outs.append(acc[0:HEAD_DIM] / acc[HEAD_DIM:LANES])
        o_ref[0, :, hp * LANES:(hp + 1) * LANES] = jnp.concatenate(outs, axis=0).T.astype(o_ref.dtype)


def _moba_attention(q_t, k_aug, v_t, km, bias_tiles):
    b, h, s, _ = k_aug.shape
    t = MOBA_BLOCK
    g = HEADS_PER_STEP
    group = lambda bi, hg, qi: (bi, hg, 0, 0)
    once = pl.Buffered(1)
    return pl.pallas_call(
        _attn_kernel,
        grid=(b, h // g, s // t),
        in_specs=[
            pl.BlockSpec((1, g, LANES, t), lambda bi, hg, qi: (bi, hg, 0, qi)),
            pl.BlockSpec((1, g, s, LANES), group, pipeline_mode=once),
            pl.BlockSpec((1, g, LANES, s), group, pipeline_mode=once),
            pl.BlockSpec((1, g, LANES, LANES), group),
            pl.BlockSpec((g, 2, t, t), lambda bi, hg, qi: (hg, 0, 0, 0), pipeline_mode=once),
        ],
        out_specs=pl.BlockSpec((1, t, g * HEAD_DIM), lambda bi, hg, qi: (bi, qi, hg)),
        out_shape=jax.ShapeDtypeStruct((b, s, WIDTH_A), BF16),
        scratch_shapes=[
            pltpu.VMEM((g, LANES, t), BF16),
            pltpu.VMEM((g, 1, t), F32),
            pltpu.VMEM((g, LANES, t), F32),
        ],
        compiler_params=pltpu.CompilerParams(
            dimension_semantics=("arbitrary", "arbitrary", "arbitrary"),
            vmem_limit_bytes=VMEM_LIMIT),
        name="moba_attn",
    )(q_t, k_aug, v_t, km, bias_tiles)


def _rel_bucket(dist):
    n = jnp.maximum(dist, 0)
    max_exact = REL_BUCKETS // 2
    nf = jnp.maximum(n, 1).astype(F32)
    large = max_exact + (jnp.log(nf / max_exact) / math.log(REL_MAX_DIST / max_exact)
                         * (REL_BUCKETS - max_exact)).astype(jnp.int32)
    large = jnp.minimum(large, REL_BUCKETS - 1)
    return jnp.where(n < max_exact, n, large)


def _bias_tiles(rel_table):
    t = MOBA_BLOCK
    kj = jnp.arange(t)[:, None]
    qi = jnp.arange(t)[None, :]
    dist = jnp.stack([qi - kj, t + qi - kj])
    onehot = _rel_bucket(dist)[None] == jnp.arange(REL_BUCKETS)[:, None, None, None]
    far = rel_table[_rel_bucket(jnp.array(2 * t, jnp.int32))]
    shifted = (rel_table - far[None, :]).T
    tiles = jnp.sum(jnp.where(onehot[None], shifted[:, :, None, None, None], 0.0), axis=1)
    causal = (dist >= 0)[None]
    return jnp.where(causal, tiles, NEG_INF).astype(F32)


def _mix_kernel(x_ref, a_ref, wpg_ref, bg_ref, wpool_ref, pscale_ref, woa_ref, wob_ref,
                wout_ref, g_ref, b_ref, h_ref, pbuf):
    si = pl.program_id(1)
    x = x_ref[0]
    t = x.shape[0]
    pg = jnp.dot(x.astype(BF16), wpg_ref[...], preferred_element_type=F32)
    p = pg[:, :WIDTH_B]

    @pl.when(si == 0)
    def _():
        pbuf[0:HALO, :] = jnp.zeros((HALO, WIDTH_B), F32)

    @pl.when(si > 0)
    def _():
        pbuf[0:HALO, :] = pbuf[t:t + HALO, :]

    pbuf[HALO:HALO + t, :] = p

    pos = si * t + lax.broadcasted_iota(jnp.int32, (t, 1), 0)
    mixed = []
    for gi, w in enumerate(POOL_WINDOWS):
        c0 = gi * POOL_GROUP_DIM
        pgp = p[:, c0:c0 + POOL_GROUP_DIM]
        ws = pgp
        for dlt in range(1, w):
            ws = ws + pbuf[HALO - dlt:HALO - dlt + t, c0:c0 + POOL_GROUP_DIM]
        cnt = jnp.minimum(pos + 1, w).astype(F32)
        pooled = ws / cnt - pgp
        mixed.append(jnp.dot(pooled.astype(BF16), wpool_ref[gi],
                             preferred_element_type=F32))
    mixed = jnp.concatenate(mixed, axis=1) * pscale_ref[...]
    branch_b = jnp.dot(mixed.astype(BF16), wob_ref[...], preferred_element_type=F32)
    branch_a = jnp.dot(a_ref[0], woa_ref[...], preferred_element_type=F32)
    gates = jax.nn.sigmoid(pg[:, WIDTH_B:] + bg_ref[...])
    mixin = gates[:, :D_MODEL] * branch_a + gates[:, D_MODEL:] * branch_b
    mix = jnp.dot(mixin.astype(BF16), wout_ref[...], preferred_element_type=F32)
    h_ref[0] = _layer_norm(DEEPNORM_ALPHA * x + mix, g_ref[...], b_ref[...])


def _mix_ln1(x, a, w_pg, b_gate, w_pool, pool_scale, w_oa, w_ob, w_out, ln_g, ln_b, t=256):
    b, s, d = x.shape
    const2 = lambda bi, si: (0, 0)
    return pl.pallas_call(
        _mix_kernel,
        grid=(b, s // t),
        in_specs=[
            pl.BlockSpec((1, t, d), lambda bi, si: (bi, si, 0)),
            pl.BlockSpec((1, t, WIDTH_A), lambda bi, si: (bi, si, 0)),
            pl.BlockSpec(w_pg.shape, const2),
            pl.BlockSpec(b_gate.shape, const2),
            pl.BlockSpec(w_pool.shape, lambda bi, si: (0, 0, 0)),
            pl.BlockSpec(pool_scale.shape, const2),
            pl.BlockSpec(w_oa.shape, const2),
            pl.BlockSpec(w_ob.shape, const2),
            pl.BlockSpec(w_out.shape, const2),
            pl.BlockSpec(ln_g.shape, const2),
            pl.BlockSpec(ln_b.shape, const2),
        ],
        out_specs=pl.BlockSpec((1, t, d), lambda bi, si: (bi, si, 0)),
        out_shape=jax.ShapeDtypeStruct((b, s, d), F32),
        scratch_shapes=[pltpu.VMEM((HALO + t, WIDTH_B), F32)],
        compiler_params=pltpu.CompilerParams(
            dimension_semantics=("arbitrary", "arbitrary"),
            vmem_limit_bytes=VMEM_LIMIT),
        name="mix_ln1",
    )(x, a, w_pg, b_gate, w_pool, pool_scale, w_oa, w_ob, w_out, ln_g, ln_b)


FF_CHUNK = 1024


def _ffn_kernel(h_ref, w1_ref, b1_ref, w2_ref, b2_ref, g_ref, b_ref, o_ref):
    h = h_ref[...]
    hb = h.astype(BF16)
    ff = jnp.zeros(h.shape, F32)
    for c in range(D_FF // FF_CHUNK):
        cs = slice(c * FF_CHUNK, (c + 1) * FF_CHUNK)
        u = jnp.dot(hb, w1_ref[:, cs], preferred_element_type=F32) + b1_ref[:, cs]
        u = jnp.square(jnp.maximum(u, 0.0))
        ff = ff + jnp.dot(u.astype(BF16), w2_ref[cs, :], preferred_element_type=F32)
    ff = ff + b2_ref[...]
    o_ref[...] = _layer_norm(DEEPNORM_ALPHA * h + ff, g_ref[...], b_ref[...])


def _ffn_ln2(h, w1, b1, w2, b2, ln_g, ln_b, t=512):
    n, d = h.shape
    const2 = lambda i: (0, 0)
    return pl.pallas_call(
        _ffn_kernel,
        grid=(n // t,),
        in_specs=[
            pl.BlockSpec((t, d), lambda i: (i, 0)),
            pl.BlockSpec(w1.shape, const2),
            pl.BlockSpec(b1.shape, const2),
            pl.BlockSpec(w2.shape, const2),
            pl.BlockSpec(b2.shape, const2),
            pl.BlockSpec(ln_g.shape, const2),
            pl.BlockSpec(ln_b.shape, const2),
        ],
        out_specs=pl.BlockSpec((t, d), lambda i: (i, 0)),
        out_shape=jax.ShapeDtypeStruct((n, d), F32),
        compiler_params=pltpu.CompilerParams(
            dimension_semantics=("arbitrary",),
            vmem_limit_bytes=VMEM_LIMIT),
        name="ffn_ln2",
    )(h, w1, b1, w2, b2, ln_g, ln_b)


def kernel(x, w_in, b_gate, rel_table, w_pool, pool_scale, w_o_attn, w_o_pool, w_out,
           ln1_g, ln1_b, w_ff1, b_ff1, w_ff2, b_ff2, ln2_g, ln2_b):
    b, s, d = x.shape
    nb = s // MOBA_BLOCK
    assert s % MOBA_BLOCK == 0 and nb <= MAX_KEY_BLOCKS
    bias_tiles = _bias_tiles(rel_table)
    h = x
    for l in range(w_in.shape[0]):
        w_q = w_in[l, :, :WIDTH_A] * (HEAD_DIM ** -0.5)
        w_k = w_in[l, :, WIDTH_A:2 * WIDTH_A].astype(BF16)
        w_v = w_in[l, :, 2 * WIDTH_A:3 * WIDTH_A]
        w_qvt = jnp.concatenate([w_q, w_v], axis=1).T.astype(BF16)
        w_pg = w_in[l, :, 3 * WIDTH_A:].astype(BF16)
        q_t, k_aug, v_t, kmean = _qkv_proj(h, w_k, w_qvt)
        km = kmean.reshape(b, nb, N_HEADS, HEAD_DIM).transpose(0, 2, 1, 3)
        km = jnp.pad(km, ((0, 0), (0, 0), (HEAD_DIM, LANES - HEAD_DIM - nb),
                          (0, LANES - HEAD_DIM))).astype(BF16)
        a = _moba_attention(q_t, k_aug, v_t, km, bias_tiles)
        h = _mix_ln1(h, a, w_pg, b_gate[l][None, :], w_pool[l].astype(BF16),
                     pool_scale[l][None, :], w_o_attn[l].astype(BF16),
                     w_o_pool[l].astype(BF16), w_out[l].astype(BF16),
                     ln1_g[l][None, :], ln1_b[l][None, :])
        h = _ffn_ln2(h.reshape(b * s, d), w_ff1[l].astype(BF16), b_ff1[l][None, :],
                     w_ff2[l].astype(BF16), b_ff2[l][None, :],
                     ln2_g[l][None, :], ln2_b[l][None, :]).reshape(b, s, d)
    return h
```

```python
import math

import jax
import jax.numpy as jnp
from jax import lax
from jax.experimental import pallas as pl
from jax.experimental.pallas import tpu as pltpu

D_MODEL = 1024
N_HEADS = 8
HEAD_DIM = 64
WIDTH_A = N_HEADS * HEAD_DIM
MOBA_BLOCK = 256
MOBA_TOPK = 3
N_POOL_GROUPS = 4
POOL_WINDOWS = (2, 4, 8, 16)
WIDTH_B = 512
POOL_GROUP_DIM = WIDTH_B // N_POOL_GROUPS
D_FF = 4 * D_MODEL
REL_BUCKETS = 32
REL_MAX_DIST = 128
DEEPNORM_ALPHA = 2.0 ** 0.25
LN_EPS = 1e-5
NEG_INF = -1e30

LANES = 128
MAX_KEY_BLOCKS = 32
HEADS_PER_STEP = 4
V_ROWS = HEAD_DIM + 16
LOG2E = math.log2(math.e)
HALO = 16
VMEM_LIMIT = 48 * 1024 * 1024

BF16 = jnp.bfloat16
F32 = jnp.float32
NT_DIMS = (((1,), (1,)), ((), ()))


def _layer_norm(y, g, b):
    mu = jnp.mean(y, axis=-1, keepdims=True)
    d = y - mu
    var = jnp.mean(d * d, axis=-1, keepdims=True)
    return d * lax.rsqrt(var + LN_EPS) * g + b


def _qkv_kernel(x_ref, wk_ref, wqvt_ref, q_ref, k_ref, v_ref, kmean_ref):
    blk = pl.program_id(1)
    xb = x_ref[0].astype(BF16)
    t = xb.shape[0]
    acc_k = jnp.dot(xb, wk_ref[...], preferred_element_type=F32)
    acc_t = lax.dot_general(wqvt_ref[...], xb, NT_DIMS,
                            preferred_element_type=F32)
    lane = lax.broadcasted_iota(jnp.int32, (t, LANES), 1)
    lo = lane < HEAD_DIM
    onehot = jnp.where(lane == HEAD_DIM + blk, 1.0, 0.0)
    for hp in range(N_HEADS // 2):
        kp = acc_k[:, hp * LANES:(hp + 1) * LANES]
        kr = pltpu.roll(kp, HEAD_DIM, 1)
        k_ref[0, 2 * hp] = jnp.where(lo, kp, onehot).astype(BF16)
        k_ref[0, 2 * hp + 1] = jnp.where(lo, kr, onehot).astype(BF16)
    pad0 = jnp.zeros((LANES - HEAD_DIM, t), BF16)
    pad1 = jnp.ones((V_ROWS - HEAD_DIM, t), BF16)
    for h in range(N_HEADS):
        q_ref[0, h, 0:HEAD_DIM, :] = acc_t[h * HEAD_DIM:(h + 1) * HEAD_DIM].astype(BF16)
        q_ref[0, h, HEAD_DIM:LANES, :] = pad0
        v_ref[0, h, 0:HEAD_DIM, :] = acc_t[
            WIDTH_A + h * HEAD_DIM:WIDTH_A + (h + 1) * HEAD_DIM].astype(BF16)
        v_ref[0, h, HEAD_DIM:V_ROWS, :] = pad1
    kmean_ref[0, 0] = jnp.mean(acc_k, axis=0, keepdims=True)


def _qkv_proj(x, w_k, w_qvt):
    b, s, d = x.shape
    t = MOBA_BLOCK
    nb = s // t
    return pl.pallas_call(
        _qkv_kernel,
        grid=(b, nb),
        in_specs=[
            pl.BlockSpec((1, t, d), lambda bi, si: (bi, si, 0)),
            pl.BlockSpec(w_k.shape, lambda bi, si: (0, 0)),
            pl.BlockSpec(w_qvt.shape, lambda bi, si: (0, 0)),
        ],
        out_specs=[
            pl.BlockSpec((1, N_HEADS, LANES, t), lambda bi, si: (bi, 0, 0, si)),
            pl.BlockSpec((1, N_HEADS, t, LANES), lambda bi, si: (bi, 0, si, 0)),
            pl.BlockSpec((1, N_HEADS, V_ROWS, t), lambda bi, si: (bi, 0, 0, si)),
            pl.BlockSpec((1, 1, 1, WIDTH_A), lambda bi, si: (bi, si, 0, 0)),
        ],
        out_shape=[
            jax.ShapeDtypeStruct((b, N_HEADS, LANES, s), BF16),
            jax.ShapeDtypeStruct((b, N_HEADS, s, LANES), BF16),
            jax.ShapeDtypeStruct((b, N_HEADS, V_ROWS, s), BF16),
            jax.ShapeDtypeStruct((b, nb, 1, WIDTH_A), F32),
        ],
        compiler_params=pltpu.CompilerParams(
            dimension_semantics=("arbitrary", "arbitrary"),
            vmem_limit_bytes=VMEM_LIMIT),
        name="qkv_proj",
    )(x, w_k, w_qvt)


def _attn_kernel(q_ref, k_ref, v_ref, km_ref, bias_ref, o_ref, qa_sc, s_a, s_b, acc_sc):
    i = pl.program_id(2)
    t = MOBA_BLOCK
    heads = range(HEADS_PER_STEP)
    blk_row = lax.broadcasted_iota(jnp.int32, (MAX_KEY_BLOCKS, t), 0)
    past = blk_row < i

    def scores(h, j):
        kk = k_ref[0, h, pl.ds(pl.multiple_of(j * t, t), t), :]
        return jnp.dot(kk, qa_sc[h], preferred_element_type=F32)

    def weighted_values(h, j, p):
        vv = v_ref[0, h, :, pl.ds(pl.multiple_of(j * t, t), t)]
        return jnp.dot(vv, p.astype(BF16), preferred_element_type=F32)

    def step(j, ms, cur, nxt, bias_idx=None):
        if nxt is not None:
            for h in heads:
                nxt[h] = scores(h, j + 1)
        new_ms = []
        for h in heads:
            s = cur[h]
            if bias_idx is not None:
                s = s + bias_ref[h, bias_idx]
            m_new = jnp.maximum(ms[h], jnp.max(s, axis=0, keepdims=True))
            alpha = jnp.exp2(ms[h] - m_new)
            pv = weighted_values(h, j, jnp.exp2(s - m_new))
            acc_sc[h] = alpha * acc_sc[h] + pv
            new_ms.append(m_new)
        return new_ms

    for h in heads:
        qt = q_ref[0, h]
        gate = jnp.dot(km_ref[0, h], qt, preferred_element_type=F32)
        g = jnp.where(past, gate[HEAD_DIM:HEAD_DIM + MAX_KEY_BLOCKS], NEG_INF)
        sel = jnp.zeros(g.shape, jnp.bool_)
        for _ in range(MOBA_TOPK):
            mx = jnp.max(g, axis=0, keepdims=True)
            first = jnp.min(jnp.where(g == mx, blk_row, MAX_KEY_BLOCKS), axis=0, keepdims=True)
            pick = blk_row == first
            sel = sel | pick
            g = jnp.where(pick, -jnp.inf, g)
        sel = (sel & past) | (blk_row == i)
        qa_sc[h, 0:HEAD_DIM, :] = qt[0:HEAD_DIM]
        qa_sc[h, HEAD_DIM:HEAD_DIM + MAX_KEY_BLOCKS, :] = jnp.where(sel, 0.0, NEG_INF).astype(BF16)
        qa_sc[h, HEAD_DIM + MAX_KEY_BLOCKS:LANES, :] = jnp.zeros(
            (LANES - HEAD_DIM - MAX_KEY_BLOCKS, t), BF16)

    ms = []
    for h in heads:
        s = scores(h, i) + bias_ref[h, 0]
        m = jnp.max(s, axis=0, keepdims=True)
        acc_sc[h] = weighted_values(h, i, jnp.exp2(s - m))
        ms.append(m)

    @pl.when(i >= 1)
    def _():
        for h in heads:
            s_a[h] = scores(h, 0)

    n_far = jnp.maximum(i - 1, 0)

    def two_steps(jj, ms):
        ms = step(2 * jj, list(ms), s_a, s_b)
        return tuple(step(2 * jj + 1, ms, s_b, s_a))

    ms = lax.fori_loop(0, n_far // 2, two_steps, tuple(ms))
    odd = n_far % 2 == 1

    @pl.when((i >= 1) & jnp.logical_not(odd))
    def _():
        step(i - 1, list(ms), s_a, None, bias_idx=1)

    @pl.when(odd)
    def _():
        step(i - 1, step(i - 2, list(ms), s_a, s_b), s_b, None, bias_idx=1)

    for hp in range(HEADS_PER_STEP // 2):
        outs = []
        for h in (2 * hp, 2 * hp + 1):
            acc = acc_sc[h]
            outs.append(acc[0:HEAD_DIM] / acc[HEAD_DIM:HEAD_DIM + 1])
        o_ref[0, :, hp * LANES:(hp + 1) * LANES] = jnp.concatenate(outs, axis=0).T.astype(o_ref.dtype)


def _moba_attention(q_t, k_aug, v_t, km, bias_tiles):
    b, h, s, _ = k_aug.shape
    t = MOBA_BLOCK
    g = HEADS_PER_STEP
    group = lambda bi, hg, qi: (bi, hg, 0, 0)
    once = pl.Buffered(1)
    return pl.pallas_call(
        _attn_kernel,
        grid=(b, h // g, s // t),
        in_specs=[
            pl.BlockSpec((1, g, LANES, t), lambda bi, hg, qi: (bi, hg, 0, qi)),
            pl.BlockSpec((1, g, s, LANES), group, pipeline_mode=once),
            pl.BlockSpec((1, g, V_ROWS, s), group, pipeline_mode=once),
            pl.BlockSpec((1, g, LANES, LANES), group),
            pl.BlockSpec((g, 2, t, t), lambda bi, hg, qi: (hg, 0, 0, 0), pipeline_mode=once),
        ],
        out_specs=pl.BlockSpec((1, t, g * HEAD_DIM), lambda bi, hg, qi: (bi, qi, hg)),
        out_shape=jax.ShapeDtypeStruct((b, s, WIDTH_A), BF16),
        scratch_shapes=[
            pltpu.VMEM((g, LANES, t), BF16),
            pltpu.VMEM((g, t, t), F32),
            pltpu.VMEM((g, t, t), F32),
            pltpu.VMEM((g, V_ROWS, t), F32),
        ],
        compiler_params=pltpu.CompilerParams(
            dimension_semantics=("arbitrary", "arbitrary", "arbitrary"),
            vmem_limit_bytes=VMEM_LIMIT),
        name="moba_attn",
    )(q_t, k_aug, v_t, km, bias_tiles)


def _rel_bucket(dist):
    n = jnp.maximum(dist, 0)
    max_exact = REL_BUCKETS // 2
    nf = jnp.maximum(n, 1).astype(F32)
    large = max_exact + (jnp.log(nf / max_exact) / math.log(REL_MAX_DIST / max_exact)
                         * (REL_BUCKETS - max_exact)).astype(jnp.int32)
    large = jnp.minimum(large, REL_BUCKETS - 1)
    return jnp.where(n < max_exact, n, large)


def _bias_tiles(rel_table):
    t = MOBA_BLOCK
    kj = jnp.arange(t)[:, None]
    qi = jnp.arange(t)[None, :]
    dist = jnp.stack([qi - kj, t + qi - kj])
    onehot = _rel_bucket(dist)[None] == jnp.arange(REL_BUCKETS)[:, None, None, None]
    far = rel_table[_rel_bucket(jnp.array(2 * t, jnp.int32))]
    shifted = (rel_table - far[None, :]).T
    tiles = jnp.sum(jnp.where(onehot[None], shifted[:, :, None, None, None], 0.0), axis=1)
    causal = (dist >= 0)[None]
    return jnp.where(causal, tiles * LOG2E, NEG_INF).astype(F32)


def _mix_kernel(x_ref, a_ref, wpg_ref, bg_ref, wpool_ref, pscale_ref, woa_ref, wob_ref,
                wout_ref, g_ref, b_ref, h_ref, pbuf):
    si = pl.program_id(1)
    x = x_ref[0]
    t = x.shape[0]
    pg = jnp.dot(x.astype(BF16), wpg_ref[...], preferred_element_type=F32)
    p = pg[:, :WIDTH_B]

    @pl.when(si == 0)
    def _():
        pbuf[0:HALO, :] = jnp.zeros((HALO, WIDTH_B), F32)

    @pl.when(si > 0)
    def _():
        pbuf[0:HALO, :] = pbuf[t:t + HALO, :]

    pbuf[HALO:HALO + t, :] = p

    pos = si * t + lax.broadcasted_iota(jnp.int32, (t, 1), 0)
    mixed = []
    for gi, w in enumerate(POOL_WINDOWS):
        c0 = gi * POOL_GROUP_DIM
        pgp = p[:, c0:c0 + POOL_GROUP_DIM]
        ws = pgp
        for dlt in range(1, w):
            ws = ws + pbuf[HALO - dlt:HALO - dlt + t, c0:c0 + POOL_GROUP_DIM]
        cnt = jnp.minimum(pos + 1, w).astype(F32)
        pooled = ws / cnt - pgp
        mixed.append(jnp.dot(pooled.astype(BF16), wpool_ref[gi],
                             preferred_element_type=F32))
    mixed = jnp.concatenate(mixed, axis=1) * pscale_ref[...]
    branch_b = jnp.dot(mixed.astype(BF16), wob_ref[...], preferred_element_type=F32)
    branch_a = jnp.dot(a_ref[0], woa_ref[...], preferred_element_type=F32)
    gates = jax.nn.sigmoid(pg[:, WIDTH_B:] + bg_ref[...])
    mixin = gates[:, :D_MODEL] * branch_a + gates[:, D_MODEL:] * branch_b
    mix = jnp.dot(mixin.astype(BF16), wout_ref[...], preferred_element_type=F32)
    h_ref[0] = _layer_norm(DEEPNORM_ALPHA * x + mix, g_ref[...], b_ref[...])


def _mix_ln1(x, a, w_pg, b_gate, w_pool, pool_scale, w_oa, w_ob, w_out, ln_g, ln_b, t=256):
    b, s, d = x.shape
    const2 = lambda bi, si: (0, 0)
    return pl.pallas_call(
        _mix_kernel,
        grid=(b, s // t),
        in_specs=[
            pl.BlockSpec((1, t, d), lambda bi, si: (bi, si, 0)),
            pl.BlockSpec((1, t, WIDTH_A), lambda bi, si: (bi, si, 0)),
            pl.BlockSpec(w_pg.shape, const2),
            pl.BlockSpec(b_gate.shape, const2),
            pl.BlockSpec(w_pool.shape, lambda bi, si: (0, 0, 0)),
            pl.BlockSpec(pool_scale.shape, const2),
            pl.BlockSpec(w_oa.shape, const2),
            pl.BlockSpec(w_ob.shape, const2),
            pl.BlockSpec(w_out.shape, const2),
            pl.BlockSpec(ln_g.shape, const2),
            pl.BlockSpec(ln_b.shape, const2),
        ],
        out_specs=pl.BlockSpec((1, t, d), lambda bi, si: (bi, si, 0)),
        out_shape=jax.ShapeDtypeStruct((b, s, d), F32),
        scratch_shapes=[pltpu.VMEM((HALO + t, WIDTH_B), F32)],
        compiler_params=pltpu.CompilerParams(
            dimension_semantics=("arbitrary", "arbitrary"),
            vmem_limit_bytes=VMEM_LIMIT),
        name="mix_ln1",
    )(x, a, w_pg, b_gate, w_pool, pool_scale, w_oa, w_ob, w_out, ln_g, ln_b)


FF_CHUNK = 1024


def _ffn_kernel(h_ref, w1_ref, b1_ref, w2_ref, b2_ref, g_ref, b_ref, o_ref):
    h = h_ref[...]
    hb = h.astype(BF16)
    ff = jnp.zeros(h.shape, F32)
    for c in range(D_FF // FF_CHUNK):
        cs = slice(c * FF_CHUNK, (c + 1) * FF_CHUNK)
        u = jnp.dot(hb, w1_ref[:, cs], preferred_element_type=F32) + b1_ref[:, cs]
        u = jnp.square(jnp.maximum(u, 0.0))
        ff = ff + jnp.dot(u.astype(BF16), w2_ref[cs, :], preferred_element_type=F32)
    ff = ff + b2_ref[...]
    o_ref[...] = _layer_norm(DEEPNORM_ALPHA * h + ff, g_ref[...], b_ref[...])


def _ffn_ln2(h, w1, b1, w2, b2, ln_g, ln_b, t=512):
    n, d = h.shape
    const2 = lambda i: (0, 0)
    return pl.pallas_call(
        _ffn_kernel,
        grid=(n // t,),
        in_specs=[
            pl.BlockSpec((t, d), lambda i: (i, 0)),
            pl.BlockSpec(w1.shape, const2),
            pl.BlockSpec(b1.shape, const2),
            pl.BlockSpec(w2.shape, const2),
            pl.BlockSpec(b2.shape, const2),
            pl.BlockSpec(ln_g.shape, const2),
            pl.BlockSpec(ln_b.shape, const2),
        ],
        out_specs=pl.BlockSpec((t, d), lambda i: (i, 0)),
        out_shape=jax.ShapeDtypeStruct((n, d), F32),
        compiler_params=pltpu.CompilerParams(
            dimension_semantics=("arbitrary",),
            vmem_limit_bytes=VMEM_LIMIT),
        name="ffn_ln2",
    )(h, w1, b1, w2, b2, ln_g, ln_b)


def kernel(x, w_in, b_gate, rel_table, w_pool, pool_scale, w_o_attn, w_o_pool, w_out,
           ln1_g, ln1_b, w_ff1, b_ff1, w_ff2, b_ff2, ln2_g, ln2_b):
    b, s, d = x.shape
    nb = s // MOBA_BLOCK
    assert s % MOBA_BLOCK == 0 and nb <= MAX_KEY_BLOCKS
    bias_tiles = _bias_tiles(rel_table)
    h = x
    for l in range(w_in.shape[0]):
        w_q = w_in[l, :, :WIDTH_A] * (HEAD_DIM ** -0.5 * LOG2E)
        w_k = w_in[l, :, WIDTH_A:2 * WIDTH_A].astype(BF16)
        w_v = w_in[l, :, 2 * WIDTH_A:3 * WIDTH_A]
        w_qvt = jnp.concatenate([w_q, w_v], axis=1).T.astype(BF16)
        w_pg = w_in[l, :, 3 * WIDTH_A:].astype(BF16)
        q_t, k_aug, v_t, kmean = _qkv_proj(h, w_k, w_qvt)
        km = kmean.reshape(b, nb, N_HEADS, HEAD_DIM).transpose(0, 2, 1, 3)
        km = jnp.pad(km, ((0, 0), (0, 0), (HEAD_DIM, LANES - HEAD_DIM - nb),
                          (0, LANES - HEAD_DIM))).astype(BF16)
        a = _moba_attention(q_t, k_aug, v_t, km, bias_tiles)
        h = _mix_ln1(h, a, w_pg, b_gate[l][None, :], w_pool[l].astype(BF16),
                     pool_scale[l][None, :], w_o_attn[l].astype(BF16),
                     w_o_pool[l].astype(BF16), w_out[l].astype(BF16),
                     ln1_g[l][None, :], ln1_b[l][None, :])
        h = _ffn_ln2(h.reshape(b * s, d), w_ff1[l].astype(BF16), b_ff1[l][None, :],
                     w_ff2[l].astype(BF16), b_ff2[l][None, :],
                     ln2_g[l][None, :], ln2_b[l][None, :]).reshape(b, s, d)
    return h
```

```python
import math

import jax
import jax.numpy as jnp
from jax import lax
from jax.experimental import pallas as pl
from jax.experimental.pallas import tpu as pltpu

D_MODEL = 1024
N_HEADS = 8
HEAD_DIM = 64
WIDTH_A = N_HEADS * HEAD_DIM
MOBA_BLOCK = 256
MOBA_TOPK = 3
N_POOL_GROUPS = 4
POOL_WINDOWS = (2, 4, 8, 16)
WIDTH_B = 512
POOL_GROUP_DIM = WIDTH_B // N_POOL_GROUPS
D_FF = 4 * D_MODEL
REL_BUCKETS = 32
REL_MAX_DIST = 128
DEEPNORM_ALPHA = 2.0 ** 0.25
LN_EPS = 1e-5
NEG_INF = -1e30

LANES = 128
MAX_KEY_BLOCKS = 32
HEADS_PER_STEP = 8
V_ROWS = HEAD_DIM + 16
LOG2E = math.log2(math.e)
HALO = 16
VMEM_LIMIT = 48 * 1024 * 1024

BF16 = jnp.bfloat16
F32 = jnp.float32
NT_DIMS = (((1,), (1,)), ((), ()))


def _layer_norm(y, g, b):
    mu = jnp.mean(y, axis=-1, keepdims=True)
    d = y - mu
    var = jnp.mean(d * d, axis=-1, keepdims=True)
    return d * lax.rsqrt(var + LN_EPS) * g + b


def _qkv_kernel(x_ref, wk_ref, wqvt_ref, q_ref, k_ref, v_ref, kmean_ref):
    blk = pl.program_id(1)
    xb = x_ref[0].astype(BF16)
    t = xb.shape[0]
    acc_k = jnp.dot(xb, wk_ref[...], preferred_element_type=F32)
    acc_t = lax.dot_general(wqvt_ref[...], xb, NT_DIMS,
                            preferred_element_type=F32)
    lane = lax.broadcasted_iota(jnp.int32, (t, LANES), 1)
    lo = lane < HEAD_DIM
    onehot = jnp.where(lane == HEAD_DIM + blk, 1.0, 0.0)
    for hp in range(N_HEADS // 2):
        kp = acc_k[:, hp * LANES:(hp + 1) * LANES] * LOG2E
        kr = pltpu.roll(kp, HEAD_DIM, 1)
        k_ref[0, 2 * hp] = jnp.where(lo, kp, onehot).astype(BF16)
        k_ref[0, 2 * hp + 1] = jnp.where(lo, kr, onehot).astype(BF16)
    pad0 = jnp.zeros((LANES - HEAD_DIM, t), BF16)
    pad1 = jnp.ones((V_ROWS - HEAD_DIM, t), BF16)
    for h in range(N_HEADS):
        q_ref[0, h, 0:HEAD_DIM, :] = acc_t[h * HEAD_DIM:(h + 1) * HEAD_DIM].astype(BF16)
        q_ref[0, h, HEAD_DIM:LANES, :] = pad0
        v_ref[0, h, 0:HEAD_DIM, :] = acc_t[
            WIDTH_A + h * HEAD_DIM:WIDTH_A + (h + 1) * HEAD_DIM].astype(BF16)
        v_ref[0, h, HEAD_DIM:V_ROWS, :] = pad1
    kmean_ref[0, 0] = jnp.mean(acc_k, axis=0, keepdims=True)


def _qkv_proj(x, w_k, w_qvt):
    b, s, d = x.shape
    t = MOBA_BLOCK
    nb = s // t
    return pl.pallas_call(
        _qkv_kernel,
        grid=(b, nb),
        in_specs=[
            pl.BlockSpec((1, t, d), lambda bi, si: (bi, si, 0)),
            pl.BlockSpec(w_k.shape, lambda bi, si: (0, 0)),
            pl.BlockSpec(w_qvt.shape, lambda bi, si: (0, 0)),
        ],
        out_specs=[
            pl.BlockSpec((1, N_HEADS, LANES, t), lambda bi, si: (bi, 0, 0, si)),
            pl.BlockSpec((1, N_HEADS, t, LANES), lambda bi, si: (bi, 0, si, 0)),
            pl.BlockSpec((1, N_HEADS, V_ROWS, t), lambda bi, si: (bi, 0, 0, si)),
            pl.BlockSpec((1, 1, 1, WIDTH_A), lambda bi, si: (bi, si, 0, 0)),
        ],
        out_shape=[
            jax.ShapeDtypeStruct((b, N_HEADS, LANES, s), BF16),
            jax.ShapeDtypeStruct((b, N_HEADS, s, LANES), BF16),
            jax.ShapeDtypeStruct((b, N_HEADS, V_ROWS, s), BF16),
            jax.ShapeDtypeStruct((b, nb, 1, WIDTH_A), F32),
        ],
        compiler_params=pltpu.CompilerParams(
            dimension_semantics=("arbitrary", "arbitrary"),
            vmem_limit_bytes=VMEM_LIMIT),
        name="qkv_proj",
    )(x, w_k, w_qvt)


def _attn_kernel(q_ref, k_ref, v_ref, km_ref, bias_ref, o_ref, qa_sc, s_a, s_b, acc_sc):
    i = pl.program_id(2)
    t = MOBA_BLOCK
    heads = range(HEADS_PER_STEP)
    blk_row = lax.broadcasted_iota(jnp.int32, (MAX_KEY_BLOCKS, t), 0)
    past = blk_row < i

    def key_block(h, j):
        return k_ref[0, h, pl.ds(pl.multiple_of(j * t, t), t), :]

    def scores_into(s_dst, j):
        for h in heads:
            s_dst[h] = jnp.dot(key_block(h, j), qa_sc[h], preferred_element_type=F32)

    def weighted_values(h, j, p):
        vv = v_ref[0, h, :, pl.ds(pl.multiple_of(j * t, t), t)]
        return jnp.dot(vv, p.astype(BF16), preferred_element_type=F32)

    def step(j, ms, s_cur, s_nxt, bias_idx=None):
        if s_nxt is not None:
            scores_into(s_nxt, j + 1)
        new_ms = []
        for h in heads:
            s = s_cur[h]
            if bias_idx is not None:
                s = s + bias_ref[h, bias_idx]
            m_new = jnp.max(s, axis=0, keepdims=True)
            if ms is None:
                acc_sc[h] = weighted_values(h, j, jnp.exp2(s - m_new))
            else:
                m_new = jnp.maximum(ms[h], m_new)
                alpha = jnp.exp2(ms[h] - m_new)
                acc_sc[h] = alpha * acc_sc[h] + weighted_values(h, j, jnp.exp2(s - m_new))
            new_ms.append(m_new)
        return new_ms

    for h in heads:
        s_b[h] = jnp.dot(key_block(h, i), q_ref[0, h], preferred_element_type=F32)

    for h in heads:
        qt = q_ref[0, h]
        gate = jnp.dot(km_ref[0, h], qt, preferred_element_type=F32)
        g = jnp.where(past, gate[HEAD_DIM:HEAD_DIM + MAX_KEY_BLOCKS], NEG_INF)
        sel = jnp.zeros(g.shape, jnp.bool_)
        for _ in range(MOBA_TOPK):
            mx = jnp.max(g, axis=0, keepdims=True)
            first = jnp.min(jnp.where(g == mx, blk_row, MAX_KEY_BLOCKS), axis=0, keepdims=True)
            pick = blk_row == first
            sel = sel | pick
            g = jnp.where(pick, -jnp.inf, g)
        sel = (sel & past) | (blk_row == i)
        qa_sc[h, 0:HEAD_DIM, :] = qt[0:HEAD_DIM]
        qa_sc[h, HEAD_DIM:HEAD_DIM + MAX_KEY_BLOCKS, :] = jnp.where(sel, 0.0, NEG_INF).astype(BF16)
        qa_sc[h, HEAD_DIM + MAX_KEY_BLOCKS:LANES, :] = jnp.zeros(
            (LANES - HEAD_DIM - MAX_KEY_BLOCKS, t), BF16)

    scores_into(s_a, 0)

    ms = step(i, None, s_b, None, bias_idx=0)

    n_far = jnp.maximum(i - 1, 0)

    def two_steps(jj, ms):
        ms = step(2 * jj, list(ms), s_a, s_b)
        return tuple(step(2 * jj + 1, ms, s_b, s_a))

    ms = lax.fori_loop(0, n_far // 2, two_steps, tuple(ms))
    odd = n_far % 2 == 1

    @pl.when((i >= 1) & jnp.logical_not(odd))
    def _():
        step(i - 1, list(ms), s_a, None, bias_idx=1)

    @pl.when(odd)
    def _():
        step(i - 1, step(i - 2, list(ms), s_a, s_b), s_b, None, bias_idx=1)

    for hp in range(HEADS_PER_STEP // 2):
        outs = []
        for h in (2 * hp, 2 * hp + 1):
            acc = acc_sc[h]
            outs.append(acc[0:HEAD_DIM] / acc[HEAD_DIM:HEAD_DIM + 1])
        o_ref[0, :, hp * LANES:(hp + 1) * LANES] = jnp.concatenate(outs, axis=0).T.astype(o_ref.dtype)


def _moba_attention(q_t, k_aug, v_t, km, bias_tiles):
    b, h, s, _ = k_aug.shape
    t = MOBA_BLOCK
    g = HEADS_PER_STEP
    group = lambda bi, hg, qi: (bi, hg, 0, 0)
    once = pl.Buffered(1)
    return pl.pallas_call(
        _attn_kernel,
        grid=(b, h // g, s // t),
        in_specs=[
            pl.BlockSpec((1, g, LANES, t), lambda bi, hg, qi: (bi, hg, 0, qi)),
            pl.BlockSpec((1, g, s, LANES), group, pipeline_mode=once),
            pl.BlockSpec((1, g, V_ROWS, s), group, pipeline_mode=once),
            pl.BlockSpec((1, g, LANES, LANES), group),
            pl.BlockSpec((g, 2, t, t), lambda bi, hg, qi: (hg, 0, 0, 0), pipeline_mode=once),
        ],
        out_specs=pl.BlockSpec((1, t, g * HEAD_DIM), lambda bi, hg, qi: (bi, qi, hg)),
        out_shape=jax.ShapeDtypeStruct((b, s, WIDTH_A), BF16),
        scratch_shapes=[
            pltpu.VMEM((g, LANES, t), BF16),
            pltpu.VMEM((g, t, t), F32),
            pltpu.VMEM((g, t, t), F32),
            pltpu.VMEM((g, V_ROWS, t), F32),
        ],
        compiler_params=pltpu.CompilerParams(
            dimension_semantics=("arbitrary", "arbitrary", "arbitrary"),
            vmem_limit_bytes=VMEM_LIMIT),
        name="moba_attn",
    )(q_t, k_aug, v_t, km, bias_tiles)


def _rel_bucket(dist):
    n = jnp.maximum(dist, 0)
    max_exact = REL_BUCKETS // 2
    nf = jnp.maximum(n, 1).astype(F32)
    large = max_exact + (jnp.log(nf / max_exact) / math.log(REL_MAX_DIST / max_exact)
                         * (REL_BUCKETS - max_exact)).astype(jnp.int32)
    large = jnp.minimum(large, REL_BUCKETS - 1)
    return jnp.where(n < max_exact, n, large)


def _bias_tiles(rel_table):
    t = MOBA_BLOCK
    kj = jnp.arange(t)[:, None]
    qi = jnp.arange(t)[None, :]
    dist = jnp.stack([qi - kj, t + qi - kj])
    onehot = _rel_bucket(dist)[None] == jnp.arange(REL_BUCKETS)[:, None, None, None]
    far = rel_table[_rel_bucket(jnp.array(2 * t, jnp.int32))]
    shifted = (rel_table - far[None, :]).T
    tiles = jnp.sum(jnp.where(onehot[None], shifted[:, :, None, None, None], 0.0), axis=1)
    causal = (dist >= 0)[None]
    return jnp.where(causal, tiles * LOG2E, NEG_INF).astype(F32)


def _mix_kernel(x_ref, a_ref, wpg_ref, bg_ref, wpool_ref, pscale_ref, woa_ref, wob_ref,
                wout_ref, g_ref, b_ref, h_ref, pbuf):
    si = pl.program_id(1)
    x = x_ref[0]
    t = x.shape[0]
    pg = jnp.dot(x.astype(BF16), wpg_ref[...], preferred_element_type=F32)
    p = pg[:, :WIDTH_B]

    @pl.when(si == 0)
    def _():
        pbuf[0:HALO, :] = jnp.zeros((HALO, WIDTH_B), F32)

    @pl.when(si > 0)
    def _():
        pbuf[0:HALO, :] = pbuf[t:t + HALO, :]

    pbuf[HALO:HALO + t, :] = p

    pos = si * t + lax.broadcasted_iota(jnp.int32, (t, 1), 0)
    mixed = []
    for gi, w in enumerate(POOL_WINDOWS):
        c0 = gi * POOL_GROUP_DIM
        pgp = p[:, c0:c0 + POOL_GROUP_DIM]
        ws = pgp
        for dlt in range(1, w):
            ws = ws + pbuf[HALO - dlt:HALO - dlt + t, c0:c0 + POOL_GROUP_DIM]
        cnt = jnp.minimum(pos + 1, w).astype(F32)
        pooled = ws / cnt - pgp
        mixed.append(jnp.dot(pooled.astype(BF16), wpool_ref[gi],
                             preferred_element_type=F32))
    mixed = jnp.concatenate(mixed, axis=1) * pscale_ref[...]
    branch_b = jnp.dot(mixed.astype(BF16), wob_ref[...], preferred_element_type=F32)
    branch_a = jnp.dot(a_ref[0], woa_ref[...], preferred_element_type=F32)
    gates = jax.nn.sigmoid(pg[:, WIDTH_B:] + bg_ref[...])
    mixin = gates[:, :D_MODEL] * branch_a + gates[:, D_MODEL:] * branch_b
    mix = jnp.dot(mixin.astype(BF16), wout_ref[...], preferred_element_type=F32)
    h_ref[0] = _layer_norm(DEEPNORM_ALPHA * x + mix, g_ref[...], b_ref[...])


def _mix_ln1(x, a, w_pg, b_gate, w_pool, pool_scale, w_oa, w_ob, w_out, ln_g, ln_b, t=256):
    b, s, d = x.shape
    const2 = lambda bi, si: (0, 0)
    return pl.pallas_call(
        _mix_kernel,
        grid=(b, s // t),
        in_specs=[
            pl.BlockSpec((1, t, d), lambda bi, si: (bi, si, 0)),
            pl.BlockSpec((1, t, WIDTH_A), lambda bi, si: (bi, si, 0)),
            pl.BlockSpec(w_pg.shape, const2),
            pl.BlockSpec(b_gate.shape, const2),
            pl.BlockSpec(w_pool.shape, lambda bi, si: (0, 0, 0)),
            pl.BlockSpec(pool_scale.shape, const2),
            pl.BlockSpec(w_oa.shape, const2),
            pl.BlockSpec(w_ob.shape, const2),
            pl.BlockSpec(w_out.shape, const2),
            pl.BlockSpec(ln_g.shape, const2),
            pl.BlockSpec(ln_b.shape, const2),
        ],
        out_specs=pl.BlockSpec((1, t, d), lambda bi, si: (bi, si, 0)),
        out_shape=jax.ShapeDtypeStruct((b, s, d), F32),
        scratch_shapes=[pltpu.VMEM((HALO + t, WIDTH_B), F32)],
        compiler_params=pltpu.CompilerParams(
            dimension_semantics=("arbitrary", "arbitrary"),
            vmem_limit_bytes=VMEM_LIMIT),
        name="mix_ln1",
    )(x, a, w_pg, b_gate, w_pool, pool_scale, w_oa, w_ob, w_out, ln_g, ln_b)


FF_CHUNK = 1024


def _ffn_kernel(h_ref, w1_ref, b1_ref, w2_ref, b2_ref, g_ref, b_ref, o_ref):
    h = h_ref[...]
    hb = h.astype(BF16)
    ff = jnp.zeros(h.shape, F32)
    for c in range(D_FF // FF_CHUNK):
        cs = slice(c * FF_CHUNK, (c + 1) * FF_CHUNK)
        u = jnp.dot(hb, w1_ref[:, cs], preferred_element_type=F32) + b1_ref[:, cs]
        u = jnp.square(jnp.maximum(u, 0.0))
        ff = ff + jnp.dot(u.astype(BF16), w2_ref[cs, :], preferred_element_type=F32)
    ff = ff + b2_ref[...]
    o_ref[...] = _layer_norm(DEEPNORM_ALPHA * h + ff, g_ref[...], b_ref[...])


def _ffn_ln2(h, w1, b1, w2, b2, ln_g, ln_b, t=512):
    n, d = h.shape
    const2 = lambda i: (0, 0)
    return pl.pallas_call(
        _ffn_kernel,
        grid=(n // t,),
        in_specs=[
            pl.BlockSpec((t, d), lambda i: (i, 0)),
            pl.BlockSpec(w1.shape, const2),
            pl.BlockSpec(b1.shape, const2),
            pl.BlockSpec(w2.shape, const2),
            pl.BlockSpec(b2.shape, const2),
            pl.BlockSpec(ln_g.shape, const2),
            pl.BlockSpec(ln_b.shape, const2),
        ],
        out_specs=pl.BlockSpec((t, d), lambda i: (i, 0)),
        out_shape=jax.ShapeDtypeStruct((n, d), F32),
        compiler_params=pltpu.CompilerParams(
            dimension_semantics=("arbitrary",),
            vmem_limit_bytes=VMEM_LIMIT),
        name="ffn_ln2",
    )(h, w1, b1, w2, b2, ln_g, ln_b)


def kernel(x, w_in, b_gate, rel_table, w_pool, pool_scale, w_o_attn, w_o_pool, w_out,
           ln1_g, ln1_b, w_ff1, b_ff1, w_ff2, b_ff2, ln2_g, ln2_b):
    b, s, d = x.shape
    nb = s // MOBA_BLOCK
    assert s % MOBA_BLOCK == 0 and nb <= MAX_KEY_BLOCKS
    bias_tiles = _bias_tiles(rel_table)
    h = x
    for l in range(w_in.shape[0]):
        w_q = w_in[l, :, :WIDTH_A] * (HEAD_DIM ** -0.5)
        w_k = w_in[l, :, WIDTH_A:2 * WIDTH_A].astype(BF16)
        w_v = w_in[l, :, 2 * WIDTH_A:3 * WIDTH_A]
        w_qvt = jnp.concatenate([w_q, w_v], axis=1).T.astype(BF16)
        w_pg = w_in[l, :, 3 * WIDTH_A:].astype(BF16)
        q_t, k_aug, v_t, kmean = _qkv_proj(h, w_k, w_qvt)
        km = kmean.reshape(b, nb, N_HEADS, HEAD_DIM).transpose(0, 2, 1, 3)
        km = jnp.pad(km, ((0, 0), (0, 0), (HEAD_DIM, LANES - HEAD_DIM - nb),
                          (0, LANES - HEAD_DIM))).astype(BF16)
        a = _moba_attention(q_t, k_aug, v_t, km, bias_tiles)
        h = _mix_ln1(h, a, w_pg, b_gate[l][None, :], w_pool[l].astype(BF16),
                     pool_scale[l][None, :], w_o_attn[l].astype(BF16),
                     w_o_pool[l].astype(BF16), w_out[l].astype(BF16),
                     ln1_g[l][None, :], ln1_b[l][None, :])
        h = _ffn_ln2(h.reshape(b * s, d), w_ff1[l].astype(BF16), b_ff1[l][None, :],
                     w_ff2[l].astype(BF16), b_ff2[l][None, :],
                     ln2_g[l][None, :], ln2_b[l][None, :]).reshape(b, s, d)
    return h
```

```python
import math

import jax
import jax.numpy as jnp
from jax import lax
from jax.experimental import pallas as pl
from jax.experimental.pallas import tpu as pltpu

D_MODEL = 1024
N_HEADS = 8
HEAD_DIM = 64
WIDTH_A = N_HEADS * HEAD_DIM
MOBA_BLOCK = 256
MOBA_TOPK = 3
N_POOL_GROUPS = 4
POOL_WINDOWS = (2, 4, 8, 16)
WIDTH_B = 512
POOL_GROUP_DIM = WIDTH_B // N_POOL_GROUPS
D_FF = 4 * D_MODEL
REL_BUCKETS = 32
REL_MAX_DIST = 128
DEEPNORM_ALPHA = 2.0 ** 0.25
LN_EPS = 1e-5
NEG_INF = -1e30

LANES = 128
MAX_KEY_BLOCKS = 32
HEADS_PER_STEP = 8
V_ROWS = HEAD_DIM + 16
LOG2E = math.log2(math.e)
HALO = 16
VMEM_LIMIT = 48 * 1024 * 1024

BF16 = jnp.bfloat16
F32 = jnp.float32
NT_DIMS = (((1,), (1,)), ((), ()))


def _layer_norm(y, g, b):
    mu = jnp.mean(y, axis=-1, keepdims=True)
    d = y - mu
    var = jnp.mean(d * d, axis=-1, keepdims=True)
    return d * lax.rsqrt(var + LN_EPS) * g + b


def _qkv_kernel(x_ref, wk_ref, wqvt_ref, q_ref, k_ref, v_ref, kmean_ref):
    blk = pl.program_id(1)
    xb = x_ref[0].astype(BF16)
    t = xb.shape[0]
    acc_k = jnp.dot(xb, wk_ref[...], preferred_element_type=F32)
    acc_t = lax.dot_general(wqvt_ref[...], xb, NT_DIMS,
                            preferred_element_type=F32)
    lane = lax.broadcasted_iota(jnp.int32, (t, LANES), 1)
    lo = lane < HEAD_DIM
    onehot = jnp.where(lane == HEAD_DIM + blk, 1.0, 0.0)
    for hp in range(N_HEADS // 2):
        kp = acc_k[:, hp * LANES:(hp + 1) * LANES] * LOG2E
        kr = pltpu.roll(kp, HEAD_DIM, 1)
        k_ref[0, 2 * hp] = jnp.where(lo, kp, onehot).astype(BF16)
        k_ref[0, 2 * hp + 1] = jnp.where(lo, kr, onehot).astype(BF16)
    pad0 = jnp.zeros((LANES - HEAD_DIM, t), BF16)
    pad1 = jnp.ones((V_ROWS - HEAD_DIM, t), BF16)
    for h in range(N_HEADS):
        q_ref[0, h, 0:HEAD_DIM, :] = acc_t[h * HEAD_DIM:(h + 1) * HEAD_DIM].astype(BF16)
        q_ref[0, h, HEAD_DIM:LANES, :] = pad0
        v_ref[0, h, 0:HEAD_DIM, :] = acc_t[
            WIDTH_A + h * HEAD_DIM:WIDTH_A + (h + 1) * HEAD_DIM].astype(BF16)
        v_ref[0, h, HEAD_DIM:V_ROWS, :] = pad1
    kmean_ref[0, 0] = jnp.mean(acc_k, axis=0, keepdims=True)


def _qkv_proj(x, w_k, w_qvt):
    b, s, d = x.shape
    t = MOBA_BLOCK
    nb = s // t
    return pl.pallas_call(
        _qkv_kernel,
        grid=(b, nb),
        in_specs=[
            pl.BlockSpec((1, t, d), lambda bi, si: (bi, si, 0)),
            pl.BlockSpec(w_k.shape, lambda bi, si: (0, 0)),
            pl.BlockSpec(w_qvt.shape, lambda bi, si: (0, 0)),
        ],
        out_specs=[
            pl.BlockSpec((1, N_HEADS, LANES, t), lambda bi, si: (bi, 0, 0, si)),
            pl.BlockSpec((1, N_HEADS, t, LANES), lambda bi, si: (bi, 0, si, 0)),
            pl.BlockSpec((1, N_HEADS, V_ROWS, t), lambda bi, si: (bi, 0, 0, si)),
            pl.BlockSpec((1, 1, 1, WIDTH_A), lambda bi, si: (bi, si, 0, 0)),
        ],
        out_shape=[
            jax.ShapeDtypeStruct((b, N_HEADS, LANES, s), BF16),
            jax.ShapeDtypeStruct((b, N_HEADS, s, LANES), BF16),
            jax.ShapeDtypeStruct((b, N_HEADS, V_ROWS, s), BF16),
            jax.ShapeDtypeStruct((b, nb, 1, WIDTH_A), F32),
        ],
        compiler_params=pltpu.CompilerParams(
            dimension_semantics=("arbitrary", "arbitrary"),
            vmem_limit_bytes=VMEM_LIMIT),
        name="qkv_proj",
    )(x, w_k, w_qvt)


def _attn_kernel(q_ref, k_ref, v_ref, km_ref, bias_ref, o_ref, qa_sc, s_a, s_b, acc_sc):
    i = pl.program_id(2)
    t = MOBA_BLOCK
    heads = range(HEADS_PER_STEP)
    blk_row = lax.broadcasted_iota(jnp.int32, (MAX_KEY_BLOCKS, t), 0)
    past = blk_row < i

    def key_block(h, j):
        return k_ref[0, h, pl.ds(pl.multiple_of(j * t, t), t), :]

    def head_scores_into(s_dst, h, j):
        s = jnp.dot(key_block(h, j), qa_sc[h], preferred_element_type=F32)
        s_dst[h] = s
        return jnp.max(s, axis=0, keepdims=True)

    def scores_into(s_dst, j):
        return [head_scores_into(s_dst, h, j) for h in heads]

    def weighted_values(h, j, p):
        vv = v_ref[0, h, :, pl.ds(pl.multiple_of(j * t, t), t)]
        return jnp.dot(vv, p.astype(BF16), preferred_element_type=F32)

    def step(j, ms, s_cur, max_cur, s_nxt, bias_idx=None):
        new_ms, max_nxt = [], []
        for h in heads:
            if s_nxt is not None:
                max_nxt.append(head_scores_into(s_nxt, h, j + 1))
            s = s_cur[h]
            if bias_idx is not None:
                s = s + bias_ref[h, bias_idx]
                m_new = jnp.max(s, axis=0, keepdims=True)
            else:
                m_new = max_cur[h]
            if ms is None:
                acc_sc[h] = weighted_values(h, j, jnp.exp2(s - m_new))
            else:
                m_new = jnp.maximum(ms[h], m_new)
                alpha = jnp.exp2(ms[h] - m_new)
                acc_sc[h] = alpha * acc_sc[h] + weighted_values(h, j, jnp.exp2(s - m_new))
            new_ms.append(m_new)
        return new_ms, max_nxt

    for h in heads:
        s_b[h] = jnp.dot(key_block(h, i), q_ref[0, h], preferred_element_type=F32)

    for h in heads:
        qt = q_ref[0, h]
        gate = jnp.dot(km_ref[0, h], qt, preferred_element_type=F32)
        g = jnp.where(past, gate[HEAD_DIM:HEAD_DIM + MAX_KEY_BLOCKS], NEG_INF)
        sel = jnp.zeros(g.shape, jnp.bool_)
        for _ in range(MOBA_TOPK):
            mx = jnp.max(g, axis=0, keepdims=True)
            first = jnp.min(jnp.where(g == mx, blk_row, MAX_KEY_BLOCKS), axis=0, keepdims=True)
            pick = blk_row == first
            sel = sel | pick
            g = jnp.where(pick, -jnp.inf, g)
        sel = (sel & past) | (blk_row == i)
        qa_sc[h, 0:HEAD_DIM, :] = qt[0:HEAD_DIM]
        qa_sc[h, HEAD_DIM:HEAD_DIM + MAX_KEY_BLOCKS, :] = jnp.where(sel, 0.0, NEG_INF).astype(BF16)
        qa_sc[h, HEAD_DIM + MAX_KEY_BLOCKS:LANES, :] = jnp.zeros(
            (LANES - HEAD_DIM - MAX_KEY_BLOCKS, t), BF16)

    max_a = scores_into(s_a, 0)

    ms, _ = step(i, None, s_b, None, None, bias_idx=0)

    n_far = jnp.maximum(i - 1, 0)

    def two_steps(jj, carry):
        ms, max_a = carry
        ms, max_b = step(2 * jj, list(ms), s_a, list(max_a), s_b)
        ms, max_a = step(2 * jj + 1, ms, s_b, max_b, s_a)
        return tuple(ms), tuple(max_a)

    ms, max_a = lax.fori_loop(0, n_far // 2, two_steps, (tuple(ms), tuple(max_a)))
    odd = n_far % 2 == 1

    @pl.when((i >= 1) & jnp.logical_not(odd))
    def _():
        step(i - 1, list(ms), s_a, None, None, bias_idx=1)

    @pl.when(odd)
    def _():
        ms1, _ = step(i - 2, list(ms), s_a, list(max_a), s_b)
        step(i - 1, ms1, s_b, None, None, bias_idx=1)

    for hp in range(HEADS_PER_STEP // 2):
        outs = []
        for h in (2 * hp, 2 * hp + 1):
            acc = acc_sc[h]
            outs.append(acc[0:HEAD_DIM] / acc[HEAD_DIM:HEAD_DIM + 1])
        o_ref[0, :, hp * LANES:(hp + 1) * LANES] = jnp.concatenate(outs, axis=0).T.astype(o_ref.dtype)


def _moba_attention(q_t, k_aug, v_t, km, bias_tiles):
    b, h, s, _ = k_aug.shape
    t = MOBA_BLOCK
    g = HEADS_PER_STEP
    group = lambda bi, hg, qi: (bi, hg, 0, 0)
    once = pl.Buffered(1)
    return pl.pallas_call(
        _attn_kernel,
        grid=(b, h // g, s // t),
        in_specs=[
            pl.BlockSpec((1, g, LANES, t), lambda bi, hg, qi: (bi, hg, 0, qi)),
            pl.BlockSpec((1, g, s, LANES), group, pipeline_mode=once),
            pl.BlockSpec((1, g, V_ROWS, s), group, pipeline_mode=once),
            pl.BlockSpec((1, g, LANES, LANES), group),
            pl.BlockSpec((g, 2, t, t), lambda bi, hg, qi: (hg, 0, 0, 0), pipeline_mode=once),
        ],
        out_specs=pl.BlockSpec((1, t, g * HEAD_DIM), lambda bi, hg, qi: (bi, qi, hg)),
        out_shape=jax.ShapeDtypeStruct((b, s, WIDTH_A), BF16),
        scratch_shapes=[
            pltpu.VMEM((g, LANES, t), BF16),
            pltpu.VMEM((g, t, t), F32),
            pltpu.VMEM((g, t, t), F32),
            pltpu.VMEM((g, V_ROWS, t), F32),
        ],
        compiler_params=pltpu.CompilerParams(
            dimension_semantics=("arbitrary", "arbitrary", "arbitrary"),
            vmem_limit_bytes=VMEM_LIMIT),
        name="moba_attn",
    )(q_t, k_aug, v_t, km, bias_tiles)


def _rel_bucket(dist):
    n = jnp.maximum(dist, 0)
    max_exact = REL_BUCKETS // 2
    nf = jnp.maximum(n, 1).astype(F32)
    large = max_exact + (jnp.log(nf / max_exact) / math.log(REL_MAX_DIST / max_exact)
                         * (REL_BUCKETS - max_exact)).astype(jnp.int32)
    large = jnp.minimum(large, REL_BUCKETS - 1)
    return jnp.where(n < max_exact, n, large)


def _bias_tiles(rel_table):
    t = MOBA_BLOCK
    kj = jnp.arange(t)[:, None]
    qi = jnp.arange(t)[None, :]
    dist = jnp.stack([qi - kj, t + qi - kj])
    onehot = _rel_bucket(dist)[None] == jnp.arange(REL_BUCKETS)[:, None, None, None]
    far = rel_table[_rel_bucket(jnp.array(2 * t, jnp.int32))]
    shifted = (rel_table - far[None, :]).T
    tiles = jnp.sum(jnp.where(onehot[None], shifted[:, :, None, None, None], 0.0), axis=1)
    causal = (dist >= 0)[None]
    return jnp.where(causal, tiles * LOG2E, NEG_INF).astype(F32)


def _mix_kernel(x_ref, a_ref, wpg_ref, bg_ref, wpool_ref, pscale_ref, woa_ref, wob_ref,
                wout_ref, g_ref, b_ref, h_ref, pbuf):
    si = pl.program_id(1)
    x = x_ref[0]
    t = x.shape[0]
    pg = jnp.dot(x.astype(BF16), wpg_ref[...], preferred_element_type=F32)
    p = pg[:, :WIDTH_B]

    @pl.when(si == 0)
    def _():
        pbuf[0:HALO, :] = jnp.zeros((HALO, WIDTH_B), F32)

    @pl.when(si > 0)
    def _():
        pbuf[0:HALO, :] = pbuf[t:t + HALO, :]

    pbuf[HALO:HALO + t, :] = p

    pos = si * t + lax.broadcasted_iota(jnp.int32, (t, 1), 0)
    mixed = []
    for gi, w in enumerate(POOL_WINDOWS):
        c0 = gi * POOL_GROUP_DIM
        pgp = p[:, c0:c0 + POOL_GROUP_DIM]
        ws = pgp
        for dlt in range(1, w):
            ws = ws + pbuf[HALO - dlt:HALO - dlt + t, c0:c0 + POOL_GROUP_DIM]
        cnt = jnp.minimum(pos + 1, w).astype(F32)
        pooled = ws / cnt - pgp
        mixed.append(jnp.dot(pooled.astype(BF16), wpool_ref[gi],
                             preferred_element_type=F32))
    mixed = jnp.concatenate(mixed, axis=1) * pscale_ref[...]
    branch_b = jnp.dot(mixed.astype(BF16), wob_ref[...], preferred_element_type=F32)
    branch_a = jnp.dot(a_ref[0], woa_ref[...], preferred_element_type=F32)
    gates = jax.nn.sigmoid(pg[:, WIDTH_B:] + bg_ref[...])
    mixin = gates[:, :D_MODEL] * branch_a + gates[:, D_MODEL:] * branch_b
    mix = jnp.dot(mixin.astype(BF16), wout_ref[...], preferred_element_type=F32)
    h_ref[0] = _layer_norm(DEEPNORM_ALPHA * x + mix, g_ref[...], b_ref[...])


def _mix_ln1(x, a, w_pg, b_gate, w_pool, pool_scale, w_oa, w_ob, w_out, ln_g, ln_b, t=512):
    b, s, d = x.shape
    const2 = lambda bi, si: (0, 0)
    return pl.pallas_call(
        _mix_kernel,
        grid=(b, s // t),
        in_specs=[
            pl.BlockSpec((1, t, d), lambda bi, si: (bi, si, 0)),
            pl.BlockSpec((1, t, WIDTH_A), lambda bi, si: (bi, si, 0)),
            pl.BlockSpec(w_pg.shape, const2),
            pl.BlockSpec(b_gate.shape, const2),
            pl.BlockSpec(w_pool.shape, lambda bi, si: (0, 0, 0)),
            pl.BlockSpec(pool_scale.shape, const2),
            pl.BlockSpec(w_oa.shape, const2),
            pl.BlockSpec(w_ob.shape, const2),
            pl.BlockSpec(w_out.shape, const2),
            pl.BlockSpec(ln_g.shape, const2),
            pl.BlockSpec(ln_b.shape, const2),
        ],
        out_specs=pl.BlockSpec((1, t, d), lambda bi, si: (bi, si, 0)),
        out_shape=jax.ShapeDtypeStruct((b, s, d), F32),
        scratch_shapes=[pltpu.VMEM((HALO + t, WIDTH_B), F32)],
        compiler_params=pltpu.CompilerParams(
            dimension_semantics=("arbitrary", "arbitrary"),
            vmem_limit_bytes=VMEM_LIMIT),
        name="mix_ln1",
    )(x, a, w_pg, b_gate, w_pool, pool_scale, w_oa, w_ob, w_out, ln_g, ln_b)


FF_CHUNK = 1024


def _ffn_kernel(h_ref, w1_ref, b1_ref, w2_ref, b2_ref, g_ref, b_ref, o_ref):
    h = h_ref[...]
    hb = h.astype(BF16)
    ff = jnp.zeros(h.shape, F32)
    for c in range(D_FF // FF_CHUNK):
        cs = slice(c * FF_CHUNK, (c + 1) * FF_CHUNK)
        u = jnp.dot(hb, w1_ref[:, cs], preferred_element_type=F32) + b1_ref[:, cs]
        u = jnp.square(jnp.maximum(u, 0.0))
        ff = ff + jnp.dot(u.astype(BF16), w2_ref[cs, :], preferred_element_type=F32)
    ff = ff + b2_ref[...]
    o_ref[...] = _layer_norm(DEEPNORM_ALPHA * h + ff, g_ref[...], b_ref[...])


def _ffn_ln2(h, w1, b1, w2, b2, ln_g, ln_b, t=512):
    n, d = h.shape
    const2 = lambda i: (0, 0)
    return pl.pallas_call(
        _ffn_kernel,
        grid=(n // t,),
        in_specs=[
            pl.BlockSpec((t, d), lambda i: (i, 0)),
            pl.BlockSpec(w1.shape, const2),
            pl.BlockSpec(b1.shape, const2),
            pl.BlockSpec(w2.shape, const2),
            pl.BlockSpec(b2.shape, const2),
            pl.BlockSpec(ln_g.shape, const2),
            pl.BlockSpec(ln_b.shape, const2),
        ],
        out_specs=pl.BlockSpec((t, d), lambda i: (i, 0)),
        out_shape=jax.ShapeDtypeStruct((n, d), F32),
        compiler_params=pltpu.CompilerParams(
            dimension_semantics=("arbitrary",),
            vmem_limit_bytes=VMEM_LIMIT),
        name="ffn_ln2",
    )(h, w1, b1, w2, b2, ln_g, ln_b)


def kernel(x, w_in, b_gate, rel_table, w_pool, pool_scale, w_o_attn, w_o_pool, w_out,
           ln1_g, ln1_b, w_ff1, b_ff1, w_ff2, b_ff2, ln2_g, ln2_b):
    b, s, d = x.shape
    nb = s // MOBA_BLOCK
    assert s % MOBA_BLOCK == 0 and nb <= MAX_KEY_BLOCKS
    bias_tiles = _bias_tiles(rel_table)
    h = x
    for l in range(w_in.shape[0]):
        w_q = w_in[l, :, :WIDTH_A] * (HEAD_DIM ** -0.5)
        w_k = w_in[l, :, WIDTH_A:2 * WIDTH_A].astype(BF16)
        w_v = w_in[l, :, 2 * WIDTH_A:3 * WIDTH_A]
        w_qvt = jnp.concatenate([w_q, w_v], axis=1).T.astype(BF16)
        w_pg = w_in[l, :, 3 * WIDTH_A:].astype(BF16)
        q_t, k_aug, v_t, kmean = _qkv_proj(h, w_k, w_qvt)
        km = kmean.reshape(b, nb, N_HEADS, HEAD_DIM).transpose(0, 2, 1, 3)
        km = jnp.pad(km, ((0, 0), (0, 0), (HEAD_DIM, LANES - HEAD_DIM - nb),
                          (0, LANES - HEAD_DIM))).astype(BF16)
        a = _moba_attention(q_t, k_aug, v_t, km, bias_tiles)
        h = _mix_ln1(h, a, w_pg, b_gate[l][None, :], w_pool[l].astype(BF16),
                     pool_scale[l][None, :], w_o_attn[l].astype(BF16),
                     w_o_pool[l].astype(BF16), w_out[l].astype(BF16),
                     ln1_g[l][None, :], ln1_b[l][None, :])
        h = _ffn_ln2(h.reshape(b * s, d), w_ff1[l].astype(BF16), b_ff1[l][None, :],
                     w_ff2[l].astype(BF16), b_ff2[l][None, :],
                     ln2_g[l][None, :], ln2_b[l][None, :]).reshape(b, s, d)
    return h
```

```python
import math

import jax
import jax.numpy as jnp
from jax import lax
from jax.experimental import pallas as pl
from jax.experimental.pallas import tpu as pltpu

D_MODEL = 1024
N_HEADS = 8
HEAD_DIM = 64
WIDTH_A = N_HEADS * HEAD_DIM
MOBA_BLOCK = 256
MOBA_TOPK = 3
N_POOL_GROUPS = 4
POOL_WINDOWS = (2, 4, 8, 16)
WIDTH_B = 512
POOL_GROUP_DIM = WIDTH_B // N_POOL_GROUPS
D_FF = 4 * D_MODEL
REL_BUCKETS = 32
REL_MAX_DIST = 128
DEEPNORM_ALPHA = 2.0 ** 0.25
LN_EPS = 1e-5
NEG_INF = -1e30

LANES = 128
MAX_KEY_BLOCKS = 32
HEADS_PER_STEP = 8
V_ROWS = HEAD_DIM + 16
LOG2E = math.log2(math.e)
HALO = 16
VMEM_LIMIT = 48 * 1024 * 1024

BF16 = jnp.bfloat16
F32 = jnp.float32
NT_DIMS = (((1,), (1,)), ((), ()))


def _layer_norm(y, g, b):
    mu = jnp.mean(y, axis=-1, keepdims=True)
    d = y - mu
    var = jnp.mean(d * d, axis=-1, keepdims=True)
    return d * lax.rsqrt(var + LN_EPS) * g + b


def _qkv_kernel(x_ref, wk_ref, wqvt_ref, q_ref, k_ref, v_ref, kmean_ref):
    xb = x_ref[0].astype(BF16)
    t = xb.shape[0]
    blocks_per_tile = t // MOBA_BLOCK
    acc_k = jnp.dot(xb, wk_ref[...], preferred_element_type=F32)
    acc_t = lax.dot_general(wqvt_ref[...], xb, NT_DIMS,
                            preferred_element_type=F32)
    lane = lax.broadcasted_iota(jnp.int32, (t, LANES), 1)
    row = lax.broadcasted_iota(jnp.int32, (t, LANES), 0)
    lo = lane < HEAD_DIM
    blk = pl.program_id(1) * blocks_per_tile + row // MOBA_BLOCK
    onehot = jnp.where(lane == HEAD_DIM + blk, 1.0, 0.0)
    for hp in range(N_HEADS // 2):
        kp = acc_k[:, hp * LANES:(hp + 1) * LANES] * LOG2E
        kr = pltpu.roll(kp, HEAD_DIM, 1)
        k_ref[0, 2 * hp] = jnp.where(lo, kp, onehot).astype(BF16)
        k_ref[0, 2 * hp + 1] = jnp.where(lo, kr, onehot).astype(BF16)
    pad0 = jnp.zeros((LANES - HEAD_DIM, t), BF16)
    pad1 = jnp.ones((V_ROWS - HEAD_DIM, t), BF16)
    for h in range(N_HEADS):
        q_ref[0, h, 0:HEAD_DIM, :] = acc_t[h * HEAD_DIM:(h + 1) * HEAD_DIM].astype(BF16)
        q_ref[0, h, HEAD_DIM:LANES, :] = pad0
        v_ref[0, h, 0:HEAD_DIM, :] = acc_t[
            WIDTH_A + h * HEAD_DIM:WIDTH_A + (h + 1) * HEAD_DIM].astype(BF16)
        v_ref[0, h, HEAD_DIM:V_ROWS, :] = pad1
    for c in range(blocks_per_tile):
        kmean_ref[0, c] = jnp.mean(acc_k[c * MOBA_BLOCK:(c + 1) * MOBA_BLOCK], axis=0, keepdims=True)


def _qkv_proj(x, w_k, w_qvt, t=2 * MOBA_BLOCK):
    b, s, d = x.shape
    nb = s // MOBA_BLOCK
    return pl.pallas_call(
        _qkv_kernel,
        grid=(b, s // t),
        in_specs=[
            pl.BlockSpec((1, t, d), lambda bi, si: (bi, si, 0)),
            pl.BlockSpec(w_k.shape, lambda bi, si: (0, 0)),
            pl.BlockSpec(w_qvt.shape, lambda bi, si: (0, 0)),
        ],
        out_specs=[
            pl.BlockSpec((1, N_HEADS, LANES, t), lambda bi, si: (bi, 0, 0, si)),
            pl.BlockSpec((1, N_HEADS, t, LANES), lambda bi, si: (bi, 0, si, 0)),
            pl.BlockSpec((1, N_HEADS, V_ROWS, t), lambda bi, si: (bi, 0, 0, si)),
            pl.BlockSpec((1, t // MOBA_BLOCK, 1, WIDTH_A), lambda bi, si: (bi, si, 0, 0)),
        ],
        out_shape=[
            jax.ShapeDtypeStruct((b, N_HEADS, LANES, s), BF16),
            jax.ShapeDtypeStruct((b, N_HEADS, s, LANES), BF16),
            jax.ShapeDtypeStruct((b, N_HEADS, V_ROWS, s), BF16),
            jax.ShapeDtypeStruct((b, nb, 1, WIDTH_A), F32),
        ],
        compiler_params=pltpu.CompilerParams(
            dimension_semantics=("arbitrary", "arbitrary"),
            vmem_limit_bytes=VMEM_LIMIT),
        name="qkv_proj",
    )(x, w_k, w_qvt)


def _attn_kernel(q_ref, k_ref, v_ref, km_ref, bias_ref, o_ref, qa_sc, s_a, s_b, acc_sc):
    i = pl.program_id(2)
    t = MOBA_BLOCK
    heads = range(HEADS_PER_STEP)
    blk_row = lax.broadcasted_iota(jnp.int32, (MAX_KEY_BLOCKS, t), 0)
    past = blk_row < i

    def key_block(h, j):
        return k_ref[0, h, pl.ds(pl.multiple_of(j * t, t), t), :]

    def head_scores_into(s_dst, h, j):
        s = jnp.dot(key_block(h, j), qa_sc[h], preferred_element_type=F32)
        s_dst[h] = s
        return jnp.max(s, axis=0, keepdims=True)

    def weighted_values(h, j, p):
        vv = v_ref[0, h, :, pl.ds(pl.multiple_of(j * t, t), t)]
        return jnp.dot(vv, p.astype(BF16), preferred_element_type=F32)

    def step(j, ms, s_cur, max_cur, s_nxt, bias_idx=None):
        new_ms, max_nxt = [], []
        for h in heads:
            if s_nxt is not None:
                max_nxt.append(head_scores_into(s_nxt, h, j + 1))
            s = s_cur[h]
            if bias_idx is not None:
                s = s + bias_ref[h, bias_idx]
                m_new = jnp.maximum(ms[h], jnp.max(s, axis=0, keepdims=True))
            else:
                m_new = jnp.maximum(ms[h], max_cur[h])
            alpha = jnp.exp2(ms[h] - m_new)
            acc_sc[h] = alpha * acc_sc[h] + weighted_values(h, j, jnp.exp2(s - m_new))
            new_ms.append(m_new)
        return new_ms, max_nxt

    ms = []
    for h in heads:
        s = jnp.dot(key_block(h, i), q_ref[0, h], preferred_element_type=F32) + bias_ref[h, 0]
        s_b[h] = s
        ms.append(jnp.max(s, axis=0, keepdims=True))

    for h in heads:
        qt = q_ref[0, h]
        gate = jnp.dot(km_ref[0, h], qt, preferred_element_type=F32)
        g = jnp.where(past, gate[HEAD_DIM:HEAD_DIM + MAX_KEY_BLOCKS], NEG_INF)
        sel = jnp.zeros(g.shape, jnp.bool_)
        for _ in range(MOBA_TOPK):
            mx = jnp.max(g, axis=0, keepdims=True)
            first = jnp.min(jnp.where(g == mx, blk_row, MAX_KEY_BLOCKS), axis=0, keepdims=True)
            pick = blk_row == first
            sel = sel | pick
            g = jnp.where(pick, -jnp.inf, g)
        sel = (sel & past) | (blk_row == i)
        qa_sc[h, 0:HEAD_DIM, :] = qt[0:HEAD_DIM]
        qa_sc[h, HEAD_DIM:HEAD_DIM + MAX_KEY_BLOCKS, :] = jnp.where(sel, 0.0, NEG_INF).astype(BF16)
        qa_sc[h, HEAD_DIM + MAX_KEY_BLOCKS:LANES, :] = jnp.zeros(
            (LANES - HEAD_DIM - MAX_KEY_BLOCKS, t), BF16)

    max_a = []
    for h in heads:
        max_a.append(head_scores_into(s_a, h, 0))
        acc_sc[h] = weighted_values(h, i, jnp.exp2(s_b[h] - ms[h]))

    n_far = jnp.maximum(i - 1, 0)

    def two_steps(jj, carry):
        ms, max_a = carry
        ms, max_b = step(2 * jj, list(ms), s_a, list(max_a), s_b)
        ms, max_a = step(2 * jj + 1, ms, s_b, max_b, s_a)
        return tuple(ms), tuple(max_a)

    ms, max_a = lax.fori_loop(0, n_far // 2, two_steps, (tuple(ms), tuple(max_a)))
    odd = n_far % 2 == 1

    @pl.when((i >= 1) & jnp.logical_not(odd))
    def _():
        step(i - 1, list(ms), s_a, None, None, bias_idx=1)

    @pl.when(odd)
    def _():
        ms1, _ = step(i - 2, list(ms), s_a, list(max_a), s_b)
        step(i - 1, ms1, s_b, None, None, bias_idx=1)

    for hp in range(HEADS_PER_STEP // 2):
        outs = []
        for h in (2 * hp, 2 * hp + 1):
            acc = acc_sc[h]
            outs.append(acc[0:HEAD_DIM] / acc[HEAD_DIM:HEAD_DIM + 1])
        o_ref[0, :, hp * LANES:(hp + 1) * LANES] = jnp.concatenate(outs, axis=0).T.astype(o_ref.dtype)


def _moba_attention(q_t, k_aug, v_t, km, bias_tiles):
    b, h, s, _ = k_aug.shape
    t = MOBA_BLOCK
    g = HEADS_PER_STEP
    group = lambda bi, hg, qi: (bi, hg, 0, 0)
    once = pl.Buffered(1)
    return pl.pallas_call(
        _attn_kernel,
        grid=(b, h // g, s // t),
        in_specs=[
            pl.BlockSpec((1, g, LANES, t), lambda bi, hg, qi: (bi, hg, 0, qi)),
            pl.BlockSpec((1, g, s, LANES), group, pipeline_mode=once),
            pl.BlockSpec((1, g, V_ROWS, s), group, pipeline_mode=once),
            pl.BlockSpec((1, g, LANES, LANES), group),
            pl.BlockSpec((g, 2, t, t), lambda bi, hg, qi: (hg, 0, 0, 0), pipeline_mode=once),
        ],
        out_specs=pl.BlockSpec((1, t, g * HEAD_DIM), lambda bi, hg, qi: (bi, qi, hg)),
        out_shape=jax.ShapeDtypeStruct((b, s, WIDTH_A), BF16),
        scratch_shapes=[
            pltpu.VMEM((g, LANES, t), BF16),
            pltpu.VMEM((g, t, t), F32),
            pltpu.VMEM((g, t, t), F32),
            pltpu.VMEM((g, V_ROWS, t), F32),
        ],
        compiler_params=pltpu.CompilerParams(
            dimension_semantics=("arbitrary", "arbitrary", "arbitrary"),
            vmem_limit_bytes=VMEM_LIMIT),
        name="moba_attn",
    )(q_t, k_aug, v_t, km, bias_tiles)


def _rel_bucket(dist):
    n = jnp.maximum(dist, 0)
    max_exact = REL_BUCKETS // 2
    nf = jnp.maximum(n, 1).astype(F32)
    large = max_exact + (jnp.log(nf / max_exact) / math.log(REL_MAX_DIST / max_exact)
                         * (REL_BUCKETS - max_exact)).astype(jnp.int32)
    large = jnp.minimum(large, REL_BUCKETS - 1)
    return jnp.where(n < max_exact, n, large)


def _bias_tiles(rel_table):
    t = MOBA_BLOCK
    kj = jnp.arange(t)[:, None]
    qi = jnp.arange(t)[None, :]
    dist = jnp.stack([qi - kj, t + qi - kj])
    onehot = _rel_bucket(dist)[None] == jnp.arange(REL_BUCKETS)[:, None, None, None]
    far = rel_table[_rel_bucket(jnp.array(2 * t, jnp.int32))]
    shifted = (rel_table - far[None, :]).T
    tiles = jnp.sum(jnp.where(onehot[None], shifted[:, :, None, None, None], 0.0), axis=1)
    causal = (dist >= 0)[None]
    return jnp.where(causal, tiles * LOG2E, NEG_INF).astype(F32)


def _mix_kernel(x_ref, a_ref, wpg_ref, bg_ref, wpool_ref, pscale_ref, woa_ref, wob_ref,
                wout_ref, g_ref, b_ref, h_ref, pbuf):
    si = pl.program_id(1)
    x = x_ref[0]
    t = x.shape[0]
    pg = jnp.dot(x.astype(BF16), wpg_ref[...], preferred_element_type=F32)
    p = pg[:, :WIDTH_B]

    @pl.when(si == 0)
    def _():
        pbuf[0:HALO, :] = jnp.zeros((HALO, WIDTH_B), F32)

    @pl.when(si > 0)
    def _():
        pbuf[0:HALO, :] = pbuf[t:t + HALO, :]

    pbuf[HALO:HALO + t, :] = p

    pos = si * t + lax.broadcasted_iota(jnp.int32, (t, 1), 0)
    mixed = []
    for gi, w in enumerate(POOL_WINDOWS):
        c0 = gi * POOL_GROUP_DIM
        pgp = p[:, c0:c0 + POOL_GROUP_DIM]
        ws = pgp
        for dlt in range(1, w):
            ws = ws + pbuf[HALO - dlt:HALO - dlt + t, c0:c0 + POOL_GROUP_DIM]
        cnt = jnp.minimum(pos + 1, w).astype(F32)
        pooled = ws / cnt - pgp
        mixed.append(jnp.dot(pooled.astype(BF16), wpool_ref[gi],
                             preferred_element_type=F32))
    mixed = jnp.concatenate(mixed, axis=1) * pscale_ref[...]
    branch_b = jnp.dot(mixed.astype(BF16), wob_ref[...], preferred_element_type=F32)
    branch_a = jnp.dot(a_ref[0], woa_ref[...], preferred_element_type=F32)
    gates = jax.nn.sigmoid(pg[:, WIDTH_B:] + bg_ref[...])
    mixin = gates[:, :D_MODEL] * branch_a + gates[:, D_MODEL:] * branch_b
    mix = jnp.dot(mixin.astype(BF16), wout_ref[...], preferred_element_type=F32)
    h_ref[0] = _layer_norm(DEEPNORM_ALPHA * x + mix, g_ref[...], b_ref[...])


def _mix_ln1(x, a, w_pg, b_gate, w_pool, pool_scale, w_oa, w_ob, w_out, ln_g, ln_b, t=512):
    b, s, d = x.shape
    const2 = lambda bi, si: (0, 0)
    return pl.pallas_call(
        _mix_kernel,
        grid=(b, s // t),
        in_specs=[
            pl.BlockSpec((1, t, d), lambda bi, si: (bi, si, 0)),
            pl.BlockSpec((1, t, WIDTH_A), lambda bi, si: (bi, si, 0)),
            pl.BlockSpec(w_pg.shape, const2),
            pl.BlockSpec(b_gate.shape, const2),
            pl.BlockSpec(w_pool.shape, lambda bi, si: (0, 0, 0)),
            pl.BlockSpec(pool_scale.shape, const2),
            pl.BlockSpec(w_oa.shape, const2),
            pl.BlockSpec(w_ob.shape, const2),
            pl.BlockSpec(w_out.shape, const2),
            pl.BlockSpec(ln_g.shape, const2),
            pl.BlockSpec(ln_b.shape, const2),
        ],
        out_specs=pl.BlockSpec((1, t, d), lambda bi, si: (bi, si, 0)),
        out_shape=jax.ShapeDtypeStruct((b, s, d), F32),
        scratch_shapes=[pltpu.VMEM((HALO + t, WIDTH_B), F32)],
        compiler_params=pltpu.CompilerParams(
            dimension_semantics=("arbitrary", "arbitrary"),
            vmem_limit_bytes=VMEM_LIMIT),
        name="mix_ln1",
    )(x, a, w_pg, b_gate, w_pool, pool_scale, w_oa, w_ob, w_out, ln_g, ln_b)


FF_CHUNK = 1024


def _ffn_kernel(h_ref, w1_ref, b1_ref, w2_ref, b2_ref, g_ref, b_ref, o_ref):
    h = h_ref[...]
    hb = h.astype(BF16)
    ff = jnp.zeros(h.shape, F32)
    for c in range(D_FF // FF_CHUNK):
        cs = slice(c * FF_CHUNK, (c + 1) * FF_CHUNK)
        u = jnp.dot(hb, w1_ref[:, cs], preferred_element_type=F32) + b1_ref[:, cs]
        u = jnp.square(jnp.maximum(u, 0.0))
        ff = ff + jnp.dot(u.astype(BF16), w2_ref[cs, :], preferred_element_type=F32)
    ff = ff + b2_ref[...]
    o_ref[...] = _layer_norm(DEEPNORM_ALPHA * h + ff, g_ref[...], b_ref[...])


def _ffn_ln2(h, w1, b1, w2, b2, ln_g, ln_b, t=512):
    n, d = h.shape
    const2 = lambda i: (0, 0)
    return pl.pallas_call(
        _ffn_kernel,
        grid=(n // t,),
        in_specs=[
            pl.BlockSpec((t, d), lambda i: (i, 0)),
            pl.BlockSpec(w1.shape, const2),
            pl.BlockSpec(b1.shape, const2),
            pl.BlockSpec(w2.shape, const2),
            pl.BlockSpec(b2.shape, const2),
            pl.BlockSpec(ln_g.shape, const2),
            pl.BlockSpec(ln_b.shape, const2),
        ],
        out_specs=pl.BlockSpec((t, d), lambda i: (i, 0)),
        out_shape=jax.ShapeDtypeStruct((n, d), F32),
        compiler_params=pltpu.CompilerParams(
            dimension_semantics=("arbitrary",),
            vmem_limit_bytes=VMEM_LIMIT),
        name="ffn_ln2",
    )(h, w1, b1, w2, b2, ln_g, ln_b)


def kernel(x, w_in, b_gate, rel_table, w_pool, pool_scale, w_o_attn, w_o_pool, w_out,
           ln1_g, ln1_b, w_ff1, b_ff1, w_ff2, b_ff2, ln2_g, ln2_b):
    b, s, d = x.shape
    nb = s // MOBA_BLOCK
    assert s % MOBA_BLOCK == 0 and nb <= MAX_KEY_BLOCKS
    bias_tiles = _bias_tiles(rel_table)
    h = x
    for l in range(w_in.shape[0]):
        w_q = w_in[l, :, :WIDTH_A] * (HEAD_DIM ** -0.5)
        w_k = w_in[l, :, WIDTH_A:2 * WIDTH_A].astype(BF16)
        w_v = w_in[l, :, 2 * WIDTH_A:3 * WIDTH_A]
        w_qvt = jnp.concatenate([w_q, w_v], axis=1).T.astype(BF16)
        w_pg = w_in[l, :, 3 * WIDTH_A:].astype(BF16)
        q_t, k_aug, v_t, kmean = _qkv_proj(h, w_k, w_qvt)
        km = kmean.reshape(b, nb, N_HEADS, HEAD_DIM).transpose(0, 2, 1, 3)
        km = jnp.pad(km, ((0, 0), (0, 0), (HEAD_DIM, LANES - HEAD_DIM - nb),
                          (0, LANES - HEAD_DIM))).astype(BF16)
        a = _moba_attention(q_t, k_aug, v_t, km, bias_tiles)
        h = _mix_ln1(h, a, w_pg, b_gate[l][None, :], w_pool[l].astype(BF16),
                     pool_scale[l][None, :], w_o_attn[l].astype(BF16),
                     w_o_pool[l].astype(BF16), w_out[l].astype(BF16),
                     ln1_g[l][None, :], ln1_b[l][None, :])
        h = _ffn_ln2(h.reshape(b * s, d), w_ff1[l].astype(BF16), b_ff1[l][None, :],
                     w_ff2[l].astype(BF16), b_ff2[l][None, :],
                     ln2_g[l][None, :], ln2_b[l][None, :]).reshape(b, s, d)
    return h
```

```python
import math

import jax
import jax.numpy as jnp
from jax import lax
from jax.experimental import pallas as pl
from jax.experimental.pallas import tpu as pltpu

D_MODEL = 1024
N_HEADS = 8
HEAD_DIM = 64
WIDTH_A = N_HEADS * HEAD_DIM
MOBA_BLOCK = 256
MOBA_TOPK = 3
N_POOL_GROUPS = 4
POOL_WINDOWS = (2, 4, 8, 16)
WIDTH_B = 512
POOL_GROUP_DIM = WIDTH_B // N_POOL_GROUPS
D_FF = 4 * D_MODEL
REL_BUCKETS = 32
REL_MAX_DIST = 128
DEEPNORM_ALPHA = 2.0 ** 0.25
LN_EPS = 1e-5
NEG_INF = -1e30

LANES = 128
MAX_KEY_BLOCKS = 32
HEADS_PER_STEP = 8
V_ROWS = HEAD_DIM + 16
LOG2E = math.log2(math.e)
HALO = 16
MIX_CHUNK = 256
VMEM_LIMIT = 48 * 1024 * 1024

BF16 = jnp.bfloat16
F32 = jnp.float32
NT_DIMS = (((1,), (1,)), ((), ()))


def _layer_norm(y, g, b):
    mu = jnp.mean(y, axis=-1, keepdims=True)
    d = y - mu
    var = jnp.mean(d * d, axis=-1, keepdims=True)
    return d * lax.rsqrt(var + LN_EPS) * g + b


def _qkv_kernel(x_ref, wk_ref, wqvt_ref, q_ref, k_ref, v_ref, kmean_ref):
    xb = x_ref[0].astype(BF16)
    t = xb.shape[0]
    blocks_per_tile = t // MOBA_BLOCK
    acc_k = jnp.dot(xb, wk_ref[...], preferred_element_type=F32)
    acc_t = lax.dot_general(wqvt_ref[...], xb, NT_DIMS,
                            preferred_element_type=F32)
    lane = lax.broadcasted_iota(jnp.int32, (t, LANES), 1)
    row = lax.broadcasted_iota(jnp.int32, (t, LANES), 0)
    lo = lane < HEAD_DIM
    blk = pl.program_id(1) * blocks_per_tile + row // MOBA_BLOCK
    onehot = jnp.where(lane == HEAD_DIM + blk, 1.0, 0.0)
    for hp in range(N_HEADS // 2):
        kp = acc_k[:, hp * LANES:(hp + 1) * LANES] * LOG2E
        kr = pltpu.roll(kp, HEAD_DIM, 1)
        k_ref[0, 2 * hp] = jnp.where(lo, kp, onehot).astype(BF16)
        k_ref[0, 2 * hp + 1] = jnp.where(lo, kr, onehot).astype(BF16)
    pad0 = jnp.zeros((LANES - HEAD_DIM, t), BF16)
    pad1 = jnp.ones((V_ROWS - HEAD_DIM, t), BF16)
    for h in range(N_HEADS):
        q_ref[0, h, 0:HEAD_DIM, :] = acc_t[h * HEAD_DIM:(h + 1) * HEAD_DIM].astype(BF16)
        q_ref[0, h, HEAD_DIM:LANES, :] = pad0
        v_ref[0, h, 0:HEAD_DIM, :] = acc_t[
            WIDTH_A + h * HEAD_DIM:WIDTH_A + (h + 1) * HEAD_DIM].astype(BF16)
        v_ref[0, h, HEAD_DIM:V_ROWS, :] = pad1
    for c in range(blocks_per_tile):
        kmean_ref[0, c] = jnp.mean(acc_k[c * MOBA_BLOCK:(c + 1) * MOBA_BLOCK], axis=0, keepdims=True)


def _qkv_proj(x, w_k, w_qvt, t=2 * MOBA_BLOCK):
    b, s, d = x.shape
    nb = s // MOBA_BLOCK
    return pl.pallas_call(
        _qkv_kernel,
        grid=(b, s // t),
        in_specs=[
            pl.BlockSpec((1, t, d), lambda bi, si: (bi, si, 0)),
            pl.BlockSpec(w_k.shape, lambda bi, si: (0, 0)),
            pl.BlockSpec(w_qvt.shape, lambda bi, si: (0, 0)),
        ],
        out_specs=[
            pl.BlockSpec((1, N_HEADS, LANES, t), lambda bi, si: (bi, 0, 0, si)),
            pl.BlockSpec((1, N_HEADS, t, LANES), lambda bi, si: (bi, 0, si, 0)),
            pl.BlockSpec((1, N_HEADS, V_ROWS, t), lambda bi, si: (bi, 0, 0, si)),
            pl.BlockSpec((1, t // MOBA_BLOCK, 1, WIDTH_A), lambda bi, si: (bi, si, 0, 0)),
        ],
        out_shape=[
            jax.ShapeDtypeStruct((b, N_HEADS, LANES, s), BF16),
            jax.ShapeDtypeStruct((b, N_HEADS, s, LANES), BF16),
            jax.ShapeDtypeStruct((b, N_HEADS, V_ROWS, s), BF16),
            jax.ShapeDtypeStruct((b, nb, 1, WIDTH_A), F32),
        ],
        compiler_params=pltpu.CompilerParams(
            dimension_semantics=("arbitrary", "arbitrary"),
            vmem_limit_bytes=VMEM_LIMIT),
        name="qkv_proj",
    )(x, w_k, w_qvt)


def _attn_kernel(q_ref, k_ref, v_ref, km_ref, bias_ref, o_ref, qa_sc, s_a, s_b, acc_sc):
    i = pl.program_id(2)
    t = MOBA_BLOCK
    heads = range(HEADS_PER_STEP)
    blk_row = lax.broadcasted_iota(jnp.int32, (MAX_KEY_BLOCKS, t), 0)
    past = blk_row < i

    def key_block(h, j):
        return k_ref[0, h, pl.ds(pl.multiple_of(j * t, t), t), :]

    def head_scores_into(s_dst, h, j):
        s = jnp.dot(key_block(h, j), qa_sc[h], preferred_element_type=F32)
        s_dst[h] = s
        return jnp.max(s, axis=0, keepdims=True)

    def weighted_values(h, j, p):
        vv = v_ref[0, h, :, pl.ds(pl.multiple_of(j * t, t), t)]
        return jnp.dot(vv, p.astype(BF16), preferred_element_type=F32)

    def step(j, ms, s_cur, max_cur, s_nxt, bias_idx=None):
        new_ms, max_nxt = [], []
        for h in heads:
            if s_nxt is not None:
                max_nxt.append(head_scores_into(s_nxt, h, j + 1))
            s = s_cur[h]
            if bias_idx is not None:
                s = s + bias_ref[h, bias_idx]
                m_new = jnp.maximum(ms[h], jnp.max(s, axis=0, keepdims=True))
            else:
                m_new = jnp.maximum(ms[h], max_cur[h])
            alpha = jnp.exp2(ms[h] - m_new)
            acc_sc[h] = alpha * acc_sc[h] + weighted_values(h, j, jnp.exp2(s - m_new))
            new_ms.append(m_new)
        return new_ms, max_nxt

    ms = []
    for h in heads:
        s = jnp.dot(key_block(h, i), q_ref[0, h], preferred_element_type=F32) + bias_ref[h, 0]
        s_b[h] = s
        ms.append(jnp.max(s, axis=0, keepdims=True))

    for h in heads:
        qt = q_ref[0, h]
        gate = jnp.dot(km_ref[0, h], qt, preferred_element_type=F32)
        g = jnp.where(past, gate[HEAD_DIM:HEAD_DIM + MAX_KEY_BLOCKS], NEG_INF)
        sel = jnp.zeros(g.shape, jnp.bool_)
        for _ in range(MOBA_TOPK):
            mx = jnp.max(g, axis=0, keepdims=True)
            first = jnp.min(jnp.where(g == mx, blk_row, MAX_KEY_BLOCKS), axis=0, keepdims=True)
            pick = blk_row == first
            sel = sel | pick
            g = jnp.where(pick, -jnp.inf, g)
        sel = (sel & past) | (blk_row == i)
        qa_sc[h, 0:HEAD_DIM, :] = qt[0:HEAD_DIM]
        qa_sc[h, HEAD_DIM:HEAD_DIM + MAX_KEY_BLOCKS, :] = jnp.where(sel, 0.0, NEG_INF).astype(BF16)
        qa_sc[h, HEAD_DIM + MAX_KEY_BLOCKS:LANES, :] = jnp.zeros(
            (LANES - HEAD_DIM - MAX_KEY_BLOCKS, t), BF16)

    max_a = []
    for h in heads:
        max_a.append(head_scores_into(s_a, h, 0))
        acc_sc[h] = weighted_values(h, i, jnp.exp2(s_b[h] - ms[h]))

    n_far = jnp.maximum(i - 1, 0)

    def two_steps(jj, carry):
        ms, max_a = carry
        ms, max_b = step(2 * jj, list(ms), s_a, list(max_a), s_b)
        ms, max_a = step(2 * jj + 1, ms, s_b, max_b, s_a)
        return tuple(ms), tuple(max_a)

    ms, max_a = lax.fori_loop(0, n_far // 2, two_steps, (tuple(ms), tuple(max_a)))
    odd = n_far % 2 == 1

    @pl.when((i >= 1) & jnp.logical_not(odd))
    def _():
        step(i - 1, list(ms), s_a, None, None, bias_idx=1)

    @pl.when(odd)
    def _():
        ms1, _ = step(i - 2, list(ms), s_a, list(max_a), s_b)
        step(i - 1, ms1, s_b, None, None, bias_idx=1)

    for hp in range(HEADS_PER_STEP // 2):
        outs = []
        for h in (2 * hp, 2 * hp + 1):
            acc = acc_sc[h]
            outs.append(acc[0:HEAD_DIM] / acc[HEAD_DIM:HEAD_DIM + 1])
        o_ref[0, :, hp * LANES:(hp + 1) * LANES] = jnp.concatenate(outs, axis=0).T.astype(o_ref.dtype)


def _moba_attention(q_t, k_aug, v_t, km, bias_tiles):
    b, h, s, _ = k_aug.shape
    t = MOBA_BLOCK
    g = HEADS_PER_STEP
    group = lambda bi, hg, qi: (bi, hg, 0, 0)
    once = pl.Buffered(1)
    return pl.pallas_call(
        _attn_kernel,
        grid=(b, h // g, s // t),
        in_specs=[
            pl.BlockSpec((1, g, LANES, t), lambda bi, hg, qi: (bi, hg, 0, qi)),
            pl.BlockSpec((1, g, s, LANES), group, pipeline_mode=once),
            pl.BlockSpec((1, g, V_ROWS, s), group, pipeline_mode=once),
            pl.BlockSpec((1, g, LANES, LANES), group),
            pl.BlockSpec((g, 2, t, t), lambda bi, hg, qi: (hg, 0, 0, 0), pipeline_mode=once),
        ],
        out_specs=pl.BlockSpec((1, t, g * HEAD_DIM), lambda bi, hg, qi: (bi, qi, hg)),
        out_shape=jax.ShapeDtypeStruct((b, s, WIDTH_A), BF16),
        scratch_shapes=[
            pltpu.VMEM((g, LANES, t), BF16),
            pltpu.VMEM((g, t, t), F32),
            pltpu.VMEM((g, t, t), F32),
            pltpu.VMEM((g, V_ROWS, t), F32),
        ],
        compiler_params=pltpu.CompilerParams(
            dimension_semantics=("arbitrary", "arbitrary", "arbitrary"),
            vmem_limit_bytes=VMEM_LIMIT),
        name="moba_attn",
    )(q_t, k_aug, v_t, km, bias_tiles)


def _rel_bucket(dist):
    n = jnp.maximum(dist, 0)
    max_exact = REL_BUCKETS // 2
    nf = jnp.maximum(n, 1).astype(F32)
    large = max_exact + (jnp.log(nf / max_exact) / math.log(REL_MAX_DIST / max_exact)
                         * (REL_BUCKETS - max_exact)).astype(jnp.int32)
    large = jnp.minimum(large, REL_BUCKETS - 1)
    return jnp.where(n < max_exact, n, large)


def _bias_tiles(rel_table):
    t = MOBA_BLOCK
    kj = jnp.arange(t)[:, None]
    qi = jnp.arange(t)[None, :]
    dist = jnp.stack([qi - kj, t + qi - kj])
    onehot = _rel_bucket(dist)[None] == jnp.arange(REL_BUCKETS)[:, None, None, None]
    far = rel_table[_rel_bucket(jnp.array(2 * t, jnp.int32))]
    shifted = (rel_table - far[None, :]).T
    tiles = jnp.sum(jnp.where(onehot[None], shifted[:, :, None, None, None], 0.0), axis=1)
    causal = (dist >= 0)[None]
    return jnp.where(causal, tiles * LOG2E, NEG_INF).astype(F32)


def _mix_kernel(x_ref, a_ref, wpg_ref, bg_ref, wpool_ref, pscale_ref, woa_ref, wob_ref,
                wout_ref, g_ref, b_ref, h_ref, pbuf):
    si = pl.program_id(1)
    t = x_ref.shape[1]

    @pl.when(si == 0)
    def _():
        pbuf[0:HALO, :] = jnp.zeros((HALO, WIDTH_B), F32)

    @pl.when(si > 0)
    def _():
        pbuf[0:HALO, :] = pbuf[t:t + HALO, :]

    pbuf[HALO:HALO + t, :] = jnp.dot(x_ref[0].astype(BF16), wpg_ref[:, :WIDTH_B],
                                     preferred_element_type=F32)
    rows = [slice(c * MIX_CHUNK, (c + 1) * MIX_CHUNK) for c in range(t // MIX_CHUNK)]
    gate_logits = [jnp.dot(x_ref[0, r, :].astype(BF16), wpg_ref[:, WIDTH_B:],
                           preferred_element_type=F32) for r in rows]
    for r, gl in zip(rows, gate_logits):
        n = r.stop - r.start
        pos = si * t + r.start + lax.broadcasted_iota(jnp.int32, (n, 1), 0)
        mixed = []
        for gi, w in enumerate(POOL_WINDOWS):
            cols = slice(gi * POOL_GROUP_DIM, (gi + 1) * POOL_GROUP_DIM)
            own = pbuf[HALO + r.start:HALO + r.stop, cols]
            ws = own
            for dlt in range(1, w):
                ws = ws + pbuf[HALO + r.start - dlt:HALO + r.stop - dlt, cols]
            cnt = jnp.minimum(pos + 1, w).astype(F32)
            pooled = ws / cnt - own
            mixed.append(jnp.dot(pooled.astype(BF16), wpool_ref[gi],
                                 preferred_element_type=F32))
        mixed = jnp.concatenate(mixed, axis=1) * pscale_ref[...]
        branch_b = jnp.dot(mixed.astype(BF16), wob_ref[...], preferred_element_type=F32)
        branch_a = jnp.dot(a_ref[0, r, :], woa_ref[...], preferred_element_type=F32)
        gates = jax.nn.sigmoid(gl + bg_ref[...])
        mixin = gates[:, :D_MODEL] * branch_a + gates[:, D_MODEL:] * branch_b
        mix = jnp.dot(mixin.astype(BF16), wout_ref[...], preferred_element_type=F32)
        h_ref[0, r, :] = _layer_norm(DEEPNORM_ALPHA * x_ref[0, r, :] + mix,
                                     g_ref[...], b_ref[...])


def _mix_ln1(x, a, w_pg, b_gate, w_pool, pool_scale, w_oa, w_ob, w_out, ln_g, ln_b, t=1024):
    b, s, d = x.shape
    const2 = lambda bi, si: (0, 0)
    once = pl.Buffered(1)
    return pl.pallas_call(
        _mix_kernel,
        grid=(b, s // t),
        in_specs=[
            pl.BlockSpec((1, t, d), lambda bi, si: (bi, si, 0)),
            pl.BlockSpec((1, t, WIDTH_A), lambda bi, si: (bi, si, 0)),
            pl.BlockSpec(w_pg.shape, const2, pipeline_mode=once),
            pl.BlockSpec(b_gate.shape, const2),
            pl.BlockSpec(w_pool.shape, lambda bi, si: (0, 0, 0)),
            pl.BlockSpec(pool_scale.shape, const2),
            pl.BlockSpec(w_oa.shape, const2, pipeline_mode=once),
            pl.BlockSpec(w_ob.shape, const2, pipeline_mode=once),
            pl.BlockSpec(w_out.shape, const2, pipeline_mode=once),
            pl.BlockSpec(ln_g.shape, const2),
            pl.BlockSpec(ln_b.shape, const2),
        ],
        out_specs=pl.BlockSpec((1, t, d), lambda bi, si: (bi, si, 0)),
        out_shape=jax.ShapeDtypeStruct((b, s, d), F32),
        scratch_shapes=[pltpu.VMEM((HALO + t, WIDTH_B), F32)],
        compiler_params=pltpu.CompilerParams(
            dimension_semantics=("arbitrary", "arbitrary"),
            vmem_limit_bytes=VMEM_LIMIT),
        name="mix_ln1",
    )(x, a, w_pg, b_gate, w_pool, pool_scale, w_oa, w_ob, w_out, ln_g, ln_b)


FF_CHUNK = 1024


def _ffn_kernel(h_ref, w1_ref, b1_ref, w2_ref, b2_ref, g_ref, b_ref, o_ref):
    h = h_ref[...]
    hb = h.astype(BF16)
    ff = jnp.zeros(h.shape, F32)
    for c in range(D_FF // FF_CHUNK):
        cs = slice(c * FF_CHUNK, (c + 1) * FF_CHUNK)
        u = jnp.dot(hb, w1_ref[:, cs], preferred_element_type=F32) + b1_ref[:, cs]
        u = jnp.square(jnp.maximum(u, 0.0))
        ff = ff + jnp.dot(u.astype(BF16), w2_ref[cs, :], preferred_element_type=F32)
    ff = ff + b2_ref[...]
    o_ref[...] = _layer_norm(DEEPNORM_ALPHA * h + ff, g_ref[...], b_ref[...])


def _ffn_ln2(h, w1, b1, w2, b2, ln_g, ln_b, t=512):
    n, d = h.shape
    const2 = lambda i: (0, 0)
    return pl.pallas_call(
        _ffn_kernel,
        grid=(n // t,),
        in_specs=[
            pl.BlockSpec((t, d), lambda i: (i, 0)),
            pl.BlockSpec(w1.shape, const2),
            pl.BlockSpec(b1.shape, const2),
            pl.BlockSpec(w2.shape, const2),
            pl.BlockSpec(b2.shape, const2),
            pl.BlockSpec(ln_g.shape, const2),
            pl.BlockSpec(ln_b.shape, const2),
        ],
        out_specs=pl.BlockSpec((t, d), lambda i: (i, 0)),
        out_shape=jax.ShapeDtypeStruct((n, d), F32),
        compiler_params=pltpu.CompilerParams(
            dimension_semantics=("arbitrary",),
            vmem_limit_bytes=VMEM_LIMIT),
        name="ffn_ln2",
    )(h, w1, b1, w2, b2, ln_g, ln_b)


def kernel(x, w_in, b_gate, rel_table, w_pool, pool_scale, w_o_attn, w_o_pool, w_out,
           ln1_g, ln1_b, w_ff1, b_ff1, w_ff2, b_ff2, ln2_g, ln2_b):
    b, s, d = x.shape
    nb = s // MOBA_BLOCK
    assert s % MOBA_BLOCK == 0 and nb <= MAX_KEY_BLOCKS
    bias_tiles = _bias_tiles(rel_table)
    h = x
    for l in range(w_in.shape[0]):
        w_q = w_in[l, :, :WIDTH_A] * (HEAD_DIM ** -0.5)
        w_k = w_in[l, :, WIDTH_A:2 * WIDTH_A].astype(BF16)
        w_v = w_in[l, :, 2 * WIDTH_A:3 * WIDTH_A]
        w_qvt = jnp.concatenate([w_q, w_v], axis=1).T.astype(BF16)
        w_pg = w_in[l, :, 3 * WIDTH_A:].astype(BF16)
        q_t, k_aug, v_t, kmean = _qkv_proj(h, w_k, w_qvt)
        km = kmean.reshape(b, nb, N_HEADS, HEAD_DIM).transpose(0, 2, 1, 3)
        km = jnp.pad(km, ((0, 0), (0, 0), (HEAD_DIM, LANES - HEAD_DIM - nb),
                          (0, LANES - HEAD_DIM))).astype(BF16)
        a = _moba_attention(q_t, k_aug, v_t, km, bias_tiles)
        h = _mix_ln1(h, a, w_pg, b_gate[l][None, :], w_pool[l].astype(BF16),
                     pool_scale[l][None, :], w_o_attn[l].astype(BF16),
                     w_o_pool[l].astype(BF16), w_out[l].astype(BF16),
                     ln1_g[l][None, :], ln1_b[l][None, :])
        h = _ffn_ln2(h.reshape(b * s, d), w_ff1[l].astype(BF16), b_ff1[l][None, :],
                     w_ff2[l].astype(BF16), b_ff2[l][None, :],
                     ln2_g[l][None, :], ln2_b[l][None, :]).reshape(b, s, d)
    return h
```

```python
import math

import jax
import jax.numpy as jnp
from jax import lax
from jax.experimental import pallas as pl
from jax.experimental.pallas import tpu as pltpu

D_MODEL = 1024
N_HEADS = 8
HEAD_DIM = 64
WIDTH_A = N_HEADS * HEAD_DIM
MOBA_BLOCK = 256
MOBA_TOPK = 3
N_POOL_GROUPS = 4
POOL_WINDOWS = (2, 4, 8, 16)
WIDTH_B = 512
POOL_GROUP_DIM = WIDTH_B // N_POOL_GROUPS
D_FF = 4 * D_MODEL
REL_BUCKETS = 32
REL_MAX_DIST = 128
DEEPNORM_ALPHA = 2.0 ** 0.25
LN_EPS = 1e-5
NEG_INF = -1e30

LANES = 128
MAX_KEY_BLOCKS = 32
HEADS_PER_STEP = 8
V_ROWS = HEAD_DIM + 16
LOG2E = math.log2(math.e)
HALO = 16
MIX_CHUNK = 256
VMEM_LIMIT = 48 * 1024 * 1024

BF16 = jnp.bfloat16
F32 = jnp.float32
NT_DIMS = (((1,), (1,)), ((), ()))


def _layer_norm(y, g, b):
    mu = jnp.mean(y, axis=-1, keepdims=True)
    d = y - mu
    var = jnp.mean(d * d, axis=-1, keepdims=True)
    return d * lax.rsqrt(var + LN_EPS) * g + b


def _qkv_kernel(x_ref, wk_ref, wqvt_ref, q_ref, k_ref, v_ref, kmean_ref):
    xb = x_ref[0].astype(BF16)
    t = xb.shape[0]
    blocks_per_tile = t // MOBA_BLOCK
    acc_k = jnp.dot(xb, wk_ref[...], preferred_element_type=F32)
    acc_t = lax.dot_general(wqvt_ref[...], xb, NT_DIMS,
                            preferred_element_type=F32)
    lane = lax.broadcasted_iota(jnp.int32, (t, LANES), 1)
    row = lax.broadcasted_iota(jnp.int32, (t, LANES), 0)
    lo = lane < HEAD_DIM
    blk = pl.program_id(1) * blocks_per_tile + row // MOBA_BLOCK
    onehot = jnp.where(lane == HEAD_DIM + blk, 1.0, 0.0)
    for hp in range(N_HEADS // 2):
        kp = acc_k[:, hp * LANES:(hp + 1) * LANES] * LOG2E
        kr = pltpu.roll(kp, HEAD_DIM, 1)
        k_ref[0, 2 * hp] = jnp.where(lo, kp, onehot).astype(BF16)
        k_ref[0, 2 * hp + 1] = jnp.where(lo, kr, onehot).astype(BF16)
    pad0 = jnp.zeros((LANES - HEAD_DIM, t), BF16)
    pad1 = jnp.ones((V_ROWS - HEAD_DIM, t), BF16)
    for h in range(N_HEADS):
        q_ref[0, h, 0:HEAD_DIM, :] = acc_t[h * HEAD_DIM:(h + 1) * HEAD_DIM].astype(BF16)
        q_ref[0, h, HEAD_DIM:LANES, :] = pad0
        v_ref[0, h, 0:HEAD_DIM, :] = acc_t[
            WIDTH_A + h * HEAD_DIM:WIDTH_A + (h + 1) * HEAD_DIM].astype(BF16)
        v_ref[0, h, HEAD_DIM:V_ROWS, :] = pad1
    for c in range(blocks_per_tile):
        kmean_ref[0, c] = jnp.mean(acc_k[c * MOBA_BLOCK:(c + 1) * MOBA_BLOCK], axis=0, keepdims=True)


def _qkv_proj(x, w_k, w_qvt, t=2 * MOBA_BLOCK):
    b, s, d = x.shape
    nb = s // MOBA_BLOCK
    return pl.pallas_call(
        _qkv_kernel,
        grid=(b, s // t),
        in_specs=[
            pl.BlockSpec((1, t, d), lambda bi, si: (bi, si, 0)),
            pl.BlockSpec(w_k.shape, lambda bi, si: (0, 0)),
            pl.BlockSpec(w_qvt.shape, lambda bi, si: (0, 0)),
        ],
        out_specs=[
            pl.BlockSpec((1, N_HEADS, LANES, t), lambda bi, si: (bi, 0, 0, si)),
            pl.BlockSpec((1, N_HEADS, t, LANES), lambda bi, si: (bi, 0, si, 0)),
            pl.BlockSpec((1, N_HEADS, V_ROWS, t), lambda bi, si: (bi, 0, 0, si)),
            pl.BlockSpec((1, t // MOBA_BLOCK, 1, WIDTH_A), lambda bi, si: (bi, si, 0, 0)),
        ],
        out_shape=[
            jax.ShapeDtypeStruct((b, N_HEADS, LANES, s), BF16),
            jax.ShapeDtypeStruct((b, N_HEADS, s, LANES), BF16),
            jax.ShapeDtypeStruct((b, N_HEADS, V_ROWS, s), BF16),
            jax.ShapeDtypeStruct((b, nb, 1, WIDTH_A), F32),
        ],
        compiler_params=pltpu.CompilerParams(
            dimension_semantics=("arbitrary", "arbitrary"),
            vmem_limit_bytes=VMEM_LIMIT),
        name="qkv_proj",
    )(x, w_k, w_qvt)


def _attn_kernel(q_ref, k_ref, v_ref, km_ref, bias_ref, o_ref, qa_sc, s_a, s_b, acc_sc):
    i = pl.program_id(2)
    t = MOBA_BLOCK
    heads = range(HEADS_PER_STEP)
    blk_row = lax.broadcasted_iota(jnp.int32, (MAX_KEY_BLOCKS, t), 0)
    past = blk_row < i

    def key_block(h, j):
        return k_ref[0, h, pl.ds(pl.multiple_of(j * t, t), t), :]

    def head_scores_into(s_dst, h, j):
        s = jnp.dot(key_block(h, j), qa_sc[h], preferred_element_type=F32)
        s_dst[h] = s
        return jnp.max(s, axis=0, keepdims=True)

    def weighted_values(h, j, p):
        vv = v_ref[0, h, :, pl.ds(pl.multiple_of(j * t, t), t)]
        return jnp.dot(vv, p.astype(BF16), preferred_element_type=F32)

    def step(j, ms, s_cur, max_cur, s_nxt, bias_idx=None):
        new_ms, max_nxt = [], []
        for h in heads:
            if s_nxt is not None:
                max_nxt.append(head_scores_into(s_nxt, h, j + 1))
            s = s_cur[h]
            if bias_idx is not None:
                s = s + bias_ref[h, bias_idx]
                m_new = jnp.maximum(ms[h], jnp.max(s, axis=0, keepdims=True))
            else:
                m_new = jnp.maximum(ms[h], max_cur[h])
            alpha = jnp.exp2(ms[h] - m_new)
            acc_sc[h] = alpha * acc_sc[h] + weighted_values(h, j, jnp.exp2(s - m_new))
            new_ms.append(m_new)
        return new_ms, max_nxt

    ms = []
    for h in heads:
        s = jnp.dot(key_block(h, i), q_ref[0, h], preferred_element_type=F32) + bias_ref[h, 0]
        s_b[h] = s
        ms.append(jnp.max(s, axis=0, keepdims=True))

    for h in heads:
        qt = q_ref[0, h]
        gate = jnp.dot(km_ref[0, h], qt, preferred_element_type=F32)
        g = jnp.where(past, gate[HEAD_DIM:HEAD_DIM + MAX_KEY_BLOCKS], NEG_INF)
        sel = jnp.zeros(g.shape, jnp.bool_)
        for _ in range(MOBA_TOPK):
            mx = jnp.max(g, axis=0, keepdims=True)
            first = jnp.min(jnp.where(g == mx, blk_row, MAX_KEY_BLOCKS), axis=0, keepdims=True)
            pick = blk_row == first
            sel = sel | pick
            g = jnp.where(pick, -jnp.inf, g)
        sel = (sel & past) | (blk_row == i)
        qa_sc[h, 0:HEAD_DIM, :] = qt[0:HEAD_DIM]
        qa_sc[h, HEAD_DIM:HEAD_DIM + MAX_KEY_BLOCKS, :] = jnp.where(sel, 0.0, NEG_INF).astype(BF16)
        qa_sc[h, HEAD_DIM + MAX_KEY_BLOCKS:LANES, :] = jnp.zeros(
            (LANES - HEAD_DIM - MAX_KEY_BLOCKS, t), BF16)

    max_a = []
    for h in heads:
        max_a.append(head_scores_into(s_a, h, 0))
        acc_sc[h] = weighted_values(h, i, jnp.exp2(s_b[h] - ms[h]))

    n_far = jnp.maximum(i - 1, 0)

    def two_steps(jj, carry):
        ms, max_a = carry
        ms, max_b = step(2 * jj, list(ms), s_a, list(max_a), s_b)
        ms, max_a = step(2 * jj + 1, ms, s_b, max_b, s_a)
        return tuple(ms), tuple(max_a)

    ms, max_a = lax.fori_loop(0, n_far // 2, two_steps, (tuple(ms), tuple(max_a)))
    odd = n_far % 2 == 1

    @pl.when((i >= 1) & jnp.logical_not(odd))
    def _():
        step(i - 1, list(ms), s_a, None, None, bias_idx=1)

    @pl.when(odd)
    def _():
        ms1, _ = step(i - 2, list(ms), s_a, list(max_a), s_b)
        step(i - 1, ms1, s_b, None, None, bias_idx=1)

    for hp in range(HEADS_PER_STEP // 2):
        outs = []
        for h in (2 * hp, 2 * hp + 1):
            acc = acc_sc[h]
            outs.append(acc[0:HEAD_DIM] / acc[HEAD_DIM:HEAD_DIM + 1])
        o_ref[0, :, hp * LANES:(hp + 1) * LANES] = jnp.concatenate(outs, axis=0).T.astype(o_ref.dtype)


def _moba_attention(q_t, k_aug, v_t, km, bias_tiles):
    b, h, s, _ = k_aug.shape
    t = MOBA_BLOCK
    g = HEADS_PER_STEP
    group = lambda bi, hg, qi: (bi, hg, 0, 0)
    once = pl.Buffered(1)
    return pl.pallas_call(
        _attn_kernel,
        grid=(b, h // g, s // t),
        in_specs=[
            pl.BlockSpec((1, g, LANES, t), lambda bi, hg, qi: (bi, hg, 0, qi)),
            pl.BlockSpec((1, g, s, LANES), group, pipeline_mode=once),
            pl.BlockSpec((1, g, V_ROWS, s), group, pipeline_mode=once),
            pl.BlockSpec((1, g, LANES, LANES), group),
            pl.BlockSpec((g, 2, t, t), lambda bi, hg, qi: (hg, 0, 0, 0), pipeline_mode=once),
        ],
        out_specs=pl.BlockSpec((1, t, g * HEAD_DIM), lambda bi, hg, qi: (bi, qi, hg)),
        out_shape=jax.ShapeDtypeStruct((b, s, WIDTH_A), BF16),
        scratch_shapes=[
            pltpu.VMEM((g, LANES, t), BF16),
            pltpu.VMEM((g, t, t), F32),
            pltpu.VMEM((g, t, t), F32),
            pltpu.VMEM((g, V_ROWS, t), F32),
        ],
        compiler_params=pltpu.CompilerParams(
            dimension_semantics=("arbitrary", "arbitrary", "arbitrary"),
            vmem_limit_bytes=VMEM_LIMIT),
        name="moba_attn",
    )(q_t, k_aug, v_t, km, bias_tiles)


def _rel_bucket(dist):
    n = jnp.maximum(dist, 0)
    max_exact = REL_BUCKETS // 2
    nf = jnp.maximum(n, 1).astype(F32)
    large = max_exact + (jnp.log(nf / max_exact) / math.log(REL_MAX_DIST / max_exact)
                         * (REL_BUCKETS - max_exact)).astype(jnp.int32)
    large = jnp.minimum(large, REL_BUCKETS - 1)
    return jnp.where(n < max_exact, n, large)


def _bias_tiles(rel_table):
    t = MOBA_BLOCK
    period = 3 * t - 1
    dist = jnp.arange(period) - (t - 1)
    onehot = _rel_bucket(dist)[None, :] == jnp.arange(REL_BUCKETS)[:, None]
    far = rel_table[_rel_bucket(jnp.array(2 * t, jnp.int32))]
    shifted = (rel_table - far[None, :]).T
    per_dist = jnp.sum(jnp.where(onehot[None], shifted[:, :, None], 0.0), axis=1)
    per_dist = jnp.where(dist >= 0, per_dist * LOG2E, NEG_INF).astype(F32)
    skew = jnp.tile(per_dist, (1, t + 1))[:, :t * (period + 1)].reshape(-1, t, period + 1)
    toeplitz = skew[:, ::-1, :2 * t]
    return jnp.stack([toeplitz[:, :, :t], toeplitz[:, :, t:]], axis=1)


def _mix_kernel(x_ref, a_ref, wpg_ref, bg_ref, wpool_ref, pscale_ref, woa_ref, wob_ref,
                wout_ref, g_ref, b_ref, h_ref, pbuf):
    si = pl.program_id(1)
    t = x_ref.shape[1]

    @pl.when(si == 0)
    def _():
        pbuf[0:HALO, :] = jnp.zeros((HALO, WIDTH_B), F32)

    @pl.when(si > 0)
    def _():
        pbuf[0:HALO, :] = pbuf[t:t + HALO, :]

    pbuf[HALO:HALO + t, :] = jnp.dot(x_ref[0].astype(BF16), wpg_ref[:, :WIDTH_B],
                                     preferred_element_type=F32)
    rows = [slice(c * MIX_CHUNK, (c + 1) * MIX_CHUNK) for c in range(t // MIX_CHUNK)]
    gate_logits = [jnp.dot(x_ref[0, r, :].astype(BF16), wpg_ref[:, WIDTH_B:],
                           preferred_element_type=F32) for r in rows]
    for r, gl in zip(rows, gate_logits):
        n = r.stop - r.start
        pos = si * t + r.start + lax.broadcasted_iota(jnp.int32, (n, 1), 0)
        mixed = []
        for gi, w in enumerate(POOL_WINDOWS):
            cols = slice(gi * POOL_GROUP_DIM, (gi + 1) * POOL_GROUP_DIM)
            own = pbuf[HALO + r.start:HALO + r.stop, cols]
            ws = own
            for dlt in range(1, w):
                ws = ws + pbuf[HALO + r.start - dlt:HALO + r.stop - dlt, cols]
            cnt = jnp.minimum(pos + 1, w).astype(F32)
            pooled = ws / cnt - own
            mixed.append(jnp.dot(pooled.astype(BF16), wpool_ref[gi],
                                 preferred_element_type=F32))
        mixed = jnp.concatenate(mixed, axis=1) * pscale_ref[...]
        branch_b = jnp.dot(mixed.astype(BF16), wob_ref[...], preferred_element_type=F32)
        branch_a = jnp.dot(a_ref[0, r, :], woa_ref[...], preferred_element_type=F32)
        gates = jax.nn.sigmoid(gl + bg_ref[...])
        mixin = gates[:, :D_MODEL] * branch_a + gates[:, D_MODEL:] * branch_b
        mix = jnp.dot(mixin.astype(BF16), wout_ref[...], preferred_element_type=F32)
        h_ref[0, r, :] = _layer_norm(DEEPNORM_ALPHA * x_ref[0, r, :] + mix,
                                     g_ref[...], b_ref[...])


def _mix_ln1(x, a, w_pg, b_gate, w_pool, pool_scale, w_oa, w_ob, w_out, ln_g, ln_b, t=1024):
    b, s, d = x.shape
    const2 = lambda bi, si: (0, 0)
    once = pl.Buffered(1)
    return pl.pallas_call(
        _mix_kernel,
        grid=(b, s // t),
        in_specs=[
            pl.BlockSpec((1, t, d), lambda bi, si: (bi, si, 0)),
            pl.BlockSpec((1, t, WIDTH_A), lambda bi, si: (bi, si, 0)),
            pl.BlockSpec(w_pg.shape, const2, pipeline_mode=once),
            pl.BlockSpec(b_gate.shape, const2),
            pl.BlockSpec(w_pool.shape, lambda bi, si: (0, 0, 0)),
            pl.BlockSpec(pool_scale.shape, const2),
            pl.BlockSpec(w_oa.shape, const2, pipeline_mode=once),
            pl.BlockSpec(w_ob.shape, const2, pipeline_mode=once),
            pl.BlockSpec(w_out.shape, const2, pipeline_mode=once),
            pl.BlockSpec(ln_g.shape, const2),
            pl.BlockSpec(ln_b.shape, const2),
        ],
        out_specs=pl.BlockSpec((1, t, d), lambda bi, si: (bi, si, 0)),
        out_shape=jax.ShapeDtypeStruct((b, s, d), F32),
        scratch_shapes=[pltpu.VMEM((HALO + t, WIDTH_B), F32)],
        compiler_params=pltpu.CompilerParams(
            dimension_semantics=("arbitrary", "arbitrary"),
            vmem_limit_bytes=VMEM_LIMIT),
        name="mix_ln1",
    )(x, a, w_pg, b_gate, w_pool, pool_scale, w_oa, w_ob, w_out, ln_g, ln_b)


FF_CHUNK = 1024
FF_ROWS = 256


def _ffn_kernel(h_ref, w1_ref, b1_ref, w2_ref, b2_ref, g_ref, b_ref, o_ref):
    for r0 in range(0, h_ref.shape[0], FF_ROWS):
        h = h_ref[r0:r0 + FF_ROWS, :]
        hb = h.astype(BF16)
        ff = jnp.zeros(h.shape, F32)
        for c in range(D_FF // FF_CHUNK):
            cs = slice(c * FF_CHUNK, (c + 1) * FF_CHUNK)
            u = jnp.dot(hb, w1_ref[:, cs], preferred_element_type=F32) + b1_ref[:, cs]
            u = jnp.square(jnp.maximum(u, 0.0))
            ff = ff + jnp.dot(u.astype(BF16), w2_ref[cs, :], preferred_element_type=F32)
        ff = ff + b2_ref[...]
        o_ref[r0:r0 + FF_ROWS, :] = _layer_norm(DEEPNORM_ALPHA * h + ff, g_ref[...], b_ref[...])


def _ffn_ln2(h, w1, b1, w2, b2, ln_g, ln_b, t=1024):
    n, d = h.shape
    const2 = lambda i: (0, 0)
    once = pl.Buffered(1)
    return pl.pallas_call(
        _ffn_kernel,
        grid=(n // t,),
        in_specs=[
            pl.BlockSpec((t, d), lambda i: (i, 0)),
            pl.BlockSpec(w1.shape, const2, pipeline_mode=once),
            pl.BlockSpec(b1.shape, const2),
            pl.BlockSpec(w2.shape, const2, pipeline_mode=once),
            pl.BlockSpec(b2.shape, const2),
            pl.BlockSpec(ln_g.shape, const2),
            pl.BlockSpec(ln_b.shape, const2),
        ],
        out_specs=pl.BlockSpec((t, d), lambda i: (i, 0)),
        out_shape=jax.ShapeDtypeStruct((n, d), F32),
        compiler_params=pltpu.CompilerParams(
            dimension_semantics=("arbitrary",),
            vmem_limit_bytes=VMEM_LIMIT),
        name="ffn_ln2",
    )(h, w1, b1, w2, b2, ln_g, ln_b)


def kernel(x, w_in, b_gate, rel_table, w_pool, pool_scale, w_o_attn, w_o_pool, w_out,
           ln1_g, ln1_b, w_ff1, b_ff1, w_ff2, b_ff2, ln2_g, ln2_b):
    b, s, d = x.shape
    nb = s // MOBA_BLOCK
    assert s % MOBA_BLOCK == 0 and nb <= MAX_KEY_BLOCKS
    bias_tiles = _bias_tiles(rel_table)
    h = x
    for l in range(w_in.shape[0]):
        w_q = w_in[l, :, :WIDTH_A] * (HEAD_DIM ** -0.5)
        w_k = w_in[l, :, WIDTH_A:2 * WIDTH_A].astype(BF16)
        w_v = w_in[l, :, 2 * WIDTH_A:3 * WIDTH_A]
        w_qvt = jnp.concatenate([w_q, w_v], axis=1).T.astype(BF16)
        w_pg = w_in[l, :, 3 * WIDTH_A:].astype(BF16)
        q_t, k_aug, v_t, kmean = _qkv_proj(h, w_k, w_qvt)
        km = kmean.reshape(b, nb, N_HEADS, HEAD_DIM).transpose(0, 2, 1, 3)
        km = jnp.pad(km, ((0, 0), (0, 0), (HEAD_DIM, LANES - HEAD_DIM - nb),
                          (0, LANES - HEAD_DIM))).astype(BF16)
        a = _moba_attention(q_t, k_aug, v_t, km, bias_tiles)
        h = _mix_ln1(h, a, w_pg, b_gate[l][None, :], w_pool[l].astype(BF16),
                     pool_scale[l][None, :], w_o_attn[l].astype(BF16),
                     w_o_pool[l].astype(BF16), w_out[l].astype(BF16),
                     ln1_g[l][None, :], ln1_b[l][None, :])
        h = _ffn_ln2(h.reshape(b * s, d), w_ff1[l].astype(BF16), b_ff1[l][None, :],
                     w_ff2[l].astype(BF16), b_ff2[l][None, :],
                     ln2_g[l][None, :], ln2_b[l][None, :]).reshape(b, s, d)
    return h
```

```python
import math

import jax
import jax.numpy as jnp
from jax import lax
from jax.experimental import pallas as pl
from jax.experimental.pallas import tpu as pltpu

D_MODEL = 1024
N_HEADS = 8
HEAD_DIM = 64
WIDTH_A = N_HEADS * HEAD_DIM
MOBA_BLOCK = 256
MOBA_TOPK = 3
N_POOL_GROUPS = 4
POOL_WINDOWS = (2, 4, 8, 16)
WIDTH_B = 512
POOL_GROUP_DIM = WIDTH_B // N_POOL_GROUPS
D_FF = 4 * D_MODEL
REL_BUCKETS = 32
REL_MAX_DIST = 128
DEEPNORM_ALPHA = 2.0 ** 0.25
LN_EPS = 1e-5
NEG_INF = -1e30

LANES = 128
MAX_KEY_BLOCKS = 32
HEADS_PER_STEP = 8
V_ROWS = HEAD_DIM + 16
LOG2E = math.log2(math.e)
HALO = 16
MIX_CHUNK = 256
VMEM_LIMIT = 48 * 1024 * 1024

BF16 = jnp.bfloat16
F32 = jnp.float32
NT_DIMS = (((1,), (1,)), ((), ()))


def _layer_norm(y, g, b):
    mu = jnp.mean(y, axis=-1, keepdims=True)
    d = y - mu
    var = jnp.mean(d * d, axis=-1, keepdims=True)
    return d * lax.rsqrt(var + LN_EPS) * g + b


def _qkv_kernel(x_ref, wk_ref, wqvt_ref, q_ref, k_ref, v_ref, kmean_sc):
    si = pl.program_id(1)

    @pl.when(si == 0)
    def _():
        kmean_sc[...] = jnp.zeros(kmean_sc.shape, F32)

    xb = x_ref[0].astype(BF16)
    t = xb.shape[0]
    blocks_per_tile = t // MOBA_BLOCK
    acc_k =jnp.dot(xb, wk_ref[...], preferred_element_type=F32)
    acc_q = lax.dot_general(wqvt_ref[0:WIDTH_A, :], xb, NT_DIMS,
                            preferred_element_type=F32)
    lane = lax.broadcasted_iota(jnp.int32, (t, LANES), 1)
    row = lax.broadcasted_iota(jnp.int32, (t, LANES), 0)
    lo = lane < HEAD_DIM
    onehot = jnp.where(lane == HEAD_DIM + si * blocks_per_tile + row // MOBA_BLOCK, 1.0, 0.0)
    for hp in range(N_HEADS // 2):
        kp = acc_k[:, hp * LANES:(hp + 1) * LANES] * LOG2E
        kr = pltpu.roll(kp, HEAD_DIM, 1)
        k_ref[0, 2 * hp] = jnp.where(lo, kp, onehot).astype(BF16)
        k_ref[0, 2 * hp + 1] = jnp.where(lo, kr, onehot).astype(BF16)

    blk_row =lax.broadcasted_iota(jnp.int32, (MAX_KEY_BLOCKS, LANES), 0)
    lo_row = lax.broadcasted_iota(jnp.int32, (MAX_KEY_BLOCKS, LANES), 1) < HEAD_DIM
    for c in range(blocks_per_tile):
        mean_c = jnp.mean(acc_k[c * MOBA_BLOCK:(c + 1) * MOBA_BLOCK], axis=0, keepdims=True)
        for hp in range(N_HEADS // 2):
            pair = mean_c[:, hp * LANES:(hp + 1) * LANES]
            for h, m in ((2 * hp, pair), (2 * hp + 1, pltpu.roll(pair, HEAD_DIM, 1))):
                new_row = jnp.where(lo_row, jnp.broadcast_to(m, (MAX_KEY_BLOCKS, LANES)), 0.0)
                kmean_sc[h] = jnp.where(blk_row == si * blocks_per_tile + c, new_row, kmean_sc[h])

    sel_row = lax.broadcasted_iota(jnp.int32, (MAX_KEY_BLOCKS, t), 0)
    own_blk = si * blocks_per_tile + lax.broadcasted_iota(
        jnp.int32, (MAX_KEY_BLOCKS, t), 1) // MOBA_BLOCK
    past = sel_row < own_blk
    zeros64 = jnp.zeros((LANES - HEAD_DIM, t), BF16)
    zeros32 = jnp.zeros((LANES - HEAD_DIM - MAX_KEY_BLOCKS, t), BF16)
    pad1 = jnp.ones((V_ROWS - HEAD_DIM, t), BF16)
    for h in range(N_HEADS):
        qt = acc_q[h * HEAD_DIM:(h + 1) * HEAD_DIM].astype(BF16)
        gate = jnp.dot(kmean_sc[h].astype(BF16), jnp.concatenate([qt, zeros64], axis=0),
                       preferred_element_type=F32)
        g = jnp.where(past, gate, NEG_INF)
        sel = jnp.zeros(g.shape, jnp.bool_)
        for _ in range(MOBA_TOPK):
            mx = jnp.max(g, axis=0, keepdims=True)
            first = jnp.min(jnp.where(g == mx, sel_row, MAX_KEY_BLOCKS), axis=0, keepdims=True)
            pick = sel_row == first
            sel = sel | pick
            g = jnp.where(pick, -jnp.inf, g)
        sel = (sel & past) | (sel_row == own_blk)
        q_ref[0, h, 0:HEAD_DIM, :] = qt
        q_ref[0, h, HEAD_DIM:HEAD_DIM + MAX_KEY_BLOCKS, :] = jnp.where(sel, 0.0, NEG_INF).astype(BF16)
        q_ref[0, h, HEAD_DIM + MAX_KEY_BLOCKS:LANES, :] = zeros32

    acc_v = lax.dot_general(wqvt_ref[WIDTH_A:2 * WIDTH_A, :], xb, NT_DIMS,
                            preferred_element_type=F32)
    for h in range(N_HEADS):
        v_ref[0, h, 0:HEAD_DIM, :] = acc_v[h * HEAD_DIM:(h + 1) * HEAD_DIM].astype(BF16)
        v_ref[0, h, HEAD_DIM:V_ROWS, :] = pad1


def _qkv_proj(x, w_k, w_qvt, t=2 * MOBA_BLOCK):
    b, s, d = x.shape
    return pl.pallas_call(
        _qkv_kernel,
        grid=(b, s // t),
        in_specs=[
            pl.BlockSpec((1, t, d), lambda bi, si: (bi, si, 0)),
            pl.BlockSpec(w_k.shape, lambda bi, si: (0, 0)),
            pl.BlockSpec(w_qvt.shape, lambda bi, si: (0, 0)),
        ],
        out_specs=[
            pl.BlockSpec((1, N_HEADS, LANES, t), lambda bi, si: (bi, 0, 0, si)),
            pl.BlockSpec((1, N_HEADS, t, LANES), lambda bi, si: (bi, 0, si, 0)),
            pl.BlockSpec((1, N_HEADS, V_ROWS, t), lambda bi, si: (bi, 0, 0, si)),
        ],
        out_shape=[
            jax.ShapeDtypeStruct((b, N_HEADS, LANES, s), BF16),
            jax.ShapeDtypeStruct((b, N_HEADS, s, LANES), BF16),
            jax.ShapeDtypeStruct((b, N_HEADS, V_ROWS, s), BF16),
        ],
        scratch_shapes=[pltpu.VMEM((N_HEADS, MAX_KEY_BLOCKS, LANES), F32)],
        compiler_params=pltpu.CompilerParams(
            dimension_semantics=("arbitrary", "arbitrary"),
            vmem_limit_bytes=VMEM_LIMIT),
        name="qkv_proj",
    )(x, w_k, w_qvt)


def _attn_kernel(q_ref, k_ref, v_ref, bias_ref, o_ref, s_a, s_b, acc_sc):
    i = pl.program_id(2)
    t = MOBA_BLOCK
    heads = range(HEADS_PER_STEP)

    def key_block(h, j):
        return k_ref[0, h, pl.ds(pl.multiple_of(j * t, t), t), :]

    def head_scores_into(s_dst, h, j):
        s = jnp.dot(key_block(h, j), q_ref[0, h], preferred_element_type=F32)
        s_dst[h] = s
        return jnp.max(s, axis=0, keepdims=True)

    def weighted_values(h, j, p):
        vv = v_ref[0, h, :, pl.ds(pl.multiple_of(j * t, t), t)]
        return jnp.dot(vv, p.astype(BF16), preferred_element_type=F32)

    def step(j, ms, s_cur, max_cur, s_nxt, bias_idx=None):
        new_ms, max_nxt = [], []
        for h in heads:
            if s_nxt is not None:
                max_nxt.append(head_scores_into(s_nxt, h, j + 1))
            s = s_cur[h]
            if bias_idx is not None:
                s = s + bias_ref[h, bias_idx]
                m_new = jnp.maximum(ms[h], jnp.max(s, axis=0, keepdims=True))
            else:
                m_new = jnp.maximum(ms[h], max_cur[h])
            alpha = jnp.exp2(ms[h] - m_new)
            acc_sc[h] = alpha * acc_sc[h] + weighted_values(h, j, jnp.exp2(s - m_new))
            new_ms.append(m_new)
        return new_ms, max_nxt

    ms = []
    for h in heads:
        s = jnp.dot(key_block(h, i), q_ref[0, h], preferred_element_type=F32) + bias_ref[h, 0]
        s_b[h] = s
        ms.append(jnp.max(s, axis=0, keepdims=True))

    max_a = []
    for h in heads:
        max_a.append(head_scores_into(s_a, h, 0))
        acc_sc[h] = weighted_values(h, i, jnp.exp2(s_b[h] - ms[h]))

    n_far = jnp.maximum(i - 1, 0)

    def two_steps(jj, carry):
        ms, max_a = carry
        ms, max_b = step(2 * jj, list(ms), s_a, list(max_a), s_b)
        ms, max_a = step(2 * jj + 1, ms, s_b, max_b, s_a)
        return tuple(ms), tuple(max_a)

    ms, max_a = lax.fori_loop(0, n_far // 2, two_steps, (tuple(ms), tuple(max_a)))
    odd = n_far % 2 == 1

    @pl.when((i >= 1) & jnp.logical_not(odd))
    def _():
        step(i - 1, list(ms), s_a, None, None, bias_idx=1)

    @pl.when(odd)
    def _():
        ms1, _ = step(i - 2, list(ms), s_a, list(max_a), s_b)
        step(i - 1, ms1, s_b, None, None, bias_idx=1)

    for hp in range(HEADS_PER_STEP // 2):
        outs = []
        for h in (2 * hp, 2 * hp + 1):
            acc = acc_sc[h]
            outs.append(acc[0:HEAD_DIM] / acc[HEAD_DIM:HEAD_DIM + 1])
        o_ref[0, :, hp * LANES:(hp + 1) * LANES] = jnp.concatenate(outs, axis=0).T.astype(o_ref.dtype)


def _moba_attention(q_t, k_aug, v_t, bias_tiles):
    b, h, s, _ = k_aug.shape
    t = MOBA_BLOCK
    g = HEADS_PER_STEP
    group = lambda bi, hg, qi: (bi, hg, 0, 0)
    once = pl.Buffered(1)
    return pl.pallas_call(
        _attn_kernel,
        grid=(b, h // g, s // t),
        in_specs=[
            pl.BlockSpec((1, g, LANES, t), lambda bi, hg, qi: (bi, hg, 0, qi)),
            pl.BlockSpec((1, g, s, LANES), group, pipeline_mode=once),
            pl.BlockSpec((1, g, V_ROWS, s), group, pipeline_mode=once),
            pl.BlockSpec((g, 2, t, t), lambda bi, hg, qi: (hg, 0, 0, 0), pipeline_mode=once),
        ],
        out_specs=pl.BlockSpec((1, t, g * HEAD_DIM), lambda bi, hg, qi: (bi, qi, hg)),
        out_shape=jax.ShapeDtypeStruct((b, s, WIDTH_A), BF16),
        scratch_shapes=[
            pltpu.VMEM((g, t, t), F32),
            pltpu.VMEM((g, t, t), F32),
            pltpu.VMEM((g, V_ROWS, t), F32),
        ],
        compiler_params=pltpu.CompilerParams(
            dimension_semantics=("arbitrary", "arbitrary", "arbitrary"),
            vmem_limit_bytes=VMEM_LIMIT),
        name="moba_attn",
    )(q_t, k_aug, v_t, bias_tiles)


def _rel_bucket(dist):
    n = jnp.maximum(dist, 0)
    max_exact = REL_BUCKETS // 2
    nf = jnp.maximum(n, 1).astype(F32)
    large = max_exact + (jnp.log(nf / max_exact) / math.log(REL_MAX_DIST / max_exact)
                         * (REL_BUCKETS - max_exact)).astype(jnp.int32)
    large = jnp.minimum(large, REL_BUCKETS - 1)
    return jnp.where(n < max_exact, n, large)


def _bias_tiles(rel_table):
    t = MOBA_BLOCK
    period = 3 * t - 1
    dist = jnp.arange(period)
    dist = jnp.where(dist < 2 * t, dist, dist - period)
    onehot = _rel_bucket(dist)[None, :] == jnp.arange(REL_BUCKETS)[:, None]
    far = rel_table[_rel_bucket(jnp.array(2 * t, jnp.int32))]
    shifted = (rel_table - far[None, :]).T
    per_dist = jnp.sum(jnp.where(onehot[None], shifted[:, :, None], 0.0), axis=1)
    per_dist = jnp.where(dist >= 0, per_dist * LOG2E, NEG_INF).astype(F32)
    skew = jnp.tile(per_dist, (1, t))[:, :t * (period - 1)].reshape(-1, t, period - 1)
    toeplitz = skew[:, :, :2 * t]
    return jnp.stack([toeplitz[:, :, :t], toeplitz[:, :, t:]], axis=1)


def _mix_kernel(x_ref, a_ref, wpg_ref, bg_ref, wpool_ref, pscale_ref, woa_ref, wob_ref,
                wout_ref, g_ref, b_ref, h_ref, pbuf):
    si = pl.program_id(1)
    t = x_ref.shape[1]

    @pl.when(si == 0)
    def _():
        pbuf[0:HALO, :] = jnp.zeros((HALO, WIDTH_B), F32)

    @pl.when(si > 0)
    def _():
        pbuf[0:HALO, :] = pbuf[t:t + HALO, :]

    pbuf[HALO:HALO + t, :] = jnp.dot(x_ref[0].astype(BF16), wpg_ref[:, :WIDTH_B],
                                     preferred_element_type=F32)
    rows = [slice(c * MIX_CHUNK, (c + 1) * MIX_CHUNK) for c in range(t // MIX_CHUNK)]
    gate_logits = [jnp.dot(x_ref[0, r, :].astype(BF16), wpg_ref[:, WIDTH_B:],
                           preferred_element_type=F32) for r in rows]
    for r, gl in zip(rows, gate_logits):
        n = r.stop - r.start
        pos = si * t + r.start + lax.broadcasted_iota(jnp.int32, (n, 1), 0)
        mixed = []
        for gi, w in enumerate(POOL_WINDOWS):
            cols = slice(gi * POOL_GROUP_DIM, (gi + 1) * POOL_GROUP_DIM)
            own = pbuf[HALO + r.start:HALO + r.stop, cols]
            ws = own
            for dlt in range(1, w):
                ws = ws + pbuf[HALO + r.start - dlt:HALO + r.stop - dlt, cols]
            cnt = jnp.minimum(pos + 1, w).astype(F32)
            pooled = ws / cnt - own
            mixed.append(jnp.dot(pooled.astype(BF16), wpool_ref[gi],
                                 preferred_element_type=F32))
        mixed = jnp.concatenate(mixed, axis=1) * pscale_ref[...]
        branch_b = jnp.dot(mixed.astype(BF16), wob_ref[...], preferred_element_type=F32)
        branch_a = jnp.dot(a_ref[0, r, :], woa_ref[...], preferred_element_type=F32)
        gates = jax.nn.sigmoid(gl + bg_ref[...])
        mixin = gates[:, :D_MODEL] * branch_a + gates[:, D_MODEL:] * branch_b
        mix = jnp.dot(mixin.astype(BF16), wout_ref[...], preferred_element_type=F32)
        h_ref[0, r, :] = _layer_norm(DEEPNORM_ALPHA * x_ref[0, r, :] + mix,
                                     g_ref[...], b_ref[...])


def _mix_ln1(x, a, w_pg, b_gate, w_pool, pool_scale, w_oa, w_ob, w_out, ln_g, ln_b, t=1024):
    b, s, d = x.shape
    const2 = lambda bi, si: (0, 0)
    once = pl.Buffered(1)
    return pl.pallas_call(
        _mix_kernel,
        grid=(b, s // t),
        in_specs=[
            pl.BlockSpec((1, t, d), lambda bi, si: (bi, si, 0)),
            pl.BlockSpec((1, t, WIDTH_A), lambda bi, si: (bi, si, 0)),
            pl.BlockSpec(w_pg.shape, const2, pipeline_mode=once),
            pl.BlockSpec(b_gate.shape, const2),
            pl.BlockSpec(w_pool.shape, lambda bi, si: (0, 0, 0)),
            pl.BlockSpec(pool_scale.shape, const2),
            pl.BlockSpec(w_oa.shape, const2, pipeline_mode=once),
            pl.BlockSpec(w_ob.shape, const2, pipeline_mode=once),
            pl.BlockSpec(w_out.shape, const2, pipeline_mode=once),
            pl.BlockSpec(ln_g.shape, const2),
            pl.BlockSpec(ln_b.shape, const2),
        ],
        out_specs=pl.BlockSpec((1, t, d), lambda bi, si: (bi, si, 0)),
        out_shape=jax.ShapeDtypeStruct((b, s, d), F32),
        scratch_shapes=[pltpu.VMEM((HALO + t, WIDTH_B), F32)],
        compiler_params=pltpu.CompilerParams(
            dimension_semantics=("arbitrary", "arbitrary"),
            vmem_limit_bytes=VMEM_LIMIT),
        name="mix_ln1",
    )(x, a, w_pg, b_gate, w_pool, pool_scale, w_oa, w_ob, w_out, ln_g, ln_b)


FF_CHUNK = 1024
FF_ROWS = 256


def _ffn_kernel(h_ref, w1_ref, b1_ref, w2_ref, b2_ref, g_ref, b_ref, o_ref):
    for r0 in range(0, h_ref.shape[0], FF_ROWS):
        h = h_ref[r0:r0 + FF_ROWS, :]
        hb = h.astype(BF16)
        ff = jnp.zeros(h.shape, F32)
        for c in range(D_FF // FF_CHUNK):
            cs = slice(c * FF_CHUNK, (c + 1) * FF_CHUNK)
            u = jnp.dot(hb, w1_ref[:, cs], preferred_element_type=F32) + b1_ref[:, cs]
            u = jnp.square(jnp.maximum(u, 0.0))
            ff = ff + jnp.dot(u.astype(BF16), w2_ref[cs, :], preferred_element_type=F32)
        ff = ff + b2_ref[...]
        o_ref[r0:r0 + FF_ROWS, :] = _layer_norm(DEEPNORM_ALPHA * h + ff, g_ref[...], b_ref[...])


def _ffn_ln2(h, w1, b1, w2, b2, ln_g, ln_b, t=1024):
    n, d = h.shape
    const2 = lambda i: (0, 0)
    once = pl.Buffered(1)
    return pl.pallas_call(
        _ffn_kernel,
        grid=(n // t,),
        in_specs=[
            pl.BlockSpec((t, d), lambda i: (i, 0)),
            pl.BlockSpec(w1.shape, const2, pipeline_mode=once),
            pl.BlockSpec(b1.shape, const2),
            pl.BlockSpec(w2.shape, const2, pipeline_mode=once),
            pl.BlockSpec(b2.shape, const2),
            pl.BlockSpec(ln_g.shape, const2),
            pl.BlockSpec(ln_b.shape, const2),
        ],
        out_specs=pl.BlockSpec((t, d), lambda i: (i, 0)),
        out_shape=jax.ShapeDtypeStruct((n, d), F32),
        compiler_params=pltpu.CompilerParams(
            dimension_semantics=("arbitrary",),
            vmem_limit_bytes=VMEM_LIMIT),
        name="ffn_ln2",
    )(h, w1, b1, w2, b2, ln_g, ln_b)


def kernel(x, w_in, b_gate, rel_table, w_pool, pool_scale, w_o_attn, w_o_pool, w_out,
           ln1_g, ln1_b, w_ff1, b_ff1, w_ff2, b_ff2, ln2_g, ln2_b):
    b, s, d = x.shape
    nb = s // MOBA_BLOCK
    assert s % MOBA_BLOCK == 0 and nb <= MAX_KEY_BLOCKS
    bias_tiles = _bias_tiles(rel_table)
    h = x
    for l in range(w_in.shape[0]):
        w_q = w_in[l, :, :WIDTH_A] * (HEAD_DIM ** -0.5)
        w_k = w_in[l, :, WIDTH_A:2 * WIDTH_A].astype(BF16)
        w_v = w_in[l, :, 2 * WIDTH_A:3 * WIDTH_A]
        w_qvt = jnp.concatenate([w_q, w_v], axis=1).T.astype(BF16)
        w_pg = w_in[l, :, 3 * WIDTH_A:].astype(BF16)
        q_t, k_aug, v_t = _qkv_proj(h, w_k, w_qvt)
        a = _moba_attention(q_t, k_aug, v_t, bias_tiles)
        h = _mix_ln1(h, a, w_pg, b_gate[l][None, :], w_pool[l].astype(BF16),
                     pool_scale[l][None, :], w_o_attn[l].astype(BF16),
                     w_o_pool[l].astype(BF16), w_out[l].astype(BF16),
                     ln1_g[l][None, :], ln1_b[l][None, :])
        h = _ffn_ln2(h.reshape(b * s, d), w_ff1[l].astype(BF16), b_ff1[l][None, :],
                     w_ff2[l].astype(BF16), b_ff2[l][None, :],
                     ln2_g[l][None, :], ln2_b[l][None, :]).reshape(b, s, d)
    return h
```

```python
import math

import jax
import jax.numpy as jnp
from jax import lax
from jax.experimental import pallas as pl
from jax.experimental.pallas import tpu as pltpu

D_MODEL = 1024
N_HEADS = 8
HEAD_DIM = 64
WIDTH_A = N_HEADS * HEAD_DIM
MOBA_BLOCK = 256
MOBA_TOPK = 3
N_POOL_GROUPS = 4
POOL_WINDOWS = (2, 4, 8, 16)
WIDTH_B = 512
POOL_GROUP_DIM = WIDTH_B // N_POOL_GROUPS
D_FF = 4 * D_MODEL
REL_BUCKETS = 32
REL_MAX_DIST = 128
DEEPNORM_ALPHA = 2.0 ** 0.25
LN_EPS = 1e-5
NEG_INF = -1e30

LANES = 128
MAX_KEY_BLOCKS = 32
HEADS_PER_STEP = 8
V_ROWS = HEAD_DIM + 16
LOG2E = math.log2(math.e)
HALO = 16
MIX_CHUNK = 256
VMEM_LIMIT = 48 * 1024 * 1024

BF16 = jnp.bfloat16
F32 = jnp.float32
NT_DIMS = (((1,), (1,)), ((), ()))


def _layer_norm(y, g, b):
    mu = jnp.mean(y, axis=-1, keepdims=True)
    d = y - mu
    var = jnp.mean(d * d, axis=-1, keepdims=True)
    return d * lax.rsqrt(var + LN_EPS) * g + b


def _qkv_kernel(x_ref, wk_ref, wqvt_ref, q_ref, k_ref, v_ref, kmean_sc):
    si = pl.program_id(1)

    @pl.when(si == 0)
    def _():
        kmean_sc[...] = jnp.zeros(kmean_sc.shape, F32)

    xb = x_ref[0].astype(BF16)
    t = xb.shape[0]
    blocks_per_tile = t // MOBA_BLOCK
    acc_k =jnp.dot(xb, wk_ref[...], preferred_element_type=F32)
    acc_q = lax.dot_general(wqvt_ref[0:WIDTH_A, :], xb, NT_DIMS,
                            preferred_element_type=F32)
    lane = lax.broadcasted_iota(jnp.int32, (t, LANES), 1)
    row = lax.broadcasted_iota(jnp.int32, (t, LANES), 0)
    lo = lane < HEAD_DIM
    onehot = jnp.where(lane == HEAD_DIM + si * blocks_per_tile + row // MOBA_BLOCK, 1.0, 0.0)
    for hp in range(N_HEADS // 2):
        kp = acc_k[:, hp * LANES:(hp + 1) * LANES] * LOG2E
        kr = pltpu.roll(kp, HEAD_DIM, 1)
        k_ref[0, 2 * hp] = jnp.where(lo, kp, onehot).astype(BF16)
        k_ref[0, 2 * hp + 1] = jnp.where(lo, kr, onehot).astype(BF16)

    blk_row =lax.broadcasted_iota(jnp.int32, (MAX_KEY_BLOCKS, LANES), 0)
    lo_row = lax.broadcasted_iota(jnp.int32, (MAX_KEY_BLOCKS, LANES), 1) < HEAD_DIM
    for c in range(blocks_per_tile):
        mean_c = jnp.mean(acc_k[c * MOBA_BLOCK:(c + 1) * MOBA_BLOCK], axis=0, keepdims=True)
        for hp in range(N_HEADS // 2):
            pair = mean_c[:, hp * LANES:(hp + 1) * LANES]
            for h, m in ((2 * hp, pair), (2 * hp + 1, pltpu.roll(pair, HEAD_DIM, 1))):
                new_row = jnp.where(lo_row, jnp.broadcast_to(m, (MAX_KEY_BLOCKS, LANES)), 0.0)
                kmean_sc[h] = jnp.where(blk_row == si * blocks_per_tile + c, new_row, kmean_sc[h])

    sel_row = lax.broadcasted_iota(jnp.int32, (MAX_KEY_BLOCKS, t), 0)
    own_blk = si * blocks_per_tile + lax.broadcasted_iota(
        jnp.int32, (MAX_KEY_BLOCKS, t), 1) // MOBA_BLOCK
    past = sel_row < own_blk
    zeros64 = jnp.zeros((LANES - HEAD_DIM, t), BF16)
    zeros32 = jnp.zeros((LANES - HEAD_DIM - MAX_KEY_BLOCKS, t), BF16)
    pad1 = jnp.ones((V_ROWS - HEAD_DIM, t), BF16)
    for h in range(N_HEADS):
        qt = acc_q[h * HEAD_DIM:(h + 1) * HEAD_DIM].astype(BF16)
        gate = jnp.dot(kmean_sc[h].astype(BF16), jnp.concatenate([qt, zeros64], axis=0),
                       preferred_element_type=F32)
        g = jnp.where(past, gate, NEG_INF)
        sel = jnp.zeros(g.shape, jnp.bool_)
        for _ in range(MOBA_TOPK):
            mx = jnp.max(g, axis=0, keepdims=True)
            first = jnp.min(jnp.where(g == mx, sel_row, MAX_KEY_BLOCKS), axis=0, keepdims=True)
            pick = sel_row == first
            sel = sel | pick
            g = jnp.where(pick, -jnp.inf, g)
        sel = (sel & past) | (sel_row == own_blk)
        q_ref[0, h, 0:HEAD_DIM, :] = qt
        q_ref[0, h, HEAD_DIM:HEAD_DIM + MAX_KEY_BLOCKS, :] = jnp.where(sel, 0.0, NEG_INF).astype(BF16)
        q_ref[0, h, HEAD_DIM + MAX_KEY_BLOCKS:LANES, :] = zeros32

    acc_v = lax.dot_general(wqvt_ref[WIDTH_A:2 * WIDTH_A, :], xb, NT_DIMS,
                            preferred_element_type=F32)
    for h in range(N_HEADS):
        v_ref[0, h, 0:HEAD_DIM, :] = acc_v[h * HEAD_DIM:(h + 1) * HEAD_DIM].astype(BF16)
        v_ref[0, h, HEAD_DIM:V_ROWS, :] = pad1


def _qkv_proj(x, w_k, w_qvt, t=2 * MOBA_BLOCK):
    b, s, d = x.shape
    return pl.pallas_call(
        _qkv_kernel,
        grid=(b, s // t),
        in_specs=[
            pl.BlockSpec((1, t, d), lambda bi, si: (bi, si, 0)),
            pl.BlockSpec(w_k.shape, lambda bi, si: (0, 0)),
            pl.BlockSpec(w_qvt.shape, lambda bi, si: (0, 0)),
        ],
        out_specs=[
            pl.BlockSpec((1, N_HEADS, LANES, t), lambda bi, si: (bi, 0, 0, si)),
            pl.BlockSpec((1, N_HEADS, t, LANES), lambda bi, si: (bi, 0, si, 0)),
            pl.BlockSpec((1, N_HEADS, V_ROWS, t), lambda bi, si: (bi, 0, 0, si)),
        ],
        out_shape=[
            jax.ShapeDtypeStruct((b, N_HEADS, LANES, s), BF16),
            jax.ShapeDtypeStruct((b, N_HEADS, s, LANES), BF16),
            jax.ShapeDtypeStruct((b, N_HEADS, V_ROWS, s), BF16),
        ],
        scratch_shapes=[pltpu.VMEM((N_HEADS, MAX_KEY_BLOCKS, LANES), F32)],
        compiler_params=pltpu.CompilerParams(
            dimension_semantics=("arbitrary", "arbitrary"),
            vmem_limit_bytes=VMEM_LIMIT),
        name="qkv_proj",
    )(x, w_k, w_qvt)


def _attn_kernel(q_ref, k_ref, v_ref, bias_ref, o_ref, s_a, s_b, acc_sc, m_sc, max_sc):
    i = pl.program_id(2)
    t = MOBA_BLOCK
    heads = range(HEADS_PER_STEP)

    def key_block(h, j):
        return k_ref[0, h, pl.ds(pl.multiple_of(j * t, t), t), :]

    def head_scores_into(s_dst, h, j):
        s = jnp.dot(key_block(h, j), q_ref[0, h], preferred_element_type=F32)
        s_dst[h] = s
        return jnp.max(s, axis=0, keepdims=True)

    def weighted_values(h, j, p):
        vv = v_ref[0, h, :, pl.ds(pl.multiple_of(j * t, t), t)]
        return jnp.dot(vv, p.astype(BF16), preferred_element_type=F32)

    def step(j, ms, s_cur, max_cur, s_nxt, bias_idx=None):
        new_ms, max_nxt = [], []
        for h in heads:
            if s_nxt is not None:
                max_nxt.append(head_scores_into(s_nxt, h, j + 1))
            s = s_cur[h]
            if bias_idx is not None:
                s = s + bias_ref[h, bias_idx]
                m_new = jnp.maximum(ms[h], jnp.max(s, axis=0, keepdims=True))
            else:
                m_new = jnp.maximum(ms[h], max_cur[h])
            alpha = jnp.exp2(ms[h] - m_new)
            acc_sc[h] = alpha * acc_sc[h] + weighted_values(h, j, jnp.exp2(s - m_new))
            new_ms.append(m_new)
        return new_ms, max_nxt

    ms = []
    for h in heads:
        s = jnp.dot(key_block(h, i), q_ref[0, h], preferred_element_type=F32) + bias_ref[h, 0]
        s_b[h] = s
        ms.append(jnp.max(s, axis=0, keepdims=True))

    max_a = []
    for h in heads:
        max_a.append(head_scores_into(s_a, h, 0))
        acc_sc[h] = weighted_values(h, i, jnp.exp2(s_b[h] - ms[h]))

    n_far = jnp.maximum(i - 1, 0)

    def two_steps(j, ms, max_a):
        ms, max_b = step(j, list(ms), s_a, list(max_a), s_b)
        return step(j + 1, ms, s_b, max_b, s_a)

    def four_steps(jj, carry):
        ms, max_a = two_steps(4 * jj, *carry)
        ms, max_a = two_steps(4 * jj + 2, ms, max_a)
        return tuple(ms), tuple(max_a)

    ms, max_a = lax.fori_loop(0, n_far // 4, four_steps, (tuple(ms), tuple(max_a)))
    for h in heads:
        m_sc[h] = ms[h]
        max_sc[h] = max_a[h]
    left = n_far % 4
    j_left = n_far - left

    @pl.when(left >= 2)
    def _():
        ms, max_a = two_steps(j_left, [m_sc[h] for h in heads], [max_sc[h] for h in heads])
        for h in heads:
            m_sc[h] = ms[h]
            max_sc[h] = max_a[h]

    @pl.when((i >= 1) & (left % 2 == 0))
    def _():
        step(i - 1, [m_sc[h] for h in heads], s_a, None, None, bias_idx=1)

    @pl.when(left % 2 == 1)
    def _():
        ms1, _ = step(i - 2, [m_sc[h] for h in heads], s_a, [max_sc[h] for h in heads], s_b)
        step(i - 1, ms1, s_b, None, None, bias_idx=1)

    for hp in range(HEADS_PER_STEP // 2):
        outs = []
        for h in (2 * hp, 2 * hp + 1):
            acc = acc_sc[h]
            outs.append(acc[0:HEAD_DIM] / acc[HEAD_DIM:HEAD_DIM + 1])
        o_ref[0, :, hp * LANES:(hp + 1) * LANES] = jnp.concatenate(outs, axis=0).T.astype(o_ref.dtype)


def _moba_attention(q_t, k_aug, v_t, bias_tiles):
    b, h, s, _ = k_aug.shape
    t = MOBA_BLOCK
    g = HEADS_PER_STEP
    group = lambda bi, hg, qi: (bi, hg, 0, 0)
    once = pl.Buffered(1)
    return pl.pallas_call(
        _attn_kernel,
        grid=(b, h // g, s // t),
        in_specs=[
            pl.BlockSpec((1, g, LANES, t), lambda bi, hg, qi: (bi, hg, 0, qi)),
            pl.BlockSpec((1, g, s, LANES), group, pipeline_mode=once),
            pl.BlockSpec((1, g, V_ROWS, s), group, pipeline_mode=once),
            pl.BlockSpec((g, 2, t, t), lambda bi, hg, qi: (hg, 0, 0, 0), pipeline_mode=once),
        ],
        out_specs=pl.BlockSpec((1, t, g * HEAD_DIM), lambda bi, hg, qi: (bi, qi, hg)),
        out_shape=jax.ShapeDtypeStruct((b, s, WIDTH_A), BF16),
        scratch_shapes=[
            pltpu.VMEM((g, t, t), F32),
            pltpu.VMEM((g, t, t), F32),
            pltpu.VMEM((g, V_ROWS, t), F32),
            pltpu.VMEM((g, 1, t), F32),
            pltpu.VMEM((g, 1, t), F32),
        ],
        compiler_params=pltpu.CompilerParams(
            dimension_semantics=("arbitrary", "arbitrary", "arbitrary"),
            vmem_limit_bytes=VMEM_LIMIT),
        name="moba_attn",
    )(q_t, k_aug, v_t, bias_tiles)


def _rel_bucket(dist):
    n = jnp.maximum(dist, 0)
    max_exact = REL_BUCKETS // 2
    nf = jnp.maximum(n, 1).astype(F32)
    large = max_exact + (jnp.log(nf / max_exact) / math.log(REL_MAX_DIST / max_exact)
                         * (REL_BUCKETS - max_exact)).astype(jnp.int32)
    large = jnp.minimum(large, REL_BUCKETS - 1)
    return jnp.where(n < max_exact, n, large)


def _bias_tiles(rel_table):
    t = MOBA_BLOCK
    period = 3 * t - 1
    dist = jnp.arange(period)
    dist = jnp.where(dist < 2 * t, dist, dist - period)
    onehot = _rel_bucket(dist)[None, :] == jnp.arange(REL_BUCKETS)[:, None]
    far = rel_table[_rel_bucket(jnp.array(2 * t, jnp.int32))]
    shifted = (rel_table - far[None, :]).T
    per_dist = jnp.sum(jnp.where(onehot[None], shifted[:, :, None], 0.0), axis=1)
    per_dist = jnp.where(dist >= 0, per_dist * LOG2E, NEG_INF).astype(F32)
    skew = jnp.tile(per_dist, (1, t))[:, :t * (period - 1)].reshape(-1, t, period - 1)
    toeplitz = skew[:, :, :2 * t]
    return jnp.stack([toeplitz[:, :, :t], toeplitz[:, :, t:]], axis=1)


def _mix_kernel(x_ref, a_ref, wpg_ref, bg_ref, wpool_ref, pscale_ref, woa_ref, wob_ref,
                wout_ref, g_ref, b_ref, h_ref, pbuf):
    si = pl.program_id(1)
    t = x_ref.shape[1]

    @pl.when(si == 0)
    def _():
        pbuf[0:HALO, :] = jnp.zeros((HALO, WIDTH_B), F32)

    @pl.when(si > 0)
    def _():
        pbuf[0:HALO, :] = pbuf[t:t + HALO, :]

    pbuf[HALO:HALO + t, :] = jnp.dot(x_ref[0].astype(BF16), wpg_ref[:, :WIDTH_B],
                                     preferred_element_type=F32)
    rows = [slice(c * MIX_CHUNK, (c + 1) * MIX_CHUNK) for c in range(t // MIX_CHUNK)]
    gate_logits = [jnp.dot(x_ref[0, r, :].astype(BF16), wpg_ref[:, WIDTH_B:],
                           preferred_element_type=F32) for r in rows]
    for r, gl in zip(rows, gate_logits):
        n = r.stop - r.start
        pos = si * t + r.start + lax.broadcasted_iota(jnp.int32, (n, 1), 0)
        mixed = []
        for gi, w in enumerate(POOL_WINDOWS):
            cols = slice(gi * POOL_GROUP_DIM, (gi + 1) * POOL_GROUP_DIM)
            own = pbuf[HALO + r.start:HALO + r.stop, cols]
            ws = own
            for dlt in range(1, w):
                ws = ws + pbuf[HALO + r.start - dlt:HALO + r.stop - dlt, cols]
            cnt = jnp.minimum(pos + 1, w).astype(F32)
            pooled = ws / cnt - own
            mixed.append(jnp.dot(pooled.astype(BF16), wpool_ref[gi],
                                 preferred_element_type=F32))
        mixed = jnp.concatenate(mixed, axis=1) * pscale_ref[...]
        branch_b = jnp.dot(mixed.astype(BF16), wob_ref[...], preferred_element_type=F32)
        branch_a = jnp.dot(a_ref[0, r, :], woa_ref[...], preferred_element_type=F32)
        gates = jax.nn.sigmoid(gl + bg_ref[...])
        mixin = gates[:, :D_MODEL] * branch_a + gates[:, D_MODEL:] * branch_b
        mix = jnp.dot(mixin.astype(BF16), wout_ref[...], preferred_element_type=F32)
        h_ref[0, r, :] = _layer_norm(DEEPNORM_ALPHA * x_ref[0, r, :] + mix,
                                     g_ref[...], b_ref[...])


def _mix_ln1(x, a, w_pg, b_gate, w_pool, pool_scale, w_oa, w_ob, w_out, ln_g, ln_b, t=1024):
    b, s, d = x.shape
    const2 = lambda bi, si: (0, 0)
    once = pl.Buffered(1)
    return pl.pallas_call(
        _mix_kernel,
        grid=(b, s // t),
        in_specs=[
            pl.BlockSpec((1, t, d), lambda bi, si: (bi, si, 0)),
            pl.BlockSpec((1, t, WIDTH_A), lambda bi, si: (bi, si, 0)),
            pl.BlockSpec(w_pg.shape, const2, pipeline_mode=once),
            pl.BlockSpec(b_gate.shape, const2),
            pl.BlockSpec(w_pool.shape, lambda bi, si: (0, 0, 0)),
            pl.BlockSpec(pool_scale.shape, const2),
            pl.BlockSpec(w_oa.shape, const2, pipeline_mode=once),
            pl.BlockSpec(w_ob.shape, const2, pipeline_mode=once),
            pl.BlockSpec(w_out.shape, const2, pipeline_mode=once),
            pl.BlockSpec(ln_g.shape, const2),
            pl.BlockSpec(ln_b.shape, const2),
        ],
        out_specs=pl.BlockSpec((1, t, d), lambda bi, si: (bi, si, 0)),
        out_shape=jax.ShapeDtypeStruct((b, s, d), F32),
        scratch_shapes=[pltpu.VMEM((HALO + t, WIDTH_B), F32)],
        compiler_params=pltpu.CompilerParams(
            dimension_semantics=("arbitrary", "arbitrary"),
            vmem_limit_bytes=VMEM_LIMIT),
        name="mix_ln1",
    )(x, a, w_pg, b_gate, w_pool, pool_scale, w_oa, w_ob, w_out, ln_g, ln_b)


FF_CHUNK = 1024
FF_ROWS = 256


def _ffn_kernel(h_ref, w1_ref, b1_ref, w2_ref, b2_ref, g_ref, b_ref, o_ref):
    for r0 in range(0, h_ref.shape[0], FF_ROWS):
        h = h_ref[r0:r0 + FF_ROWS, :]
        hb = h.astype(BF16)
        ff = jnp.zeros(h.shape, F32)
        for c in range(D_FF // FF_CHUNK):
            cs = slice(c * FF_CHUNK, (c + 1) * FF_CHUNK)
            u = jnp.dot(hb, w1_ref[:, cs], preferred_element_type=F32) + b1_ref[:, cs]
            u = jnp.square(jnp.maximum(u, 0.0))
            ff = ff + jnp.dot(u.astype(BF16), w2_ref[cs, :], preferred_element_type=F32)
        ff = ff + b2_ref[...]
        o_ref[r0:r0 + FF_ROWS, :] = _layer_norm(DEEPNORM_ALPHA * h + ff, g_ref[...], b_ref[...])


def _ffn_ln2(h, w1, b1, w2, b2, ln_g, ln_b, t=1024):
    n, d = h.shape
    const2 = lambda i: (0, 0)
    once = pl.Buffered(1)
    return pl.pallas_call(
        _ffn_kernel,
        grid=(n // t,),
        in_specs=[
            pl.BlockSpec((t, d), lambda i: (i, 0)),
            pl.BlockSpec(w1.shape, const2, pipeline_mode=once),
            pl.BlockSpec(b1.shape, const2),
            pl.BlockSpec(w2.shape, const2, pipeline_mode=once),
            pl.BlockSpec(b2.shape, const2),
            pl.BlockSpec(ln_g.shape, const2),
            pl.BlockSpec(ln_b.shape, const2),
        ],
        out_specs=pl.BlockSpec((t, d), lambda i: (i, 0)),
        out_shape=jax.ShapeDtypeStruct((n, d), F32),
        compiler_params=pltpu.CompilerParams(
            dimension_semantics=("arbitrary",),
            vmem_limit_bytes=VMEM_LIMIT),
        name="ffn_ln2",
    )(h, w1, b1, w2, b2, ln_g, ln_b)


def kernel(x, w_in, b_gate, rel_table, w_pool, pool_scale, w_o_attn, w_o_pool, w_out,
           ln1_g, ln1_b, w_ff1, b_ff1, w_ff2, b_ff2, ln2_g, ln2_b):
    b, s, d = x.shape
    nb = s // MOBA_BLOCK
    assert s % MOBA_BLOCK == 0 and nb <= MAX_KEY_BLOCKS
    bias_tiles = _bias_tiles(rel_table)
    h = x
    for l in range(w_in.shape[0]):
        w_q = w_in[l, :, :WIDTH_A] * (HEAD_DIM ** -0.5)
        w_k = w_in[l, :, WIDTH_A:2 * WIDTH_A].astype(BF16)
        w_v = w_in[l, :, 2 * WIDTH_A:3 * WIDTH_A]
        w_qvt = jnp.concatenate([w_q, w_v], axis=1).T.astype(BF16)
        w_pg = w_in[l, :, 3 * WIDTH_A:].astype(BF16)
        q_t, k_aug, v_t = _qkv_proj(h, w_k, w_qvt)
        a = _moba_attention(q_t, k_aug, v_t, bias_tiles)
        h = _mix_ln1(h, a, w_pg, b_gate[l][None, :], w_pool[l].astype(BF16),
                     pool_scale[l][None, :], w_o_attn[l].astype(BF16),
                     w_o_pool[l].astype(BF16), w_out[l].astype(BF16),
                     ln1_g[l][None, :], ln1_b[l][None, :])
        h = _ffn_ln2(h.reshape(b * s, d), w_ff1[l].astype(BF16), b_ff1[l][None, :],
                     w_ff2[l].astype(BF16), b_ff2[l][None, :],
                     ln2_g[l][None, :], ln2_b[l][None, :]).reshape(b, s, d)
    return h
```

```python
import math

import jax
import jax.numpy as jnp
from jax import lax
from jax.experimental import pallas as pl
from jax.experimental.pallas import tpu as pltpu

D_MODEL = 1024
N_HEADS = 8
HEAD_DIM = 64
WIDTH_A = N_HEADS * HEAD_DIM
MOBA_BLOCK = 256
MOBA_TOPK = 3
N_POOL_GROUPS = 4
POOL_WINDOWS = (2, 4, 8, 16)
WIDTH_B = 512
POOL_GROUP_DIM = WIDTH_B // N_POOL_GROUPS
D_FF = 4 * D_MODEL
REL_BUCKETS = 32
REL_MAX_DIST = 128
DEEPNORM_ALPHA = 2.0 ** 0.25
LN_EPS = 1e-5
NEG_INF = -1e30

LANES = 128
MAX_KEY_BLOCKS = 32
HEADS_PER_STEP = 8
V_ROWS = HEAD_DIM + 16
LOG2E = math.log2(math.e)
HALO = 16
MIX_CHUNK = 256
VMEM_LIMIT = 48 * 1024 * 1024

BF16 = jnp.bfloat16
F32 = jnp.float32
NT_DIMS = (((1,), (1,)), ((), ()))


def _layer_norm(y, g, b):
    mu = jnp.mean(y, axis=-1, keepdims=True)
    d = y - mu
    var = jnp.mean(d * d, axis=-1, keepdims=True)
    return d * lax.rsqrt(var + LN_EPS) * g + b


def _qkv_kernel(x_ref, wk_ref, wqvt_ref, q_ref, k_ref, v_ref, kmean_sc):
    si = pl.program_id(1)

    @pl.when(si == 0)
    def _():
        kmean_sc[...] = jnp.zeros(kmean_sc.shape, F32)

    xb = x_ref[0].astype(BF16)
    t = xb.shape[0]
    blocks_per_tile = t // MOBA_BLOCK
    acc_k =jnp.dot(xb, wk_ref[...], preferred_element_type=F32)
    acc_q = lax.dot_general(wqvt_ref[0:WIDTH_A, :], xb, NT_DIMS,
                            preferred_element_type=F32)
    lane = lax.broadcasted_iota(jnp.int32, (t, LANES), 1)
    row = lax.broadcasted_iota(jnp.int32, (t, LANES), 0)
    lo = lane < HEAD_DIM
    onehot = jnp.where(lane == HEAD_DIM + si * blocks_per_tile + row // MOBA_BLOCK, 1.0, 0.0)
    for hp in range(N_HEADS // 2):
        kp = acc_k[:, hp * LANES:(hp + 1) * LANES] * LOG2E
        kr = pltpu.roll(kp, HEAD_DIM, 1)
        k_ref[0, 2 * hp] = jnp.where(lo, kp, onehot).astype(BF16)
        k_ref[0, 2 * hp + 1] = jnp.where(lo, kr, onehot).astype(BF16)

    blk_row =lax.broadcasted_iota(jnp.int32, (MAX_KEY_BLOCKS, LANES), 0)
    lo_row = lax.broadcasted_iota(jnp.int32, (MAX_KEY_BLOCKS, LANES), 1) < HEAD_DIM
    for c in range(blocks_per_tile):
        mean_c = jnp.mean(acc_k[c * MOBA_BLOCK:(c + 1) * MOBA_BLOCK], axis=0, keepdims=True)
        for hp in range(N_HEADS // 2):
            pair = mean_c[:, hp * LANES:(hp + 1) * LANES]
            for h, m in ((2 * hp, pair), (2 * hp + 1, pltpu.roll(pair, HEAD_DIM, 1))):
                new_row = jnp.where(lo_row, jnp.broadcast_to(m, (MAX_KEY_BLOCKS, LANES)), 0.0)
                kmean_sc[h] = jnp.where(blk_row == si * blocks_per_tile + c, new_row, kmean_sc[h])

    sel_row = lax.broadcasted_iota(jnp.int32, (MAX_KEY_BLOCKS, t), 0)
    own_blk = si * blocks_per_tile + lax.broadcasted_iota(
        jnp.int32, (MAX_KEY_BLOCKS, t), 1) // MOBA_BLOCK
    past = sel_row < own_blk
    zeros64 = jnp.zeros((LANES - HEAD_DIM, t), BF16)
    zeros32 = jnp.zeros((LANES - HEAD_DIM - MAX_KEY_BLOCKS, t), BF16)
    pad1 = jnp.ones((V_ROWS - HEAD_DIM, t), BF16)
    for h in range(N_HEADS):
        qt = acc_q[h * HEAD_DIM:(h + 1) * HEAD_DIM].astype(BF16)
        gate = jnp.dot(kmean_sc[h].astype(BF16), jnp.concatenate([qt, zeros64], axis=0),
                       preferred_element_type=F32)
        g = jnp.where(past, gate, NEG_INF)
        sel = jnp.zeros(g.shape, jnp.bool_)
        for _ in range(MOBA_TOPK):
            mx = jnp.max(g, axis=0, keepdims=True)
            first = jnp.min(jnp.where(g == mx, sel_row, MAX_KEY_BLOCKS), axis=0, keepdims=True)
            pick = sel_row == first
            sel = sel | pick
            g = jnp.where(pick, -jnp.inf, g)
        sel = (sel & past) | (sel_row == own_blk)
        q_ref[0, h, 0:HEAD_DIM, :] = qt
        q_ref[0, h, HEAD_DIM:HEAD_DIM + MAX_KEY_BLOCKS, :] = jnp.where(sel, 0.0, NEG_INF).astype(BF16)
        q_ref[0, h, HEAD_DIM + MAX_KEY_BLOCKS:LANES, :] = zeros32

    acc_v = lax.dot_general(wqvt_ref[WIDTH_A:2 * WIDTH_A, :], xb, NT_DIMS,
                            preferred_element_type=F32)
    for h in range(N_HEADS):
        v_ref[0, h, 0:HEAD_DIM, :] = acc_v[h * HEAD_DIM:(h + 1) * HEAD_DIM].astype(BF16)
        v_ref[0, h, HEAD_DIM:V_ROWS, :] = pad1


def _qkv_proj(x, w_k, w_qvt, t=2 * MOBA_BLOCK):
    b, s, d = x.shape
    return pl.pallas_call(
        _qkv_kernel,
        grid=(b, s // t),
        in_specs=[
            pl.BlockSpec((1, t, d), lambda bi, si: (bi, si, 0)),
            pl.BlockSpec(w_k.shape, lambda bi, si: (0, 0)),
            pl.BlockSpec(w_qvt.shape, lambda bi, si: (0, 0)),
        ],
        out_specs=[
            pl.BlockSpec((1, N_HEADS, LANES, t), lambda bi, si: (bi, 0, 0, si)),
            pl.BlockSpec((1, N_HEADS, t, LANES), lambda bi, si: (bi, 0, si, 0)),
            pl.BlockSpec((1, N_HEADS, V_ROWS, t), lambda bi, si: (bi, 0, 0, si)),
        ],
        out_shape=[
            jax.ShapeDtypeStruct((b, N_HEADS, LANES, s), BF16),
            jax.ShapeDtypeStruct((b, N_HEADS, s, LANES), BF16),
            jax.ShapeDtypeStruct((b, N_HEADS, V_ROWS, s), BF16),
        ],
        scratch_shapes=[pltpu.VMEM((N_HEADS, MAX_KEY_BLOCKS, LANES), F32)],
        compiler_params=pltpu.CompilerParams(
            dimension_semantics=("arbitrary", "arbitrary"),
            vmem_limit_bytes=VMEM_LIMIT),
        name="qkv_proj",
    )(x, w_k, w_qvt)


def _attn_kernel(q_ref, k_ref, v_ref, bias_ref, o_ref, k_all, v_all, s_a, s_b, acc_sc, m_sc,
                 max_sc):
    i = pl.program_id(2)
    t = MOBA_BLOCK
    heads = range(HEADS_PER_STEP)
    own = pl.ds(pl.multiple_of(i * t, t), t)
    for h in heads:
        k_all[h, own, :] = k_ref[0, h]
        v_all[h, :, own] = v_ref[0, h]

    def key_block(h, j):
        return k_all[h, pl.ds(pl.multiple_of(j * t, t), t), :]

    def head_scores_into(s_dst, h, j):
        s = jnp.dot(key_block(h, j), q_ref[0, h], preferred_element_type=F32)
        s_dst[h] = s
        return jnp.max(s, axis=0, keepdims=True)

    def weighted_values(h, j, p):
        vv = v_all[h, :, pl.ds(pl.multiple_of(j * t, t), t)]
        return jnp.dot(vv, p.astype(BF16), preferred_element_type=F32)

    def step(j, ms, s_cur, max_cur, s_nxt, bias_idx=None):
        new_ms, max_nxt = [], []
        for h in heads:
            if s_nxt is not None:
                max_nxt.append(head_scores_into(s_nxt, h, j + 1))
            s = s_cur[h]
            if bias_idx is not None:
                s = s + bias_ref[h, bias_idx]
                m_new = jnp.maximum(ms[h], jnp.max(s, axis=0, keepdims=True))
            else:
                m_new = jnp.maximum(ms[h], max_cur[h])
            alpha = jnp.exp2(ms[h] - m_new)
            acc_sc[h] = alpha * acc_sc[h] + weighted_values(h, j, jnp.exp2(s - m_new))
            new_ms.append(m_new)
        return new_ms, max_nxt

    ms = []
    for h in heads:
        s = jnp.dot(k_ref[0, h], q_ref[0, h], preferred_element_type=F32) + bias_ref[h, 0]
        s_b[h] = s
        ms.append(jnp.max(s, axis=0, keepdims=True))

    max_a = []
    for h in heads:
        max_a.append(head_scores_into(s_a, h, 0))
        acc_sc[h] = jnp.dot(v_ref[0, h], jnp.exp2(s_b[h] - ms[h]).astype(BF16),
                            preferred_element_type=F32)

    n_far = jnp.maximum(i - 1, 0)

    def two_steps(j, ms, max_a):
        ms, max_b = step(j, list(ms), s_a, list(max_a), s_b)
        return step(j + 1, ms, s_b, max_b, s_a)

    def four_steps(jj, carry):
        ms, max_a = two_steps(4 * jj, *carry)
        ms, max_a = two_steps(4 * jj + 2, ms, max_a)
        return tuple(ms), tuple(max_a)

    ms, max_a = lax.fori_loop(0, n_far // 4, four_steps, (tuple(ms), tuple(max_a)))
    for h in heads:
        m_sc[h] = ms[h]
        max_sc[h] = max_a[h]
    left = n_far % 4
    j_left = n_far - left

    @pl.when(left >= 2)
    def _():
        ms, max_a = two_steps(j_left, [m_sc[h] for h in heads], [max_sc[h] for h in heads])
        for h in heads:
            m_sc[h] = ms[h]
            max_sc[h] = max_a[h]

    @pl.when((i >= 1) & (left % 2 == 0))
    def _():
        step(i - 1, [m_sc[h] for h in heads], s_a, None, None, bias_idx=1)

    @pl.when(left % 2 == 1)
    def _():
        ms1, _ = step(i - 2, [m_sc[h] for h in heads], s_a, [max_sc[h] for h in heads], s_b)
        step(i - 1, ms1, s_b, None, None, bias_idx=1)

    for hp in range(HEADS_PER_STEP // 2):
        outs = []
        for h in (2 * hp, 2 * hp + 1):
            acc = acc_sc[h]
            outs.append(acc[0:HEAD_DIM] / acc[HEAD_DIM:HEAD_DIM + 1])
        o_ref[0, :, hp * LANES:(hp + 1) * LANES] = jnp.concatenate(outs, axis=0).T.astype(o_ref.dtype)


def _moba_attention(q_t, k_aug, v_t, bias_tiles):
    b, h, s, _ = k_aug.shape
    t = MOBA_BLOCK
    g = HEADS_PER_STEP
    once = pl.Buffered(1)
    return pl.pallas_call(
        _attn_kernel,
        grid=(b, h // g, s // t),
        in_specs=[
            pl.BlockSpec((1, g, LANES, t), lambda bi, hg, qi: (bi, hg, 0, qi)),
            pl.BlockSpec((1, g, t, LANES), lambda bi, hg, qi: (bi, hg, qi, 0)),
            pl.BlockSpec((1, g, V_ROWS, t), lambda bi, hg, qi: (bi, hg, 0, qi)),
            pl.BlockSpec((g, 2, t, t), lambda bi, hg, qi: (hg, 0, 0, 0), pipeline_mode=once),
        ],
        out_specs=pl.BlockSpec((1, t, g * HEAD_DIM), lambda bi, hg, qi: (bi, qi, hg)),
        out_shape=jax.ShapeDtypeStruct((b, s, WIDTH_A), BF16),
        scratch_shapes=[
            pltpu.VMEM((g, s, LANES), BF16),
            pltpu.VMEM((g, V_ROWS, s), BF16),
            pltpu.VMEM((g, t, t), F32),
            pltpu.VMEM((g, t, t), F32),
            pltpu.VMEM((g, V_ROWS, t), F32),
            pltpu.VMEM((g, 1, t), F32),
            pltpu.VMEM((g, 1, t), F32),
        ],
        compiler_params=pltpu.CompilerParams(
            dimension_semantics=("arbitrary", "arbitrary", "arbitrary"),
            vmem_limit_bytes=VMEM_LIMIT),
        name="moba_attn",
    )(q_t, k_aug, v_t, bias_tiles)


def _rel_bucket(dist):
    n = jnp.maximum(dist, 0)
    max_exact = REL_BUCKETS // 2
    nf = jnp.maximum(n, 1).astype(F32)
    large = max_exact + (jnp.log(nf / max_exact) / math.log(REL_MAX_DIST / max_exact)
                         * (REL_BUCKETS - max_exact)).astype(jnp.int32)
    large = jnp.minimum(large, REL_BUCKETS - 1)
    return jnp.where(n < max_exact, n, large)


def _bias_tiles(rel_table):
    t = MOBA_BLOCK
    period = 3 * t - 1
    dist = jnp.arange(period)
    dist = jnp.where(dist < 2 * t, dist, dist - period)
    onehot = _rel_bucket(dist)[None, :] == jnp.arange(REL_BUCKETS)[:, None]
    far = rel_table[_rel_bucket(jnp.array(2 * t, jnp.int32))]
    shifted = (rel_table - far[None, :]).T
    per_dist = jnp.sum(jnp.where(onehot[None], shifted[:, :, None], 0.0), axis=1)
    per_dist = jnp.where(dist >= 0, per_dist * LOG2E, NEG_INF).astype(F32)
    skew = jnp.tile(per_dist, (1, t))[:, :t * (period - 1)].reshape(-1, t, period - 1)
    toeplitz = skew[:, :, :2 * t]
    return jnp.stack([toeplitz[:, :, :t], toeplitz[:, :, t:]], axis=1)


def _mix_kernel(x_ref, a_ref, wpg_ref, bg_ref, wpool_ref, pscale_ref, woa_ref, wob_ref,
                wout_ref, g_ref, b_ref, h_ref, pbuf):
    si = pl.program_id(1)
    t = x_ref.shape[1]

    @pl.when(si == 0)
    def _():
        pbuf[0:HALO, :] = jnp.zeros((HALO, WIDTH_B), F32)

    @pl.when(si > 0)
    def _():
        pbuf[0:HALO, :] = pbuf[t:t + HALO, :]

    pbuf[HALO:HALO + t, :] = jnp.dot(x_ref[0].astype(BF16), wpg_ref[:, :WIDTH_B],
                                     preferred_element_type=F32)
    rows = [slice(c * MIX_CHUNK, (c + 1) * MIX_CHUNK) for c in range(t // MIX_CHUNK)]
    gate_logits = [jnp.dot(x_ref[0, r, :].astype(BF16), wpg_ref[:, WIDTH_B:],
                           preferred_element_type=F32) for r in rows]
    for r, gl in zip(rows, gate_logits):
        n = r.stop - r.start
        pos = si * t + r.start + lax.broadcasted_iota(jnp.int32, (n, 1), 0)
        mixed = []
        for gi, w in enumerate(POOL_WINDOWS):
            cols = slice(gi * POOL_GROUP_DIM, (gi + 1) * POOL_GROUP_DIM)
            own = pbuf[HALO + r.start:HALO + r.stop, cols]
            ws = own
            for dlt in range(1, w):
                ws = ws + pbuf[HALO + r.start - dlt:HALO + r.stop - dlt, cols]
            cnt = jnp.minimum(pos + 1, w).astype(F32)
            pooled = ws / cnt - own
            mixed.append(jnp.dot(pooled.astype(BF16), wpool_ref[gi],
                                 preferred_element_type=F32))
        mixed = jnp.concatenate(mixed, axis=1) * pscale_ref[...]
        branch_b = jnp.dot(mixed.astype(BF16), wob_ref[...], preferred_element_type=F32)
        branch_a = jnp.dot(a_ref[0, r, :], woa_ref[...], preferred_element_type=F32)
        gates = jax.nn.sigmoid(gl + bg_ref[...])
        mixin = gates[:, :D_MODEL] * branch_a + gates[:, D_MODEL:] * branch_b
        mix = jnp.dot(mixin.astype(BF16), wout_ref[...], preferred_element_type=F32)
        h_ref[0, r, :] = _layer_norm(DEEPNORM_ALPHA * x_ref[0, r, :] + mix,
                                     g_ref[...], b_ref[...])


def _mix_ln1(x, a, w_pg, b_gate, w_pool, pool_scale, w_oa, w_ob, w_out, ln_g, ln_b, t=1024):
    b, s, d = x.shape
    const2 = lambda bi, si: (0, 0)
    once = pl.Buffered(1)
    return pl.pallas_call(
        _mix_kernel,
        grid=(b, s // t),
        in_specs=[
            pl.BlockSpec((1, t, d), lambda bi, si: (bi, si, 0)),
            pl.BlockSpec((1, t, WIDTH_A), lambda bi, si: (bi, si, 0)),
            pl.BlockSpec(w_pg.shape, const2, pipeline_mode=once),
            pl.BlockSpec(b_gate.shape, const2),
            pl.BlockSpec(w_pool.shape, lambda bi, si: (0, 0, 0)),
            pl.BlockSpec(pool_scale.shape, const2),
            pl.BlockSpec(w_oa.shape, const2, pipeline_mode=once),
            pl.BlockSpec(w_ob.shape, const2, pipeline_mode=once),
            pl.BlockSpec(w_out.shape, const2, pipeline_mode=once),
            pl.BlockSpec(ln_g.shape, const2),
            pl.BlockSpec(ln_b.shape, const2),
        ],
        out_specs=pl.BlockSpec((1, t, d), lambda bi, si: (bi, si, 0)),
        out_shape=jax.ShapeDtypeStruct((b, s, d), F32),
        scratch_shapes=[pltpu.VMEM((HALO + t, WIDTH_B), F32)],
        compiler_params=pltpu.CompilerParams(
            dimension_semantics=("arbitrary", "arbitrary"),
            vmem_limit_bytes=VMEM_LIMIT),
        name="mix_ln1",
    )(x, a, w_pg, b_gate, w_pool, pool_scale, w_oa, w_ob, w_out, ln_g, ln_b)


FF_CHUNK = 1024
FF_ROWS = 256


def _ffn_kernel(h_ref, w1_ref, b1_ref, w2_ref, b2_ref, g_ref, b_ref, o_ref):
    for r0 in range(0, h_ref.shape[0], FF_ROWS):
        h = h_ref[r0:r0 + FF_ROWS, :]
        hb = h.astype(BF16)
        ff = jnp.zeros(h.shape, F32)
        for c in range(D_FF // FF_CHUNK):
            cs = slice(c * FF_CHUNK, (c + 1) * FF_CHUNK)
            u = jnp.dot(hb, w1_ref[:, cs], preferred_element_type=F32) + b1_ref[:, cs]
            u = jnp.square(jnp.maximum(u, 0.0))
            ff = ff + jnp.dot(u.astype(BF16), w2_ref[cs, :], preferred_element_type=F32)
        ff = ff + b2_ref[...]
        o_ref[r0:r0 + FF_ROWS, :] = _layer_norm(DEEPNORM_ALPHA * h + ff, g_ref[...], b_ref[...])


def _ffn_ln2(h, w1, b1, w2, b2, ln_g, ln_b, t=1024):
    n, d = h.shape
    const2 = lambda i: (0, 0)
    once = pl.Buffered(1)
    return pl.pallas_call(
        _ffn_kernel,
        grid=(n // t,),
        in_specs=[
            pl.BlockSpec((t, d), lambda i: (i, 0)),
            pl.BlockSpec(w1.shape, const2, pipeline_mode=once),
            pl.BlockSpec(b1.shape, const2),
            pl.BlockSpec(w2.shape, const2, pipeline_mode=once),
            pl.BlockSpec(b2.shape, const2),
            pl.BlockSpec(ln_g.shape, const2),
            pl.BlockSpec(ln_b.shape, const2),
        ],
        out_specs=pl.BlockSpec((t, d), lambda i: (i, 0)),
        out_shape=jax.ShapeDtypeStruct((n, d), F32),
        compiler_params=pltpu.CompilerParams(
            dimension_semantics=("arbitrary",),
            vmem_limit_bytes=VMEM_LIMIT),
        name="ffn_ln2",
    )(h, w1, b1, w2, b2, ln_g, ln_b)


def kernel(x, w_in, b_gate, rel_table, w_pool, pool_scale, w_o_attn, w_o_pool, w_out,
           ln1_g, ln1_b, w_ff1, b_ff1, w_ff2, b_ff2, ln2_g, ln2_b):
    b, s, d = x.shape
    nb = s // MOBA_BLOCK
    assert s % MOBA_BLOCK == 0 and nb <= MAX_KEY_BLOCKS
    bias_tiles = _bias_tiles(rel_table)
    h = x
    for l in range(w_in.shape[0]):
        w_q = w_in[l, :, :WIDTH_A] * (HEAD_DIM ** -0.5)
        w_k = w_in[l, :, WIDTH_A:2 * WIDTH_A].astype(BF16)
        w_v = w_in[l, :, 2 * WIDTH_A:3 * WIDTH_A]
        w_qvt = jnp.concatenate([w_q, w_v], axis=1).T.astype(BF16)
        w_pg = w_in[l, :, 3 * WIDTH_A:].astype(BF16)
        q_t, k_aug, v_t = _qkv_proj(h, w_k, w_qvt)
        a = _moba_attention(q_t, k_aug, v_t, bias_tiles)
        h = _mix_ln1(h, a, w_pg, b_gate[l][None, :], w_pool[l].astype(BF16),
                     pool_scale[l][None, :], w_o_attn[l].astype(BF16),
                     w_o_pool[l].astype(BF16), w_out[l].astype(BF16),
                     ln1_g[l][None, :], ln1_b[l][None, :])
        h = _ffn_ln2(h.reshape(b * s, d), w_ff1[l].astype(BF16), b_ff1[l][None, :],
                     w_ff2[l].astype(BF16), b_ff2[l][None, :],
                     ln2_g[l][None, :], ln2_b[l][None, :]).reshape(b, s, d)
    return h
```

```python
import math

import jax
import jax.numpy as jnp
from jax import lax
from jax.experimental import pallas as pl
from jax.experimental.pallas import tpu as pltpu

D_MODEL = 1024
N_HEADS = 8
HEAD_DIM = 64
WIDTH_A = N_HEADS * HEAD_DIM
MOBA_BLOCK = 256
MOBA_TOPK = 3
N_POOL_GROUPS = 4
POOL_WINDOWS = (2, 4, 8, 16)
WIDTH_B = 512
POOL_GROUP_DIM = WIDTH_B // N_POOL_GROUPS
D_FF = 4 * D_MODEL
REL_BUCKETS = 32
REL_MAX_DIST = 128
DEEPNORM_ALPHA = 2.0 ** 0.25
LN_EPS = 1e-5
NEG_INF = -1e30

LANES = 128
MAX_KEY_BLOCKS = 32
HEADS_PER_STEP = 8
V_ROWS = HEAD_DIM + 16
LOG2E = math.log2(math.e)
HALO = 16
MIX_CHUNK = 256
VMEM_LIMIT = 48 * 1024 * 1024

BF16 = jnp.bfloat16
F32 = jnp.float32
NT_DIMS = (((1,), (1,)), ((), ()))


def _layer_norm(y, g, b):
    mu = jnp.mean(y, axis=-1, keepdims=True)
    d = y - mu
    var = jnp.mean(d * d, axis=-1, keepdims=True)
    return d * lax.rsqrt(var + LN_EPS) * g + b


def _qkv_kernel(x_ref, wk_ref, wqvt_ref, q_ref, k_ref, v_ref, kmean_sc):
    si = pl.program_id(1)

    @pl.when(si == 0)
    def _():
        kmean_sc[...] = jnp.zeros(kmean_sc.shape, F32)

    xb = x_ref[0].astype(BF16)
    t = xb.shape[0]
    blocks_per_tile = t // MOBA_BLOCK
    acc_k =jnp.dot(xb, wk_ref[...], preferred_element_type=F32)
    acc_q = lax.dot_general(wqvt_ref[0:WIDTH_A, :], xb, NT_DIMS,
                            preferred_element_type=F32)
    lane = lax.broadcasted_iota(jnp.int32, (t, LANES), 1)
    row = lax.broadcasted_iota(jnp.int32, (t, LANES), 0)
    lo = lane < HEAD_DIM
    onehot = jnp.where(lane == HEAD_DIM + si * blocks_per_tile + row // MOBA_BLOCK, 1.0, 0.0)
    for hp in range(N_HEADS // 2):
        kp = acc_k[:, hp * LANES:(hp + 1) * LANES] * LOG2E
        kr = pltpu.roll(kp, HEAD_DIM, 1)
        k_ref[0, 2 * hp] = jnp.where(lo, kp, onehot).astype(BF16)
        k_ref[0, 2 * hp + 1] = jnp.where(lo, kr, onehot).astype(BF16)

    blk_row =lax.broadcasted_iota(jnp.int32, (MAX_KEY_BLOCKS, LANES), 0)
    lo_row = lax.broadcasted_iota(jnp.int32, (MAX_KEY_BLOCKS, LANES), 1) < HEAD_DIM
    for c in range(blocks_per_tile):
        mean_c = jnp.mean(acc_k[c * MOBA_BLOCK:(c + 1) * MOBA_BLOCK], axis=0, keepdims=True)
        for hp in range(N_HEADS // 2):
            pair = mean_c[:, hp * LANES:(hp + 1) * LANES]
            for h, m in ((2 * hp, pair), (2 * hp + 1, pltpu.roll(pair, HEAD_DIM, 1))):
                new_row = jnp.where(lo_row, jnp.broadcast_to(m, (MAX_KEY_BLOCKS, LANES)), 0.0)
                kmean_sc[h] = jnp.where(blk_row == si * blocks_per_tile + c, new_row, kmean_sc[h])

    sel_row = lax.broadcasted_iota(jnp.int32, (MAX_KEY_BLOCKS, t), 0)
    own_blk = si * blocks_per_tile + lax.broadcasted_iota(
        jnp.int32, (MAX_KEY_BLOCKS, t), 1) // MOBA_BLOCK
    past = sel_row < own_blk
    zeros64 = jnp.zeros((LANES - HEAD_DIM, t), BF16)
    zeros32 = jnp.zeros((LANES - HEAD_DIM - MAX_KEY_BLOCKS, t), BF16)
    pad1 = jnp.ones((V_ROWS - HEAD_DIM, t), BF16)
    for h in range(N_HEADS):
        qt = acc_q[h * HEAD_DIM:(h + 1) * HEAD_DIM].astype(BF16)
        gate = jnp.dot(kmean_sc[h].astype(BF16), jnp.concatenate([qt, zeros64], axis=0),
                       preferred_element_type=F32)
        g = jnp.where(past, gate, NEG_INF)
        sel = jnp.zeros(g.shape, jnp.bool_)
        for _ in range(MOBA_TOPK):
            mx = jnp.max(g, axis=0, keepdims=True)
            first = jnp.min(jnp.where(g == mx, sel_row, MAX_KEY_BLOCKS), axis=0, keepdims=True)
            pick = sel_row == first
            sel = sel | pick
            g = jnp.where(pick, -jnp.inf, g)
        sel = (sel & past) | (sel_row == own_blk)
        q_ref[0, h, 0:HEAD_DIM, :] = qt
        q_ref[0, h, HEAD_DIM:HEAD_DIM + MAX_KEY_BLOCKS, :] = jnp.where(sel, 0.0, NEG_INF).astype(BF16)
        q_ref[0, h, HEAD_DIM + MAX_KEY_BLOCKS:LANES, :] = zeros32

    acc_v = lax.dot_general(wqvt_ref[WIDTH_A:2 * WIDTH_A, :], xb, NT_DIMS,
                            preferred_element_type=F32)
    for h in range(N_HEADS):
        v_ref[0, h, 0:HEAD_DIM, :] = acc_v[h * HEAD_DIM:(h + 1) * HEAD_DIM].astype(BF16)
        v_ref[0, h, HEAD_DIM:V_ROWS, :] = pad1


def _qkv_proj(x, w_k, w_qvt, t=2 * MOBA_BLOCK):
    b, s, d = x.shape
    return pl.pallas_call(
        _qkv_kernel,
        grid=(b, s // t),
        in_specs=[
            pl.BlockSpec((1, t, d), lambda bi, si: (bi, si, 0)),
            pl.BlockSpec(w_k.shape, lambda bi, si: (0, 0)),
            pl.BlockSpec(w_qvt.shape, lambda bi, si: (0, 0)),
        ],
        out_specs=[
            pl.BlockSpec((1, N_HEADS, LANES, t), lambda bi, si: (bi, 0, 0, si)),
            pl.BlockSpec((1, N_HEADS, t, LANES), lambda bi, si: (bi, 0, si, 0)),
            pl.BlockSpec((1, N_HEADS, V_ROWS, t), lambda bi, si: (bi, 0, 0, si)),
        ],
        out_shape=[
            jax.ShapeDtypeStruct((b, N_HEADS, LANES, s), BF16),
            jax.ShapeDtypeStruct((b, N_HEADS, s, LANES), BF16),
            jax.ShapeDtypeStruct((b, N_HEADS, V_ROWS, s), BF16),
        ],
        scratch_shapes=[pltpu.VMEM((N_HEADS, MAX_KEY_BLOCKS, LANES), F32)],
        compiler_params=pltpu.CompilerParams(
            dimension_semantics=("arbitrary", "arbitrary"),
            vmem_limit_bytes=VMEM_LIMIT),
        name="qkv_proj",
    )(x, w_k, w_qvt)


def _attn_kernel(q_ref, k_ref, v_ref, bias_ref, o_ref, k_all, v_all, s_a, s_b, acc_sc, m_sc,
                 max_sc):
    i = pl.program_id(2)
    t = MOBA_BLOCK
    heads = range(HEADS_PER_STEP)
    own = pl.ds(pl.multiple_of(i * t, t), t)
    for h in heads:
        k_all[h, own, :] = k_ref[0, h]
        v_all[h, :, own] = v_ref[0, h]

    def key_block(h, j):
        return k_all[h, pl.ds(pl.multiple_of(j * t, t), t), :]

    def head_scores_into(s_dst, h, j):
        s = jnp.dot(key_block(h, j), q_ref[0, h], preferred_element_type=F32)
        s_dst[h] = s
        return jnp.max(s, axis=0, keepdims=True)

    def weighted_values(h, j, p):
        vv = v_all[h, :, pl.ds(pl.multiple_of(j * t, t), t)]
        return jnp.dot(vv, p.astype(BF16), preferred_element_type=F32)

    def step(j, ms, s_cur, max_cur, s_nxt, bias_idx=None):
        new_ms, max_nxt = [], []
        for h in heads:
            if s_nxt is not None:
                max_nxt.append(head_scores_into(s_nxt, h, j + 1))
            s = s_cur[h]
            if bias_idx is not None:
                s = s + bias_ref[h, bias_idx]
                m_new = jnp.maximum(ms[h], jnp.max(s, axis=0, keepdims=True))
            else:
                m_new = jnp.maximum(ms[h], max_cur[h])
            alpha = jnp.exp2(ms[h] - m_new)
            acc_sc[h] = alpha * acc_sc[h] + weighted_values(h, j, jnp.exp2(s - m_new))
            new_ms.append(m_new)
        return new_ms, max_nxt

    ms = []
    for h in heads:
        s = jnp.dot(k_ref[0, h], q_ref[0, h], preferred_element_type=F32) + bias_ref[h, 0]
        s_b[h] = s
        ms.append(jnp.max(s, axis=0, keepdims=True))

    max_a = []
    for h in heads:
        max_a.append(head_scores_into(s_a, h, 0))
        acc_sc[h] = jnp.dot(v_ref[0, h], jnp.exp2(s_b[h] - ms[h]).astype(BF16),
                            preferred_element_type=F32)

    n_far = jnp.maximum(i - 1, 0)

    def two_steps(j, ms, max_a):
        ms, max_b = step(j, list(ms), s_a, list(max_a), s_b)
        return step(j + 1, ms, s_b, max_b, s_a)

    def four_steps(jj, carry):
        ms, max_a = two_steps(4 * jj, *carry)
        ms, max_a = two_steps(4 * jj + 2, ms, max_a)
        return tuple(ms), tuple(max_a)

    ms, max_a = lax.fori_loop(0, n_far // 4, four_steps, (tuple(ms), tuple(max_a)))
    for h in heads:
        m_sc[h] = ms[h]
        max_sc[h] = max_a[h]
    left = n_far % 4
    j_left = n_far - left

    @pl.when(left >= 2)
    def _():
        ms, max_a = two_steps(j_left, [m_sc[h] for h in heads], [max_sc[h] for h in heads])
        for h in heads:
            m_sc[h] = ms[h]
            max_sc[h] = max_a[h]

    @pl.when((i >= 1) & (left % 2 == 0))
    def _():
        step(i - 1, [m_sc[h] for h in heads], s_a, None, None, bias_idx=1)

    @pl.when(left % 2 == 1)
    def _():
        ms1, _ = step(i - 2, [m_sc[h] for h in heads], s_a, [max_sc[h] for h in heads], s_b)
        step(i - 1, ms1, s_b, None, None, bias_idx=1)

    for hp in range(HEADS_PER_STEP // 2):
        outs = []
        for h in (2 * hp, 2 * hp + 1):
            acc = acc_sc[h]
            outs.append(acc[0:HEAD_DIM] / acc[HEAD_DIM:HEAD_DIM + 1])
        o_ref[0, :, hp * LANES:(hp + 1) * LANES] = jnp.concatenate(outs, axis=0).T.astype(o_ref.dtype)


def _moba_attention(q_t, k_aug, v_t, bias_tiles):
    b, h, s, _ = k_aug.shape
    t = MOBA_BLOCK
    g = HEADS_PER_STEP
    once = pl.Buffered(1)
    return pl.pallas_call(
        _attn_kernel,
        grid=(b, h // g, s // t),
        in_specs=[
            pl.BlockSpec((1, g, LANES, t), lambda bi, hg, qi: (bi, hg, 0, qi)),
            pl.BlockSpec((1, g, t, LANES), lambda bi, hg, qi: (bi, hg, qi, 0)),
            pl.BlockSpec((1, g, V_ROWS, t), lambda bi, hg, qi: (bi, hg, 0, qi)),
            pl.BlockSpec((g, 2, t, t), lambda bi, hg, qi: (hg, 0, 0, 0), pipeline_mode=once),
        ],
        out_specs=pl.BlockSpec((1, t, g * HEAD_DIM), lambda bi, hg, qi: (bi, qi, hg)),
        out_shape=jax.ShapeDtypeStruct((b, s, WIDTH_A), BF16),
        scratch_shapes=[
            pltpu.VMEM((g, s, LANES), BF16),
            pltpu.VMEM((g, V_ROWS, s), BF16),
            pltpu.VMEM((g, t, t), F32),
            pltpu.VMEM((g, t, t), F32),
            pltpu.VMEM((g, V_ROWS, t), F32),
            pltpu.VMEM((g, 1, t), F32),
            pltpu.VMEM((g, 1, t), F32),
        ],
        compiler_params=pltpu.CompilerParams(
            dimension_semantics=("arbitrary", "arbitrary", "arbitrary"),
            vmem_limit_bytes=VMEM_LIMIT),
        name="moba_attn",
    )(q_t, k_aug, v_t, bias_tiles)


def _rel_bucket(dist):
    n = jnp.maximum(dist, 0)
    max_exact = REL_BUCKETS // 2
    nf = jnp.maximum(n, 1).astype(F32)
    large = max_exact + (jnp.log(nf / max_exact) / math.log(REL_MAX_DIST / max_exact)
                         * (REL_BUCKETS - max_exact)).astype(jnp.int32)
    large = jnp.minimum(large, REL_BUCKETS - 1)
    return jnp.where(n < max_exact, n, large)


def _bias_tiles(rel_table):
    t = MOBA_BLOCK
    period = 3 * t + 1
    dist = jnp.arange(period)
    dist = jnp.where(dist < 2 * t, dist, dist - period)
    onehot = _rel_bucket(dist)[None, :] == jnp.arange(REL_BUCKETS)[:, None]
    far = rel_table[_rel_bucket(jnp.array(2 * t, jnp.int32))]
    shifted = (rel_table - far[None, :]).T
    per_dist = jnp.sum(jnp.where(onehot[None], shifted[:, :, None], 0.0), axis=1)
    per_dist = jnp.where(dist >= 0, per_dist * LOG2E, NEG_INF).astype(F32)
    skew = jnp.tile(per_dist, (1, t))[:, :t * (period - 1)].reshape(-1, t, period - 1)
    toeplitz = skew[:, :, :2 * t]
    return jnp.stack([toeplitz[:, :, :t], toeplitz[:, :, t:]], axis=1)


def _mix_kernel(x_ref, a_ref, wpg_ref, bg_ref, wpool_ref, pscale_ref, woa_ref, wob_ref,
                wout_ref, g_ref, b_ref, h_ref, pbuf):
    si = pl.program_id(1)
    t = x_ref.shape[1]

    @pl.when(si == 0)
    def _():
        pbuf[0:HALO, :] = jnp.zeros((HALO, WIDTH_B), F32)

    @pl.when(si > 0)
    def _():
        pbuf[0:HALO, :] = pbuf[t:t + HALO, :]

    pbuf[HALO:HALO + t, :] = jnp.dot(x_ref[0].astype(BF16), wpg_ref[:, :WIDTH_B],
                                     preferred_element_type=F32)
    rows = [slice(c * MIX_CHUNK, (c + 1) * MIX_CHUNK) for c in range(t // MIX_CHUNK)]
    gate_logits = [jnp.dot(x_ref[0, r, :].astype(BF16), wpg_ref[:, WIDTH_B:],
                           preferred_element_type=F32) for r in rows]
    for r, gl in zip(rows, gate_logits):
        n = r.stop - r.start
        pos = si * t + r.start + lax.broadcasted_iota(jnp.int32, (n, 1), 0)
        mixed = []
        for gi, w in enumerate(POOL_WINDOWS):
            cols = slice(gi * POOL_GROUP_DIM, (gi + 1) * POOL_GROUP_DIM)
            own = pbuf[HALO + r.start:HALO + r.stop, cols]
            ws = own
            for dlt in range(1, w):
                ws = ws + pbuf[HALO + r.start - dlt:HALO + r.stop - dlt, cols]
            cnt = jnp.minimum(pos + 1, w).astype(F32)
            pooled = ws / cnt - own
            mixed.append(jnp.dot(pooled.astype(BF16), wpool_ref[gi],
                                 preferred_element_type=F32))
        mixed = jnp.concatenate(mixed, axis=1) * pscale_ref[...]
        branch_b = jnp.dot(mixed.astype(BF16), wob_ref[...], preferred_element_type=F32)
        branch_a = jnp.dot(a_ref[0, r, :], woa_ref[...], preferred_element_type=F32)
        gates = jax.nn.sigmoid(gl + bg_ref[...])
        mixin = gates[:, :D_MODEL] * branch_a + gates[:, D_MODEL:] * branch_b
        mix = jnp.dot(mixin.astype(BF16), wout_ref[...], preferred_element_type=F32)
        h_ref[0, r, :] = _layer_norm(DEEPNORM_ALPHA * x_ref[0, r, :] + mix,
                                     g_ref[...], b_ref[...])


def _mix_ln1(x, a, w_pg, b_gate, w_pool, pool_scale, w_oa, w_ob, w_out, ln_g, ln_b, t=1024):
    b, s, d = x.shape
    const2 = lambda bi, si: (0, 0)
    once = pl.Buffered(1)
    return pl.pallas_call(
        _mix_kernel,
        grid=(b, s // t),
        in_specs=[
            pl.BlockSpec((1, t, d), lambda bi, si: (bi, si, 0)),
            pl.BlockSpec((1, t, WIDTH_A), lambda bi, si: (bi, si, 0)),
            pl.BlockSpec(w_pg.shape, const2, pipeline_mode=once),
            pl.BlockSpec(b_gate.shape, const2),
            pl.BlockSpec(w_pool.shape, lambda bi, si: (0, 0, 0)),
            pl.BlockSpec(pool_scale.shape, const2),
            pl.BlockSpec(w_oa.shape, const2, pipeline_mode=once),
            pl.BlockSpec(w_ob.shape, const2, pipeline_mode=once),
            pl.BlockSpec(w_out.shape, const2, pipeline_mode=once),
            pl.BlockSpec(ln_g.shape, const2),
            pl.BlockSpec(ln_b.shape, const2),
        ],
        out_specs=pl.BlockSpec((1, t, d), lambda bi, si: (bi, si, 0)),
        out_shape=jax.ShapeDtypeStruct((b, s, d), F32),
        scratch_shapes=[pltpu.VMEM((HALO + t, WIDTH_B), F32)],
        compiler_params=pltpu.CompilerParams(
            dimension_semantics=("arbitrary", "arbitrary"),
            vmem_limit_bytes=VMEM_LIMIT),
        name="mix_ln1",
    )(x, a, w_pg, b_gate, w_pool, pool_scale, w_oa, w_ob, w_out, ln_g, ln_b)


FF_CHUNK = 1024
FF_ROWS = 256


def _ffn_kernel(h_ref, w1_ref, b1_ref, w2_ref, b2_ref, g_ref, b_ref, o_ref):
    for r0 in range(0, h_ref.shape[0], FF_ROWS):
        h = h_ref[r0:r0 + FF_ROWS, :]
        hb = h.astype(BF16)
        ff = jnp.zeros(h.shape, F32)
        for c in range(D_FF // FF_CHUNK):
            cs = slice(c * FF_CHUNK, (c + 1) * FF_CHUNK)
            u = jnp.dot(hb, w1_ref[:, cs], preferred_element_type=F32) + b1_ref[:, cs]
            u = jnp.square(jnp.maximum(u, 0.0))
            ff = ff + jnp.dot(u.astype(BF16), w2_ref[cs, :], preferred_element_type=F32)
        ff = ff + b2_ref[...]
        o_ref[r0:r0 + FF_ROWS, :] = _layer_norm(DEEPNORM_ALPHA * h + ff, g_ref[...], b_ref[...])


def _ffn_ln2(h, w1, b1, w2, b2, ln_g, ln_b, t=1024):
    n, d = h.shape
    const2 = lambda i: (0, 0)
    once = pl.Buffered(1)
    return pl.pallas_call(
        _ffn_kernel,
        grid=(n // t,),
        in_specs=[
            pl.BlockSpec((t, d), lambda i: (i, 0)),
            pl.BlockSpec(w1.shape, const2, pipeline_mode=once),
            pl.BlockSpec(b1.shape, const2),
            pl.BlockSpec(w2.shape, const2, pipeline_mode=once),
            pl.BlockSpec(b2.shape, const2),
            pl.BlockSpec(ln_g.shape, const2),
            pl.BlockSpec(ln_b.shape, const2),
        ],
        out_specs=pl.BlockSpec((t, d), lambda i: (i, 0)),
        out_shape=jax.ShapeDtypeStruct((n, d), F32),
        compiler_params=pltpu.CompilerParams(
            dimension_semantics=("arbitrary",),
            vmem_limit_bytes=VMEM_LIMIT),
        name="ffn_ln2",
    )(h, w1, b1, w2, b2, ln_g, ln_b)


def kernel(x, w_in, b_gate, rel_table, w_pool, pool_scale, w_o_attn, w_o_pool, w_out,
           ln1_g, ln1_b, w_ff1, b_ff1, w_ff2, b_ff2, ln2_g, ln2_b):
    b, s, d = x.shape
    nb = s // MOBA_BLOCK
    assert s % MOBA_BLOCK == 0 and nb <= MAX_KEY_BLOCKS
    bias_tiles = _bias_tiles(rel_table)
    h = x
    for l in range(w_in.shape[0]):
        w_q = w_in[l, :, :WIDTH_A] * (HEAD_DIM ** -0.5)
        w_k = w_in[l, :, WIDTH_A:2 * WIDTH_A].astype(BF16)
        w_v = w_in[l, :, 2 * WIDTH_A:3 * WIDTH_A]
        w_qvt = jnp.concatenate([w_q, w_v], axis=1).T.astype(BF16)
        w_pg = w_in[l, :, 3 * WIDTH_A:].astype(BF16)
        q_t, k_aug, v_t = _qkv_proj(h, w_k, w_qvt)
        a = _moba_attention(q_t, k_aug, v_t, bias_tiles)
        h = _mix_ln1(h, a, w_pg, b_gate[l][None, :], w_pool[l].astype(BF16),
                     pool_scale[l][None, :], w_o_attn[l].astype(BF16),
                     w_o_pool[l].astype(BF16), w_out[l].astype(BF16),
                     ln1_g[l][None, :], ln1_b[l][None, :])
        h = _ffn_ln2(h.reshape(b * s, d), w_ff1[l].astype(BF16), b_ff1[l][None, :],
                     w_ff2[l].astype(BF16), b_ff2[l][None, :],
                     ln2_g[l][None, :], ln2_b[l][None, :]).reshape(b, s, d)
    return h
```

```python
import math

import jax
import jax.numpy as jnp
from jax import lax
from jax.experimental import pallas as pl
from jax.experimental.pallas import tpu as pltpu

D_MODEL = 1024
N_HEADS = 8
HEAD_DIM = 64
WIDTH_A = N_HEADS * HEAD_DIM
MOBA_BLOCK = 256
MOBA_TOPK = 3
N_POOL_GROUPS = 4
POOL_WINDOWS = (2, 4, 8, 16)
WIDTH_B = 512
POOL_GROUP_DIM = WIDTH_B // N_POOL_GROUPS
D_FF = 4 * D_MODEL
REL_BUCKETS = 32
REL_MAX_DIST = 128
DEEPNORM_ALPHA = 2.0 ** 0.25
LN_EPS = 1e-5
NEG_INF = -1e30

LANES = 128
MAX_KEY_BLOCKS = 32
HEADS_PER_STEP = 8
V_ROWS = HEAD_DIM + 16
LOG2E = math.log2(math.e)
HALO = 16
MIX_CHUNK = 256
VMEM_LIMIT = 48 * 1024 * 1024

BF16 = jnp.bfloat16
F32 = jnp.float32
NT_DIMS = (((1,), (1,)), ((), ()))


def _layer_norm(y, g, b):
    mu = jnp.mean(y, axis=-1, keepdims=True)
    d = y - mu
    var = jnp.mean(d * d, axis=-1, keepdims=True)
    return d * lax.rsqrt(var + LN_EPS) * g + b


def _qkv_kernel(x_ref, wk_ref, wqvt_ref, q_ref, k_ref, v_ref, kmean_sc):
    si = pl.program_id(1)

    @pl.when(si == 0)
    def _():
        kmean_sc[...] = jnp.zeros(kmean_sc.shape, F32)

    xb = x_ref[0].astype(BF16)
    t = xb.shape[0]
    blocks_per_tile = t // MOBA_BLOCK
    acc_k =jnp.dot(xb, wk_ref[...], preferred_element_type=F32)
    acc_q = lax.dot_general(wqvt_ref[0:WIDTH_A, :], xb, NT_DIMS,
                            preferred_element_type=F32)
    lane = lax.broadcasted_iota(jnp.int32, (t, LANES), 1)
    row = lax.broadcasted_iota(jnp.int32, (t, LANES), 0)
    lo = lane < HEAD_DIM
    onehot = jnp.where(lane == HEAD_DIM + si * blocks_per_tile + row // MOBA_BLOCK, 1.0, 0.0)
    for hp in range(N_HEADS // 2):
        kp = acc_k[:, hp * LANES:(hp + 1) * LANES] * LOG2E
        kr = pltpu.roll(kp, HEAD_DIM, 1)
        k_ref[0, 2 * hp] = jnp.where(lo, kp, onehot).astype(BF16)
        k_ref[0, 2 * hp + 1] = jnp.where(lo, kr, onehot).astype(BF16)

    blk_row =lax.broadcasted_iota(jnp.int32, (MAX_KEY_BLOCKS, LANES), 0)
    lo_row = lax.broadcasted_iota(jnp.int32, (MAX_KEY_BLOCKS, LANES), 1) < HEAD_DIM
    for c in range(blocks_per_tile):
        mean_c = jnp.mean(acc_k[c * MOBA_BLOCK:(c + 1) * MOBA_BLOCK], axis=0, keepdims=True)
        for hp in range(N_HEADS // 2):
            pair = mean_c[:, hp * LANES:(hp + 1) * LANES]
            for h, m in ((2 * hp, pair), (2 * hp + 1, pltpu.roll(pair, HEAD_DIM, 1))):
                new_row = jnp.where(lo_row, jnp.broadcast_to(m, (MAX_KEY_BLOCKS, LANES)), 0.0)
                kmean_sc[h] = jnp.where(blk_row == si * blocks_per_tile + c, new_row, kmean_sc[h])

    sel_row = lax.broadcasted_iota(jnp.int32, (MAX_KEY_BLOCKS, t), 0)
    own_blk = si * blocks_per_tile + lax.broadcasted_iota(
        jnp.int32, (MAX_KEY_BLOCKS, t), 1) // MOBA_BLOCK
    past = sel_row < own_blk
    zeros64 = jnp.zeros((LANES - HEAD_DIM, t), BF16)
    zeros32 = jnp.zeros((LANES - HEAD_DIM - MAX_KEY_BLOCKS, t), BF16)
    pad1 = jnp.ones((V_ROWS - HEAD_DIM, t), BF16)
    for h in range(N_HEADS):
        qt = acc_q[h * HEAD_DIM:(h + 1) * HEAD_DIM].astype(BF16)
        gate = jnp.dot(kmean_sc[h].astype(BF16), jnp.concatenate([qt, zeros64], axis=0),
                       preferred_element_type=F32)
        g = jnp.where(past, gate, NEG_INF)
        sel = jnp.zeros(g.shape, jnp.bool_)
        for _ in range(MOBA_TOPK):
            mx = jnp.max(g, axis=0, keepdims=True)
            first = jnp.min(jnp.where(g == mx, sel_row, MAX_KEY_BLOCKS), axis=0, keepdims=True)
            pick = sel_row == first
            sel = sel | pick
            g = jnp.where(pick, -jnp.inf, g)
        sel = (sel & past) | (sel_row == own_blk)
        q_ref[0, h, 0:HEAD_DIM, :] = qt
        q_ref[0, h, HEAD_DIM:HEAD_DIM + MAX_KEY_BLOCKS, :] = jnp.where(sel, 0.0, NEG_INF).astype(BF16)
        q_ref[0, h, HEAD_DIM + MAX_KEY_BLOCKS:LANES, :] = zeros32

    acc_v = lax.dot_general(wqvt_ref[WIDTH_A:2 * WIDTH_A, :], xb, NT_DIMS,
                            preferred_element_type=F32)
    for h in range(N_HEADS):
        v_ref[0, h, 0:HEAD_DIM, :] = acc_v[h * HEAD_DIM:(h + 1) * HEAD_DIM].astype(BF16)
        v_ref[0, h, HEAD_DIM:V_ROWS, :] = pad1


def _qkv_proj(x, w_k, w_qvt, t=2 * MOBA_BLOCK):
    b, s, d = x.shape
    return pl.pallas_call(
        _qkv_kernel,
        grid=(b, s // t),
        in_specs=[
            pl.BlockSpec((1, t, d), lambda bi, si: (bi, si, 0)),
            pl.BlockSpec(w_k.shape, lambda bi, si: (0, 0)),
            pl.BlockSpec(w_qvt.shape, lambda bi, si: (0, 0)),
        ],
        out_specs=[
            pl.BlockSpec((1, N_HEADS, LANES, t), lambda bi, si: (bi, 0, 0, si)),
            pl.BlockSpec((1, N_HEADS, t, LANES), lambda bi, si: (bi, 0, si, 0)),
            pl.BlockSpec((1, N_HEADS, V_ROWS, t), lambda bi, si: (bi, 0, 0, si)),
        ],
        out_shape=[
            jax.ShapeDtypeStruct((b, N_HEADS, LANES, s), BF16),
            jax.ShapeDtypeStruct((b, N_HEADS, s, LANES), BF16),
            jax.ShapeDtypeStruct((b, N_HEADS, V_ROWS, s), BF16),
        ],
        scratch_shapes=[pltpu.VMEM((N_HEADS, MAX_KEY_BLOCKS, LANES), F32)],
        compiler_params=pltpu.CompilerParams(
            dimension_semantics=("arbitrary", "arbitrary"),
            vmem_limit_bytes=VMEM_LIMIT),
        name="qkv_proj",
    )(x, w_k, w_qvt)


def _attn_kernel(q_ref, k_ref, v_ref, bias_ref, o_ref, k_all, v_all, s_a, s_b, acc_sc, m_sc,
                 max_sc):
    i = pl.program_id(2)
    t = MOBA_BLOCK
    heads = range(HEADS_PER_STEP)
    own = pl.ds(pl.multiple_of(i * t, t), t)
    for h in heads:
        k_all[h, own, :] = k_ref[0, h]
        v_all[h, :, own] = v_ref[0, h]

    def key_block(h, j):
        return k_all[h, pl.ds(pl.multiple_of(j * t, t), t), :]

    def head_scores_into(s_dst, h, j):
        s = jnp.dot(key_block(h, j), q_ref[0, h], preferred_element_type=F32)
        s_dst[h] = s
        return jnp.max(s, axis=0, keepdims=True)

    def weighted_values(h, j, p):
        vv = v_all[h, :, pl.ds(pl.multiple_of(j * t, t), t)]
        return jnp.dot(vv, p.astype(BF16), preferred_element_type=F32)

    def step(j, ms, s_cur, max_cur, s_nxt, bias_idx=None):
        new_ms, max_nxt = [], []
        for h in heads:
            if s_nxt is not None:
                max_nxt.append(head_scores_into(s_nxt, h, j + 1))
            s = s_cur[h]
            if bias_idx is not None:
                s = s + bias_ref[h, bias_idx]
                m_new = jnp.maximum(ms[h], jnp.max(s, axis=0, keepdims=True))
            else:
                m_new = jnp.maximum(ms[h], max_cur[h])
            alpha = jnp.exp2(ms[h] - m_new)
            acc_sc[h] = alpha * acc_sc[h] + weighted_values(h, j, jnp.exp2(s - m_new))
            new_ms.append(m_new)
        return new_ms, max_nxt

    ms = []
    for h in heads:
        s = jnp.dot(k_ref[0, h], q_ref[0, h], preferred_element_type=F32) + bias_ref[h, 0]
        s_b[h] = s
        ms.append(jnp.max(s, axis=0, keepdims=True))

    max_a = []
    for h in heads:
        max_a.append(head_scores_into(s_a, h, 0))
        acc_sc[h] = jnp.dot(v_ref[0, h], jnp.exp2(s_b[h] - ms[h]).astype(BF16),
                            preferred_element_type=F32)

    n_far = jnp.maximum(i - 1, 0)

    def two_steps(j, ms, max_a):
        ms, max_b = step(j, list(ms), s_a, list(max_a), s_b)
        return step(j + 1, ms, s_b, max_b, s_a)

    def four_steps(jj, carry):
        ms, max_a = two_steps(4 * jj, *carry)
        ms, max_a = two_steps(4 * jj + 2, ms, max_a)
        return tuple(ms), tuple(max_a)

    ms, max_a = lax.fori_loop(0, n_far // 4, four_steps, (tuple(ms), tuple(max_a)))
    for h in heads:
        m_sc[h] = ms[h]
        max_sc[h] = max_a[h]
    left = n_far % 4
    j_left = n_far - left

    def finish():
        for hp in range(HEADS_PER_STEP // 2):
            outs = []
            for h in (2 * hp, 2 * hp + 1):
                acc = acc_sc[h]
                outs.append(acc[0:HEAD_DIM] / acc[HEAD_DIM:HEAD_DIM + 1])
            o_ref[0, :, hp * LANES:(hp + 1) * LANES] = jnp.concatenate(
                outs, axis=0).T.astype(o_ref.dtype)

    @pl.when(i == 0)
    def _():
        finish()

    for n_left in range(4):
        @pl.when((i >= 1) & (left == n_left))
        def _(n_left=n_left):
            ms = [m_sc[h] for h in heads]
            maxes = [max_sc[h] for h in heads]
            bufs = (s_a, s_b)
            for k in range(n_left):
                ms, maxes = step(j_left + k, ms, bufs[k % 2], maxes, bufs[(k + 1) % 2])
            step(i - 1, ms, bufs[n_left % 2], None, None, bias_idx=1)
            finish()


def _moba_attention(q_t, k_aug, v_t, bias_tiles):
    b, h, s, _ = k_aug.shape
    t = MOBA_BLOCK
    g = HEADS_PER_STEP
    once = pl.Buffered(1)
    return pl.pallas_call(
        _attn_kernel,
        grid=(b, h // g, s // t),
        in_specs=[
            pl.BlockSpec((1, g, LANES, t), lambda bi, hg, qi: (bi, hg, 0, qi)),
            pl.BlockSpec((1, g, t, LANES), lambda bi, hg, qi: (bi, hg, qi, 0)),
            pl.BlockSpec((1, g, V_ROWS, t), lambda bi, hg, qi: (bi, hg, 0, qi)),
            pl.BlockSpec((g, 2, t, t), lambda bi, hg, qi: (hg, 0, 0, 0), pipeline_mode=once),
        ],
        out_specs=pl.BlockSpec((1, t, g * HEAD_DIM), lambda bi, hg, qi: (bi, qi, hg)),
        out_shape=jax.ShapeDtypeStruct((b, s, WIDTH_A), BF16),
        scratch_shapes=[
            pltpu.VMEM((g, s, LANES), BF16),
            pltpu.VMEM((g, V_ROWS, s), BF16),
            pltpu.VMEM((g, t, t), F32),
            pltpu.VMEM((g, t, t), F32),
            pltpu.VMEM((g, V_ROWS, t), F32),
            pltpu.VMEM((g, 1, t), F32),
            pltpu.VMEM((g, 1, t), F32),
        ],
        compiler_params=pltpu.CompilerParams(
            dimension_semantics=("arbitrary", "arbitrary", "arbitrary"),
            vmem_limit_bytes=VMEM_LIMIT),
        name="moba_attn",
    )(q_t, k_aug, v_t, bias_tiles)


def _rel_bucket(dist):
    n = jnp.maximum(dist, 0)
    max_exact = REL_BUCKETS // 2
    nf = jnp.maximum(n, 1).astype(F32)
    large = max_exact + (jnp.log(nf / max_exact) / math.log(REL_MAX_DIST / max_exact)
                         * (REL_BUCKETS - max_exact)).astype(jnp.int32)
    large = jnp.minimum(large, REL_BUCKETS - 1)
    return jnp.where(n < max_exact, n, large)


def _bias_tiles(rel_table):
    t = MOBA_BLOCK
    period = 3 * t + 1
    dist = jnp.arange(period)
    dist = jnp.where(dist < 2 * t, dist, dist - period)
    onehot = _rel_bucket(dist)[None, :] == jnp.arange(REL_BUCKETS)[:, None]
    far = rel_table[_rel_bucket(jnp.array(2 * t, jnp.int32))]
    shifted = (rel_table - far[None, :]).T
    per_dist = jnp.sum(jnp.where(onehot[None], shifted[:, :, None], 0.0), axis=1)
    per_dist = jnp.where(dist >= 0, per_dist * LOG2E, NEG_INF).astype(F32)
    skew = jnp.tile(per_dist, (1, t))[:, :t * (period - 1)].reshape(-1, t, period - 1)
    toeplitz = skew[:, :, :2 * t]
    return jnp.stack([toeplitz[:, :, :t], toeplitz[:, :, t:]], axis=1)


def _mix_kernel(x_ref, a_ref, wpg_ref, bg_ref, wpool_ref, pscale_ref, woa_ref, wob_ref,
                wout_ref, g_ref, b_ref, h_ref, pbuf):
    si = pl.program_id(1)
    t = x_ref.shape[1]

    @pl.when(si == 0)
    def _():
        pbuf[0:HALO, :] = jnp.zeros((HALO, WIDTH_B), F32)

    @pl.when(si > 0)
    def _():
        pbuf[0:HALO, :] = pbuf[t:t + HALO, :]

    pbuf[HALO:HALO + t, :] = jnp.dot(x_ref[0].astype(BF16), wpg_ref[:, :WIDTH_B],
                                     preferred_element_type=F32)
    rows = [slice(c * MIX_CHUNK, (c + 1) * MIX_CHUNK) for c in range(t // MIX_CHUNK)]
    gate_logits = [jnp.dot(x_ref[0, r, :].astype(BF16), wpg_ref[:, WIDTH_B:],
                           preferred_element_type=F32) for r in rows]
    for r, gl in zip(rows, gate_logits):
        n = r.stop - r.start
        pos = si * t + r.start + lax.broadcasted_iota(jnp.int32, (n, 1), 0)
        mixed = []
        for gi, w in enumerate(POOL_WINDOWS):
            cols = slice(gi * POOL_GROUP_DIM, (gi + 1) * POOL_GROUP_DIM)
            own = pbuf[HALO + r.start:HALO + r.stop, cols]
            ws = own
            for dlt in range(1, w):
                ws = ws + pbuf[HALO + r.start - dlt:HALO + r.stop - dlt, cols]
            cnt = jnp.minimum(pos + 1, w).astype(F32)
            pooled = ws / cnt - own
            mixed.append(jnp.dot(pooled.astype(BF16), wpool_ref[gi],
                                 preferred_element_type=F32))
        mixed = jnp.concatenate(mixed, axis=1) * pscale_ref[...]
        branch_b = jnp.dot(mixed.astype(BF16), wob_ref[...], preferred_element_type=F32)
        branch_a = jnp.dot(a_ref[0, r, :], woa_ref[...], preferred_element_type=F32)
        gates = jax.nn.sigmoid(gl + bg_ref[...])
        mixin = gates[:, :D_MODEL] * branch_a + gates[:, D_MODEL:] * branch_b
        mix = jnp.dot(mixin.astype(BF16), wout_ref[...], preferred_element_type=F32)
        h_ref[0, r, :] = _layer_norm(DEEPNORM_ALPHA * x_ref[0, r, :] + mix,
                                     g_ref[...], b_ref[...])


def _mix_ln1(x, a, w_pg, b_gate, w_pool, pool_scale, w_oa, w_ob, w_out, ln_g, ln_b, t=1024):
    b, s, d = x.shape
    const2 = lambda bi, si: (0, 0)
    once = pl.Buffered(1)
    return pl.pallas_call(
        _mix_kernel,
        grid=(b, s // t),
        in_specs=[
            pl.BlockSpec((1, t, d), lambda bi, si: (bi, si, 0)),
            pl.BlockSpec((1, t, WIDTH_A), lambda bi, si: (bi, si, 0)),
            pl.BlockSpec(w_pg.shape, const2, pipeline_mode=once),
            pl.BlockSpec(b_gate.shape, const2),
            pl.BlockSpec(w_pool.shape, lambda bi, si: (0, 0, 0)),
            pl.BlockSpec(pool_scale.shape, const2),
            pl.BlockSpec(w_oa.shape, const2, pipeline_mode=once),
            pl.BlockSpec(w_ob.shape, const2, pipeline_mode=once),
            pl.BlockSpec(w_out.shape, const2, pipeline_mode=once),
            pl.BlockSpec(ln_g.shape, const2),
            pl.BlockSpec(ln_b.shape, const2),
        ],
        out_specs=pl.BlockSpec((1, t, d), lambda bi, si: (bi, si, 0)),
        out_shape=jax.ShapeDtypeStruct((b, s, d), F32),
        scratch_shapes=[pltpu.VMEM((HALO + t, WIDTH_B), F32)],
        compiler_params=pltpu.CompilerParams(
            dimension_semantics=("arbitrary", "arbitrary"),
            vmem_limit_bytes=VMEM_LIMIT),
        name="mix_ln1",
    )(x, a, w_pg, b_gate, w_pool, pool_scale, w_oa, w_ob, w_out, ln_g, ln_b)


FF_CHUNK = 1024
FF_ROWS = 256


def _ffn_kernel(h_ref, w1_ref, b1_ref, w2_ref, b2_ref, g_ref, b_ref, o_ref):
    for r0 in range(0, h_ref.shape[0], FF_ROWS):
        h = h_ref[r0:r0 + FF_ROWS, :]
        hb = h.astype(BF16)
        ff = jnp.zeros(h.shape, F32)
        for c in range(D_FF // FF_CHUNK):
            cs = slice(c * FF_CHUNK, (c + 1) * FF_CHUNK)
            u = jnp.dot(hb, w1_ref[:, cs], preferred_element_type=F32) + b1_ref[:, cs]
            u = jnp.square(jnp.maximum(u, 0.0))
            ff = ff + jnp.dot(u.astype(BF16), w2_ref[cs, :], preferred_element_type=F32)
        ff = ff + b2_ref[...]
        o_ref[r0:r0 + FF_ROWS, :] = _layer_norm(DEEPNORM_ALPHA * h + ff, g_ref[...], b_ref[...])


def _ffn_ln2(h, w1, b1, w2, b2, ln_g, ln_b, t=1024):
    n, d = h.shape
    const2 = lambda i: (0, 0)
    once = pl.Buffered(1)
    return pl.pallas_call(
        _ffn_kernel,
        grid=(n // t,),
        in_specs=[
            pl.BlockSpec((t, d), lambda i: (i, 0)),
            pl.BlockSpec(w1.shape, const2, pipeline_mode=once),
            pl.BlockSpec(b1.shape, const2),
            pl.BlockSpec(w2.shape, const2, pipeline_mode=once),
            pl.BlockSpec(b2.shape, const2),
            pl.BlockSpec(ln_g.shape, const2),
            pl.BlockSpec(ln_b.shape, const2),
        ],
        out_specs=pl.BlockSpec((t, d), lambda i: (i, 0)),
        out_shape=jax.ShapeDtypeStruct((n, d), F32),
        compiler_params=pltpu.CompilerParams(
            dimension_semantics=("arbitrary",),
            vmem_limit_bytes=VMEM_LIMIT),
        name="ffn_ln2",
    )(h, w1, b1, w2, b2, ln_g, ln_b)


def kernel(x, w_in, b_gate, rel_table, w_pool, pool_scale, w_o_attn, w_o_pool, w_out,
           ln1_g, ln1_b, w_ff1, b_ff1, w_ff2, b_ff2, ln2_g, ln2_b):
    b, s, d = x.shape
    nb = s // MOBA_BLOCK
    assert s % MOBA_BLOCK == 0 and nb <= MAX_KEY_BLOCKS
    bias_tiles = _bias_tiles(rel_table)
    h = x
    for l in range(w_in.shape[0]):
        w_q = w_in[l, :, :WIDTH_A] * (HEAD_DIM ** -0.5)
        w_k = w_in[l, :, WIDTH_A:2 * WIDTH_A].astype(BF16)
        w_v = w_in[l, :, 2 * WIDTH_A:3 * WIDTH_A]
        w_qvt = jnp.concatenate([w_q, w_v], axis=1).T.astype(BF16)
        w_pg = w_in[l, :, 3 * WIDTH_A:].astype(BF16)
        q_t, k_aug, v_t = _qkv_proj(h, w_k, w_qvt)
        a = _moba_attention(q_t, k_aug, v_t, bias_tiles)
        h = _mix_ln1(h, a, w_pg, b_gate[l][None, :], w_pool[l].astype(BF16),
                     pool_scale[l][None, :], w_o_attn[l].astype(BF16),
                     w_o_pool[l].astype(BF16), w_out[l].astype(BF16),
                     ln1_g[l][None, :], ln1_b[l][None, :])
        h = _ffn_ln2(h.reshape(b * s, d), w_ff1[l].astype(BF16), b_ff1[l][None, :],
                     w_ff2[l].astype(BF16), b_ff2[l][None, :],
                     ln2_g[l][None, :], ln2_b[l][None, :]).reshape(b, s, d)
    return h
```

```python
import math

import jax
import jax.numpy as jnp
from jax import lax
from jax.experimental import pallas as pl
from jax.experimental.pallas import tpu as pltpu

D_MODEL = 1024
N_HEADS = 8
HEAD_DIM = 64
WIDTH_A = N_HEADS * HEAD_DIM
MOBA_BLOCK = 256
MOBA_TOPK = 3
N_POOL_GROUPS = 4
POOL_WINDOWS = (2, 4, 8, 16)
WIDTH_B = 512
POOL_GROUP_DIM = WIDTH_B // N_POOL_GROUPS
D_FF = 4 * D_MODEL
REL_BUCKETS = 32
REL_MAX_DIST = 128
DEEPNORM_ALPHA = 2.0 ** 0.25
LN_EPS = 1e-5
NEG_INF = -1e30

LANES = 128
MAX_KEY_BLOCKS = 32
HEADS_PER_STEP = 8
TILES_PER_STEP = 2
V_ROWS = HEAD_DIM + 16
LOG2E = math.log2(math.e)
HALO = 16
MIX_CHUNK = 256
VMEM_LIMIT = 48 * 1024 * 1024

BF16 = jnp.bfloat16
F32 = jnp.float32
NT_DIMS = (((1,), (1,)), ((), ()))


def _layer_norm(y, g, b):
    mu = jnp.mean(y, axis=-1, keepdims=True)
    d = y - mu
    var = jnp.mean(d * d, axis=-1, keepdims=True)
    return d * lax.rsqrt(var + LN_EPS) * g + b


def _qkv_kernel(x_ref, wk_ref, wqvt_ref, q_ref, k_ref, v_ref, kmean_sc):
    si = pl.program_id(1)

    @pl.when(si == 0)
    def _():
        kmean_sc[...] = jnp.zeros(kmean_sc.shape, F32)

    xb = x_ref[0].astype(BF16)
    t = xb.shape[0]
    blocks_per_tile = t // MOBA_BLOCK
    acc_k =jnp.dot(xb, wk_ref[...], preferred_element_type=F32)
    acc_q = lax.dot_general(wqvt_ref[0:WIDTH_A, :], xb, NT_DIMS,
                            preferred_element_type=F32)
    lane = lax.broadcasted_iota(jnp.int32, (t, LANES), 1)
    row = lax.broadcasted_iota(jnp.int32, (t, LANES), 0)
    lo = lane < HEAD_DIM
    onehot = jnp.where(lane == HEAD_DIM + si * blocks_per_tile + row // MOBA_BLOCK, 1.0, 0.0)
    for hp in range(N_HEADS // 2):
        kp = acc_k[:, hp * LANES:(hp + 1) * LANES] * LOG2E
        kr = pltpu.roll(kp, HEAD_DIM, 1)
        k_ref[0, 2 * hp] = jnp.where(lo, kp, onehot).astype(BF16)
        k_ref[0, 2 * hp + 1] = jnp.where(lo, kr, onehot).astype(BF16)

    blk_row =lax.broadcasted_iota(jnp.int32, (MAX_KEY_BLOCKS, LANES), 0)
    lo_row = lax.broadcasted_iota(jnp.int32, (MAX_KEY_BLOCKS, LANES), 1) < HEAD_DIM
    for c in range(blocks_per_tile):
        mean_c = jnp.mean(acc_k[c * MOBA_BLOCK:(c + 1) * MOBA_BLOCK], axis=0, keepdims=True)
        for hp in range(N_HEADS // 2):
            pair = mean_c[:, hp * LANES:(hp + 1) * LANES]
            for h, m in ((2 * hp, pair), (2 * hp + 1, pltpu.roll(pair, HEAD_DIM, 1))):
                new_row = jnp.where(lo_row, jnp.broadcast_to(m, (MAX_KEY_BLOCKS, LANES)), 0.0)
                kmean_sc[h] = jnp.where(blk_row == si * blocks_per_tile + c, new_row, kmean_sc[h])

    sel_row = lax.broadcasted_iota(jnp.int32, (MAX_KEY_BLOCKS, t), 0)
    own_blk = si * blocks_per_tile + lax.broadcasted_iota(
        jnp.int32, (MAX_KEY_BLOCKS, t), 1) // MOBA_BLOCK
    past = sel_row < own_blk
    zeros64 = jnp.zeros((LANES - HEAD_DIM, t), BF16)
    zeros32 = jnp.zeros((LANES - HEAD_DIM - MAX_KEY_BLOCKS, t), BF16)
    pad1 = jnp.ones((V_ROWS - HEAD_DIM, t), BF16)
    for h in range(N_HEADS):
        qt = acc_q[h * HEAD_DIM:(h + 1) * HEAD_DIM].astype(BF16)
        gate = jnp.dot(kmean_sc[h].astype(BF16), jnp.concatenate([qt, zeros64], axis=0),
                       preferred_element_type=F32)
        g = jnp.where(past, gate, NEG_INF)
        sel = jnp.zeros(g.shape, jnp.bool_)
        for _ in range(MOBA_TOPK):
            mx = jnp.max(g, axis=0, keepdims=True)
            first = jnp.min(jnp.where(g == mx, sel_row, MAX_KEY_BLOCKS), axis=0, keepdims=True)
            pick = sel_row == first
            sel = sel | pick
            g = jnp.where(pick, -jnp.inf, g)
        sel = (sel & past) | (sel_row == own_blk)
        q_ref[0, h, 0:HEAD_DIM, :] = qt
        q_ref[0, h, HEAD_DIM:HEAD_DIM + MAX_KEY_BLOCKS, :] = jnp.where(sel, 0.0, NEG_INF).astype(BF16)
        q_ref[0, h, HEAD_DIM + MAX_KEY_BLOCKS:LANES, :] = zeros32

    acc_v = lax.dot_general(wqvt_ref[WIDTH_A:2 * WIDTH_A, :], xb, NT_DIMS,
                            preferred_element_type=F32)
    for h in range(N_HEADS):
        v_ref[0, h, 0:HEAD_DIM, :] = acc_v[h * HEAD_DIM:(h + 1) * HEAD_DIM].astype(BF16)
        v_ref[0, h, HEAD_DIM:V_ROWS, :] = pad1


def _qkv_proj(x, w_k, w_qvt, t=2 * MOBA_BLOCK):
    b, s, d = x.shape
    return pl.pallas_call(
        _qkv_kernel,
        grid=(b, s // t),
        in_specs=[
            pl.BlockSpec((1, t, d), lambda bi, si: (bi, si, 0)),
            pl.BlockSpec(w_k.shape, lambda bi, si: (0, 0)),
            pl.BlockSpec(w_qvt.shape, lambda bi, si: (0, 0)),
        ],
        out_specs=[
            pl.BlockSpec((1, N_HEADS, LANES, t), lambda bi, si: (bi, 0, 0, si)),
            pl.BlockSpec((1, N_HEADS, t, LANES), lambda bi, si: (bi, 0, si, 0)),
            pl.BlockSpec((1, N_HEADS, V_ROWS, t), lambda bi, si: (bi, 0, 0, si)),
        ],
        out_shape=[
            jax.ShapeDtypeStruct((b, N_HEADS, LANES, s), BF16),
            jax.ShapeDtypeStruct((b, N_HEADS, s, LANES), BF16),
            jax.ShapeDtypeStruct((b, N_HEADS, V_ROWS, s), BF16),
        ],
        scratch_shapes=[pltpu.VMEM((N_HEADS, MAX_KEY_BLOCKS, LANES), F32)],
        compiler_params=pltpu.CompilerParams(
            dimension_semantics=("arbitrary", "arbitrary"),
            vmem_limit_bytes=VMEM_LIMIT),
        name="qkv_proj",
    )(x, w_k, w_qvt)


def _attn_kernel(*refs):
    def one_tile(sub, carry):
        _attn_tile(pl.program_id(2) * TILES_PER_STEP + sub, sub, *refs)
        return carry

    lax.fori_loop(0, TILES_PER_STEP, one_tile, 0)


def _attn_tile(i, sub, q_ref, k_ref, v_ref, bias_ref, o_ref, k_all, v_all, s_a, s_b, acc_sc,
               m_sc, max_sc):
    t = MOBA_BLOCK
    heads = range(HEADS_PER_STEP)
    own = pl.ds(pl.multiple_of(i * t, t), t)
    cols = pl.ds(pl.multiple_of(sub * t, t), t)
    for h in heads:
        k_all[h, own, :] = k_ref[0, h, cols, :]
        v_all[h, :, own] = v_ref[0, h, :, cols]

    def key_block(h, j):
        return k_all[h, pl.ds(pl.multiple_of(j * t, t), t), :]

    def head_scores_into(s_dst, h, j):
        s = jnp.dot(key_block(h, j), q_ref[0, h, :, cols], preferred_element_type=F32)
        s_dst[h] = s
        return jnp.max(s, axis=0, keepdims=True)

    def weighted_values(h, j, p):
        vv = v_all[h, :, pl.ds(pl.multiple_of(j * t, t), t)]
        return jnp.dot(vv, p.astype(BF16), preferred_element_type=F32)

    def step(j, ms, s_cur, max_cur, s_nxt, bias_idx=None):
        new_ms, max_nxt = [], []
        for h in heads:
            if s_nxt is not None:
                max_nxt.append(head_scores_into(s_nxt, h, j + 1))
            s = s_cur[h]
            if bias_idx is not None:
                s = s + bias_ref[h, bias_idx]
                m_new = jnp.maximum(ms[h], jnp.max(s, axis=0, keepdims=True))
            else:
                m_new = jnp.maximum(ms[h], max_cur[h])
            alpha = jnp.exp2(ms[h] - m_new)
            acc_sc[h] = alpha * acc_sc[h] + weighted_values(h, j, jnp.exp2(s - m_new))
            new_ms.append(m_new)
        return new_ms, max_nxt

    ms = []
    for h in heads:
        s = jnp.dot(k_ref[0, h, cols, :], q_ref[0, h, :, cols],
                    preferred_element_type=F32) + bias_ref[h, 0]
        s_b[h] = s
        ms.append(jnp.max(s, axis=0, keepdims=True))

    max_a = []
    for h in heads:
        max_a.append(head_scores_into(s_a, h, 0))
        acc_sc[h] = jnp.dot(v_ref[0, h, :, cols], jnp.exp2(s_b[h] - ms[h]).astype(BF16),
                            preferred_element_type=F32)

    n_far = jnp.maximum(i - 1, 0)

    def two_steps(j, ms, max_a):
        ms, max_b = step(j, list(ms), s_a, list(max_a), s_b)
        return step(j + 1, ms, s_b, max_b, s_a)

    def four_steps(jj, carry):
        ms, max_a = two_steps(4 * jj, *carry)
        ms, max_a = two_steps(4 * jj + 2, ms, max_a)
        return tuple(ms), tuple(max_a)

    ms, max_a = lax.fori_loop(0, n_far // 4, four_steps, (tuple(ms), tuple(max_a)))
    for h in heads:
        m_sc[h] = ms[h]
        max_sc[h] = max_a[h]
    left = n_far % 4
    j_left = n_far - left

    def finish():
        for hp in range(HEADS_PER_STEP // 2):
            outs = []
            for h in (2 * hp, 2 * hp + 1):
                acc = acc_sc[h]
                outs.append(acc[0:HEAD_DIM] / acc[HEAD_DIM:HEAD_DIM + 1])
            o_ref[0, cols, hp * LANES:(hp + 1) * LANES] = jnp.concatenate(
                outs, axis=0).T.astype(o_ref.dtype)

    @pl.when(i == 0)
    def _():
        finish()

    for n_left in range(4):
        @pl.when((i >= 1) & (left == n_left))
        def _(n_left=n_left):
            ms = [m_sc[h] for h in heads]
            maxes = [max_sc[h] for h in heads]
            bufs = (s_a, s_b)
            for k in range(n_left):
                ms, maxes = step(j_left + k, ms, bufs[k % 2], maxes, bufs[(k + 1) % 2])
            step(i - 1, ms, bufs[n_left % 2], None, None, bias_idx=1)
            finish()


def _moba_attention(q_t, k_aug, v_t, bias_tiles):
    b, h, s, _ = k_aug.shape
    t = MOBA_BLOCK
    g = HEADS_PER_STEP
    once = pl.Buffered(1)
    ts = t * TILES_PER_STEP
    return pl.pallas_call(
        _attn_kernel,
        grid=(b, h // g, s // ts),
        in_specs=[
            pl.BlockSpec((1, g, LANES, ts), lambda bi, hg, qi: (bi, hg, 0, qi)),
            pl.BlockSpec((1, g, ts, LANES), lambda bi, hg, qi: (bi, hg, qi, 0)),
            pl.BlockSpec((1, g, V_ROWS, ts), lambda bi, hg, qi: (bi, hg, 0, qi)),
            pl.BlockSpec((g, 2, t, t), lambda bi, hg, qi: (hg, 0, 0, 0), pipeline_mode=once),
        ],
        out_specs=pl.BlockSpec((1, ts, g * HEAD_DIM), lambda bi, hg, qi: (bi, qi, hg)),
        out_shape=jax.ShapeDtypeStruct((b, s, WIDTH_A), BF16),
        scratch_shapes=[
            pltpu.VMEM((g, s, LANES), BF16),
            pltpu.VMEM((g, V_ROWS, s), BF16),
            pltpu.VMEM((g, t, t), F32),
            pltpu.VMEM((g, t, t), F32),
            pltpu.VMEM((g, V_ROWS, t), F32),
            pltpu.VMEM((g, 1, t), F32),
            pltpu.VMEM((g, 1, t), F32),
        ],
        compiler_params=pltpu.CompilerParams(
            dimension_semantics=("arbitrary", "arbitrary", "arbitrary"),
            vmem_limit_bytes=VMEM_LIMIT),
        name="moba_attn",
    )(q_t, k_aug, v_t, bias_tiles)


def _rel_bucket(dist):
    n = jnp.maximum(dist, 0)
    max_exact = REL_BUCKETS // 2
    nf = jnp.maximum(n, 1).astype(F32)
    large = max_exact + (jnp.log(nf / max_exact) / math.log(REL_MAX_DIST / max_exact)
                         * (REL_BUCKETS - max_exact)).astype(jnp.int32)
    large = jnp.minimum(large, REL_BUCKETS - 1)
    return jnp.where(n < max_exact, n, large)


def _bias_tiles(rel_table):
    t = MOBA_BLOCK
    period = 3 * t + 1
    dist = jnp.arange(period)
    dist = jnp.where(dist < 2 * t, dist, dist - period)
    onehot = _rel_bucket(dist)[None, :] == jnp.arange(REL_BUCKETS)[:, None]
    far = rel_table[_rel_bucket(jnp.array(2 * t, jnp.int32))]
    shifted = (rel_table - far[None, :]).T
    per_dist = jnp.sum(jnp.where(onehot[None], shifted[:, :, None], 0.0), axis=1)
    per_dist = jnp.where(dist >= 0, per_dist * LOG2E, NEG_INF).astype(F32)
    skew = jnp.tile(per_dist, (1, t))[:, :t * (period - 1)].reshape(-1, t, period - 1)
    toeplitz = skew[:, :, :2 * t]
    return jnp.stack([toeplitz[:, :, :t], toeplitz[:, :, t:]], axis=1)


def _mix_kernel(x_ref, a_ref, wpg_ref, bg_ref, wpool_ref, pscale_ref, woa_ref, wob_ref,
                wout_ref, g_ref, b_ref, h_ref, pbuf):
    si = pl.program_id(1)
    t = x_ref.shape[1]

    @pl.when(si == 0)
    def _():
        pbuf[0:HALO, :] = jnp.zeros((HALO, WIDTH_B), F32)

    @pl.when(si > 0)
    def _():
        pbuf[0:HALO, :] = pbuf[t:t + HALO, :]

    pbuf[HALO:HALO + t, :] = jnp.dot(x_ref[0].astype(BF16), wpg_ref[:, :WIDTH_B],
                                     preferred_element_type=F32)
    rows = [slice(c * MIX_CHUNK, (c + 1) * MIX_CHUNK) for c in range(t // MIX_CHUNK)]
    gate_logits = [jnp.dot(x_ref[0, r, :].astype(BF16), wpg_ref[:, WIDTH_B:],
                           preferred_element_type=F32) for r in rows]
    for r, gl in zip(rows, gate_logits):
        n = r.stop - r.start
        pos = si * t + r.start + lax.broadcasted_iota(jnp.int32, (n, 1), 0)
        mixed = []
        for gi, w in enumerate(POOL_WINDOWS):
            cols = slice(gi * POOL_GROUP_DIM, (gi + 1) * POOL_GROUP_DIM)
            own = pbuf[HALO + r.start:HALO + r.stop, cols]
            ws = own
            for dlt in range(1, w):
                ws = ws + pbuf[HALO + r.start - dlt:HALO + r.stop - dlt, cols]
            cnt = jnp.minimum(pos + 1, w).astype(F32)
            pooled = ws / cnt - own
            mixed.append(jnp.dot(pooled.astype(BF16), wpool_ref[gi],
                                 preferred_element_type=F32))
        mixed = jnp.concatenate(mixed, axis=1) * pscale_ref[...]
        branch_b = jnp.dot(mixed.astype(BF16), wob_ref[...], preferred_element_type=F32)
        branch_a = jnp.dot(a_ref[0, r, :], woa_ref[...], preferred_element_type=F32)
        gates = jax.nn.sigmoid(gl + bg_ref[...])
        mixin = gates[:, :D_MODEL] * branch_a + gates[:, D_MODEL:] * branch_b
        mix = jnp.dot(mixin.astype(BF16), wout_ref[...], preferred_element_type=F32)
        h_ref[0, r, :] = _layer_norm(DEEPNORM_ALPHA * x_ref[0, r, :] + mix,
                                     g_ref[...], b_ref[...])


def _mix_ln1(x, a, w_pg, b_gate, w_pool, pool_scale, w_oa, w_ob, w_out, ln_g, ln_b, t=1024):
    b, s, d = x.shape
    const2 = lambda bi, si: (0, 0)
    once = pl.Buffered(1)
    return pl.pallas_call(
        _mix_kernel,
        grid=(b, s // t),
        in_specs=[
            pl.BlockSpec((1, t, d), lambda bi, si: (bi, si, 0)),
            pl.BlockSpec((1, t, WIDTH_A), lambda bi, si: (bi, si, 0)),
            pl.BlockSpec(w_pg.shape, const2, pipeline_mode=once),
            pl.BlockSpec(b_gate.shape, const2),
            pl.BlockSpec(w_pool.shape, lambda bi, si: (0, 0, 0)),
            pl.BlockSpec(pool_scale.shape, const2),
            pl.BlockSpec(w_oa.shape, const2, pipeline_mode=once),
            pl.BlockSpec(w_ob.shape, const2, pipeline_mode=once),
            pl.BlockSpec(w_out.shape, const2, pipeline_mode=once),
            pl.BlockSpec(ln_g.shape, const2),
            pl.BlockSpec(ln_b.shape, const2),
        ],
        out_specs=pl.BlockSpec((1, t, d), lambda bi, si: (bi, si, 0)),
        out_shape=jax.ShapeDtypeStruct((b, s, d), F32),
        scratch_shapes=[pltpu.VMEM((HALO + t, WIDTH_B), F32)],
        compiler_params=pltpu.CompilerParams(
            dimension_semantics=("arbitrary", "arbitrary"),
            vmem_limit_bytes=VMEM_LIMIT),
        name="mix_ln1",
    )(x, a, w_pg, b_gate, w_pool, pool_scale, w_oa, w_ob, w_out, ln_g, ln_b)


FF_CHUNK = 1024
FF_ROWS = 256


def _ffn_kernel(h_ref, w1_ref, b1_ref, w2_ref, b2_ref, g_ref, b_ref, o_ref):
    for r0 in range(0, h_ref.shape[0], FF_ROWS):
        h = h_ref[r0:r0 + FF_ROWS, :]
        hb = h.astype(BF16)
        ff = jnp.zeros(h.shape, F32)
        for c in range(D_FF // FF_CHUNK):
            cs = slice(c * FF_CHUNK, (c + 1) * FF_CHUNK)
            u = jnp.dot(hb, w1_ref[:, cs], preferred_element_type=F32) + b1_ref[:, cs]
            u = jnp.square(jnp.maximum(u, 0.0))
            ff = ff + jnp.dot(u.astype(BF16), w2_ref[cs, :], preferred_element_type=F32)
        ff = ff + b2_ref[...]
        o_ref[r0:r0 + FF_ROWS, :] = _layer_norm(DEEPNORM_ALPHA * h + ff, g_ref[...], b_ref[...])


def _ffn_ln2(h, w1, b1, w2, b2, ln_g, ln_b, t=1024):
    n, d = h.shape
    const2 = lambda i: (0, 0)
    once = pl.Buffered(1)
    return pl.pallas_call(
        _ffn_kernel,
        grid=(n // t,),
        in_specs=[
            pl.BlockSpec((t, d), lambda i: (i, 0)),
            pl.BlockSpec(w1.shape, const2, pipeline_mode=once),
            pl.BlockSpec(b1.shape, const2),
            pl.BlockSpec(w2.shape, const2, pipeline_mode=once),
            pl.BlockSpec(b2.shape, const2),
            pl.BlockSpec(ln_g.shape, const2),
            pl.BlockSpec(ln_b.shape, const2),
        ],
        out_specs=pl.BlockSpec((t, d), lambda i: (i, 0)),
        out_shape=jax.ShapeDtypeStruct((n, d), F32),
        compiler_params=pltpu.CompilerParams(
            dimension_semantics=("arbitrary",),
            vmem_limit_bytes=VMEM_LIMIT),
        name="ffn_ln2",
    )(h, w1, b1, w2, b2, ln_g, ln_b)


def kernel(x, w_in, b_gate, rel_table, w_pool, pool_scale, w_o_attn, w_o_pool, w_out,
           ln1_g, ln1_b, w_ff1, b_ff1, w_ff2, b_ff2, ln2_g, ln2_b):
    b, s, d = x.shape
    nb = s // MOBA_BLOCK
    assert s % MOBA_BLOCK == 0 and nb <= MAX_KEY_BLOCKS
    bias_tiles = _bias_tiles(rel_table)
    h = x
    for l in range(w_in.shape[0]):
        w_q = w_in[l, :, :WIDTH_A] * (HEAD_DIM ** -0.5)
        w_k = w_in[l, :, WIDTH_A:2 * WIDTH_A].astype(BF16)
        w_v = w_in[l, :, 2 * WIDTH_A:3 * WIDTH_A]
        w_qvt = jnp.concatenate([w_q, w_v], axis=1).T.astype(BF16)
        w_pg = w_in[l, :, 3 * WIDTH_A:].astype(BF16)
        q_t, k_aug, v_t = _qkv_proj(h, w_k, w_qvt)
        a = _moba_attention(q_t, k_aug, v_t, bias_tiles)
        h = _mix_ln1(h, a, w_pg, b_gate[l][None, :], w_pool[l].astype(BF16),
                     pool_scale[l][None, :], w_o_attn[l].astype(BF16),
                     w_o_pool[l].astype(BF16), w_out[l].astype(BF16),
                     ln1_g[l][None, :], ln1_b[l][None, :])
        h = _ffn_ln2(h.reshape(b * s, d), w_ff1[l].astype(BF16), b_ff1[l][None, :],
                     w_ff2[l].astype(BF16), b_ff2[l][None, :],
                     ln2_g[l][None, :], ln2_b[l][None, :]).reshape(b, s, d)
    return h
```

```python
import math

import jax
import jax.numpy as jnp
from jax import lax
from jax.experimental import pallas as pl
from jax.experimental.pallas import tpu as pltpu

D_MODEL = 1024
N_HEADS = 8
HEAD_DIM = 64
WIDTH_A = N_HEADS * HEAD_DIM
MOBA_BLOCK = 256
MOBA_TOPK = 3
N_POOL_GROUPS = 4
POOL_WINDOWS = (2, 4, 8, 16)
WIDTH_B = 512
POOL_GROUP_DIM = WIDTH_B // N_POOL_GROUPS
D_FF = 4 * D_MODEL
REL_BUCKETS = 32
REL_MAX_DIST = 128
DEEPNORM_ALPHA = 2.0 ** 0.25
LN_EPS = 1e-5
NEG_INF = -1e30

LANES = 128
MAX_KEY_BLOCKS = 32
HEADS_PER_STEP = 8
V_ROWS = HEAD_DIM + 16
LOG2E = math.log2(math.e)
HALO = 16
MIX_CHUNK = 256
VMEM_LIMIT = 48 * 1024 * 1024

BF16 = jnp.bfloat16
F32 = jnp.float32
NT_DIMS = (((1,), (1,)), ((), ()))


def _layer_norm(y, g, b):
    mu = jnp.mean(y, axis=-1, keepdims=True)
    d = y - mu
    var = jnp.mean(d * d, axis=-1, keepdims=True)
    return d * lax.rsqrt(var + LN_EPS) * g + b


def _qkv_kernel(x_ref, wk_ref, wqvt_ref, q_ref, k_ref, v_ref, kmean_sc):
    si = pl.program_id(1)

    @pl.when(si == 0)
    def _():
        kmean_sc[...] = jnp.zeros(kmean_sc.shape, F32)

    xb = x_ref[0].astype(BF16)
    t = xb.shape[0]
    blocks_per_tile = t // MOBA_BLOCK
    acc_k =jnp.dot(xb, wk_ref[...], preferred_element_type=F32)
    acc_q = lax.dot_general(wqvt_ref[0:WIDTH_A, :], xb, NT_DIMS,
                            preferred_element_type=F32)
    lane = lax.broadcasted_iota(jnp.int32, (t, LANES), 1)
    row = lax.broadcasted_iota(jnp.int32, (t, LANES), 0)
    lo = lane < HEAD_DIM
    onehot = jnp.where(lane == HEAD_DIM + si * blocks_per_tile + row // MOBA_BLOCK, 1.0, 0.0)
    for hp in range(N_HEADS // 2):
        kp = acc_k[:, hp * LANES:(hp + 1) * LANES] * LOG2E
        kr = pltpu.roll(kp, HEAD_DIM, 1)
        k_ref[0, 2 * hp] = jnp.where(lo, kp, onehot).astype(BF16)
        k_ref[0, 2 * hp + 1] = jnp.where(lo, kr, onehot).astype(BF16)

    blk_row =lax.broadcasted_iota(jnp.int32, (MAX_KEY_BLOCKS, LANES), 0)
    lo_row = lax.broadcasted_iota(jnp.int32, (MAX_KEY_BLOCKS, LANES), 1) < HEAD_DIM
    for c in range(blocks_per_tile):
        mean_c = jnp.mean(acc_k[c * MOBA_BLOCK:(c + 1) * MOBA_BLOCK], axis=0, keepdims=True)
        for hp in range(N_HEADS // 2):
            pair = mean_c[:, hp * LANES:(hp + 1) * LANES]
            for h, m in ((2 * hp, pair), (2 * hp + 1, pltpu.roll(pair, HEAD_DIM, 1))):
                new_row = jnp.where(lo_row, jnp.broadcast_to(m, (MAX_KEY_BLOCKS, LANES)), 0.0)
                kmean_sc[h] = jnp.where(blk_row == si * blocks_per_tile + c, new_row, kmean_sc[h])

    sel_row = lax.broadcasted_iota(jnp.int32, (MAX_KEY_BLOCKS, t), 0)
    own_blk = si * blocks_per_tile + lax.broadcasted_iota(
        jnp.int32, (MAX_KEY_BLOCKS, t), 1) // MOBA_BLOCK
    past = sel_row < own_blk
    zeros64 = jnp.zeros((LANES - HEAD_DIM, t), BF16)
    zeros32 = jnp.zeros((LANES - HEAD_DIM - MAX_KEY_BLOCKS, t), BF16)
    pad1 = jnp.ones((V_ROWS - HEAD_DIM, t), BF16)
    for h in range(N_HEADS):
        qt = acc_q[h * HEAD_DIM:(h + 1) * HEAD_DIM].astype(BF16)
        gate = jnp.dot(kmean_sc[h].astype(BF16), jnp.concatenate([qt, zeros64], axis=0),
                       preferred_element_type=F32)
        g = jnp.where(past, gate, NEG_INF)
        sel = jnp.zeros(g.shape, jnp.bool_)
        for _ in range(MOBA_TOPK):
            mx = jnp.max(g, axis=0, keepdims=True)
            first = jnp.min(jnp.where(g == mx, sel_row, MAX_KEY_BLOCKS), axis=0, keepdims=True)
            pick = sel_row == first
            sel = sel | pick
            g = jnp.where(pick, -jnp.inf, g)
        sel = (sel & past) | (sel_row == own_blk)
        q_ref[0, h, 0:HEAD_DIM, :] = qt
        q_ref[0, h, HEAD_DIM:HEAD_DIM + MAX_KEY_BLOCKS, :] = jnp.where(sel, 0.0, NEG_INF).astype(BF16)
        q_ref[0, h, HEAD_DIM + MAX_KEY_BLOCKS:LANES, :] = zeros32

    acc_v = lax.dot_general(wqvt_ref[WIDTH_A:2 * WIDTH_A, :], xb, NT_DIMS,
                            preferred_element_type=F32)
    for h in range(N_HEADS):
        v_ref[0, h, 0:HEAD_DIM, :] = acc_v[h * HEAD_DIM:(h + 1) * HEAD_DIM].astype(BF16)
        v_ref[0, h, HEAD_DIM:V_ROWS, :] = pad1


def _qkv_proj(x, w_k, w_qvt, t=2 * MOBA_BLOCK):
    b, s, d = x.shape
    return pl.pallas_call(
        _qkv_kernel,
        grid=(b, s // t),
        in_specs=[
            pl.BlockSpec((1, t, d), lambda bi, si: (bi, si, 0)),
            pl.BlockSpec(w_k.shape, lambda bi, si: (0, 0)),
            pl.BlockSpec(w_qvt.shape, lambda bi, si: (0, 0)),
        ],
        out_specs=[
            pl.BlockSpec((1, N_HEADS, LANES, t), lambda bi, si: (bi, 0, 0, si)),
            pl.BlockSpec((1, N_HEADS, t, LANES), lambda bi, si: (bi, 0, si, 0)),
            pl.BlockSpec((1, N_HEADS, V_ROWS, t), lambda bi, si: (bi, 0, 0, si)),
        ],
        out_shape=[
            jax.ShapeDtypeStruct((b, N_HEADS, LANES, s), BF16),
            jax.ShapeDtypeStruct((b, N_HEADS, s, LANES), BF16),
            jax.ShapeDtypeStruct((b, N_HEADS, V_ROWS, s), BF16),
        ],
        scratch_shapes=[pltpu.VMEM((N_HEADS, MAX_KEY_BLOCKS, LANES), F32)],
        compiler_params=pltpu.CompilerParams(
            dimension_semantics=("arbitrary", "arbitrary"),
            vmem_limit_bytes=VMEM_LIMIT),
        name="qkv_proj",
    )(x, w_k, w_qvt)


def _attn_kernel(q_ref, k_ref, v_ref, qn_ref, kn_ref, bias_ref, o_ref, k_all, v_all, s_a, s_b,
                 s_own, acc_sc, m_sc, max_sc, own_max_sc):
    i = pl.program_id(2)
    t = MOBA_BLOCK
    heads = range(HEADS_PER_STEP)
    own = pl.ds(pl.multiple_of(i * t, t), t)
    for h in heads:
        k_all[h, own, :] = k_ref[0, h]
        v_all[h, :, own] = v_ref[0, h]

    def key_block(h, j):
        return k_all[h, pl.ds(pl.multiple_of(j * t, t), t), :]

    def head_scores_into(s_dst, h, j):
        s = jnp.dot(key_block(h, j), q_ref[0, h], preferred_element_type=F32)
        s_dst[h] = s
        return jnp.max(s, axis=0, keepdims=True)

    def weighted_values(h, j, p):
        vv = v_all[h, :, pl.ds(pl.multiple_of(j * t, t), t)]
        return jnp.dot(vv, p.astype(BF16), preferred_element_type=F32)

    def step(j, ms, s_cur, max_cur, s_nxt, bias_idx=None):
        new_ms, max_nxt = [], []
        for h in heads:
            if s_nxt is not None:
                max_nxt.append(head_scores_into(s_nxt, h, j + 1))
            s = s_cur[h]
            if bias_idx is not None:
                s = s + bias_ref[h, bias_idx]
                m_new = jnp.maximum(ms[h], jnp.max(s, axis=0, keepdims=True))
            else:
                m_new = jnp.maximum(ms[h], max_cur[h])
            alpha = jnp.exp2(ms[h] - m_new)
            acc_sc[h] = alpha * acc_sc[h] + weighted_values(h, j, jnp.exp2(s - m_new))
            new_ms.append(m_new)
        return new_ms, max_nxt

    def own_scores(q_r, k_r):
        for h in heads:
            s = jnp.dot(k_r[0, h], q_r[0, h], preferred_element_type=F32) + bias_ref[h, 0]
            s_own[h] = s
            own_max_sc[h] = jnp.max(s, axis=0, keepdims=True)

    @pl.when(i == 0)
    def _():
        own_scores(q_ref, k_ref)

    ms = [own_max_sc[h] for h in heads]
    max_a = []
    for h in heads:
        max_a.append(head_scores_into(s_a, h, 0))
        acc_sc[h] = jnp.dot(v_ref[0, h], jnp.exp2(s_own[h] - ms[h]).astype(BF16),
                            preferred_element_type=F32)

    n_far = jnp.maximum(i - 1, 0)

    def two_steps(j, ms, max_a):
        ms, max_b = step(j, list(ms), s_a, list(max_a), s_b)
        return step(j + 1, ms, s_b, max_b, s_a)

    def four_steps(jj, carry):
        ms, max_a = two_steps(4 * jj, *carry)
        ms, max_a = two_steps(4 * jj + 2, ms, max_a)
        return tuple(ms), tuple(max_a)

    ms, max_a = lax.fori_loop(0, n_far // 4, four_steps, (tuple(ms), tuple(max_a)))
    for h in heads:
        m_sc[h] = ms[h]
        max_sc[h] = max_a[h]
    left = n_far % 4
    j_left = n_far - left

    def finish():
        for hp in range(HEADS_PER_STEP // 2):
            outs = []
            for h in (2 * hp, 2 * hp + 1):
                acc = acc_sc[h]
                outs.append(acc[0:HEAD_DIM] / acc[HEAD_DIM:HEAD_DIM + 1])
            o_ref[0, :, hp * LANES:(hp + 1) * LANES] = jnp.concatenate(
                outs, axis=0).T.astype(o_ref.dtype)

    @pl.when(i == 0)
    def _():
        own_scores(qn_ref, kn_ref)
        finish()

    for n_left in range(4):
        @pl.when((i >= 1) & (left == n_left))
        def _(n_left=n_left):
            ms = [m_sc[h] for h in heads]
            maxes = [max_sc[h] for h in heads]
            bufs = (s_a, s_b)
            for k in range(n_left):
                ms, maxes = step(j_left + k, ms, bufs[k % 2], maxes, bufs[(k + 1) % 2])
            step(i - 1, ms, bufs[n_left % 2], None, None, bias_idx=1)
            own_scores(qn_ref, kn_ref)
            finish()


def _moba_attention(q_t, k_aug, v_t, bias_tiles):
    b, h, s, _ = k_aug.shape
    t = MOBA_BLOCK
    g = HEADS_PER_STEP
    once = pl.Buffered(1)
    nxt = lambda qi: jnp.minimum(qi + 1, s // t - 1)
    return pl.pallas_call(
        _attn_kernel,
        grid=(b, h // g, s // t),
        in_specs=[
            pl.BlockSpec((1, g, LANES, t), lambda bi, hg, qi: (bi, hg, 0, qi)),
            pl.BlockSpec((1, g, t, LANES), lambda bi, hg, qi: (bi, hg, qi, 0)),
            pl.BlockSpec((1, g, V_ROWS, t), lambda bi, hg, qi: (bi, hg, 0, qi)),
            pl.BlockSpec((1, g, LANES, t), lambda bi, hg, qi: (bi, hg, 0, nxt(qi))),
            pl.BlockSpec((1, g, t, LANES), lambda bi, hg, qi: (bi, hg, nxt(qi), 0)),
            pl.BlockSpec((g, 2, t, t), lambda bi, hg, qi: (hg, 0, 0, 0), pipeline_mode=once),
        ],
        out_specs=pl.BlockSpec((1, t, g * HEAD_DIM), lambda bi, hg, qi: (bi, qi, hg)),
        out_shape=jax.ShapeDtypeStruct((b, s, WIDTH_A), BF16),
        scratch_shapes=[
            pltpu.VMEM((g, s, LANES), BF16),
            pltpu.VMEM((g, V_ROWS, s), BF16),
            pltpu.VMEM((g, t, t), F32),
            pltpu.VMEM((g, t, t), F32),
            pltpu.VMEM((g, t, t), F32),
            pltpu.VMEM((g, V_ROWS, t), F32),
            pltpu.VMEM((g, 1, t), F32),
            pltpu.VMEM((g, 1, t), F32),
            pltpu.VMEM((g, 1, t), F32),
        ],
        compiler_params=pltpu.CompilerParams(
            dimension_semantics=("arbitrary", "arbitrary", "arbitrary"),
            vmem_limit_bytes=VMEM_LIMIT),
        name="moba_attn",
    )(q_t, k_aug, v_t, q_t, k_aug, bias_tiles)


def _rel_bucket(dist):
    n = jnp.maximum(dist, 0)
    max_exact = REL_BUCKETS // 2
    nf = jnp.maximum(n, 1).astype(F32)
    large = max_exact + (jnp.log(nf / max_exact) / math.log(REL_MAX_DIST / max_exact)
                         * (REL_BUCKETS - max_exact)).astype(jnp.int32)
    large = jnp.minimum(large, REL_BUCKETS - 1)
    return jnp.where(n < max_exact, n, large)


def _bias_tiles(rel_table):
    t = MOBA_BLOCK
    period = 3 * t + 1
    dist = jnp.arange(period)
    dist = jnp.where(dist < 2 * t, dist, dist - period)
    onehot = _rel_bucket(dist)[None, :] == jnp.arange(REL_BUCKETS)[:, None]
    far = rel_table[_rel_bucket(jnp.array(2 * t, jnp.int32))]
    shifted = (rel_table - far[None, :]).T
    per_dist = jnp.sum(jnp.where(onehot[None], shifted[:, :, None], 0.0), axis=1)
    per_dist = jnp.where(dist >= 0, per_dist * LOG2E, NEG_INF).astype(F32)
    skew = jnp.tile(per_dist, (1, t))[:, :t * (period - 1)].reshape(-1, t, period - 1)
    toeplitz = skew[:, :, :2 * t]
    return jnp.stack([toeplitz[:, :, :t], toeplitz[:, :, t:]], axis=1)


def _mix_kernel(x_ref, a_ref, wpg_ref, bg_ref, wpool_ref, pscale_ref, woa_ref, wob_ref,
                wout_ref, g_ref, b_ref, h_ref, pbuf):
    si = pl.program_id(1)
    t = x_ref.shape[1]

    @pl.when(si == 0)
    def _():
        pbuf[0:HALO, :] = jnp.zeros((HALO, WIDTH_B), F32)

    @pl.when(si > 0)
    def _():
        pbuf[0:HALO, :] = pbuf[t:t + HALO, :]

    pbuf[HALO:HALO + t, :] = jnp.dot(x_ref[0].astype(BF16), wpg_ref[:, :WIDTH_B],
                                     preferred_element_type=F32)
    rows = [slice(c * MIX_CHUNK, (c + 1) * MIX_CHUNK) for c in range(t // MIX_CHUNK)]
    gate_logits = [jnp.dot(x_ref[0, r, :].astype(BF16), wpg_ref[:, WIDTH_B:],
                           preferred_element_type=F32) for r in rows]
    for r, gl in zip(rows, gate_logits):
        n = r.stop - r.start
        pos = si * t + r.start + lax.broadcasted_iota(jnp.int32, (n, 1), 0)
        mixed = []
        for gi, w in enumerate(POOL_WINDOWS):
            cols = slice(gi * POOL_GROUP_DIM, (gi + 1) * POOL_GROUP_DIM)
            own = pbuf[HALO + r.start:HALO + r.stop, cols]
            ws = pbuf[r.start:HALO + r.stop, cols]
            span = 1
            while span < w:
                ws = ws + pltpu.roll(ws, span, 0)
                span *= 2
            ws = ws[HALO:]
            cnt = jnp.minimum(pos + 1, w).astype(F32)
            pooled = ws / cnt - own
            mixed.append(jnp.dot(pooled.astype(BF16), wpool_ref[gi],
                                 preferred_element_type=F32))
        mixed = jnp.concatenate(mixed, axis=1) * pscale_ref[...]
        branch_b = jnp.dot(mixed.astype(BF16), wob_ref[...], preferred_element_type=F32)
        branch_a = jnp.dot(a_ref[0, r, :], woa_ref[...], preferred_element_type=F32)
        gates = jax.nn.sigmoid(gl + bg_ref[...])
        mixin = gates[:, :D_MODEL] * branch_a + gates[:, D_MODEL:] * branch_b
        mix = jnp.dot(mixin.astype(BF16), wout_ref[...], preferred_element_type=F32)
        h_ref[0, r, :] = _layer_norm(DEEPNORM_ALPHA * x_ref[0, r, :] + mix,
                                     g_ref[...], b_ref[...])


def _mix_ln1(x, a, w_pg, b_gate, w_pool, pool_scale, w_oa, w_ob, w_out, ln_g, ln_b, t=1024):
    b, s, d = x.shape
    const2 = lambda bi, si: (0, 0)
    once = pl.Buffered(1)
    return pl.pallas_call(
        _mix_kernel,
        grid=(b, s // t),
        in_specs=[
            pl.BlockSpec((1, t, d), lambda bi, si: (bi, si, 0)),
            pl.BlockSpec((1, t, WIDTH_A), lambda bi, si: (bi, si, 0)),
            pl.BlockSpec(w_pg.shape, const2, pipeline_mode=once),
            pl.BlockSpec(b_gate.shape, const2),
            pl.BlockSpec(w_pool.shape, lambda bi, si: (0, 0, 0)),
            pl.BlockSpec(pool_scale.shape, const2),
            pl.BlockSpec(w_oa.shape, const2, pipeline_mode=once),
            pl.BlockSpec(w_ob.shape, const2, pipeline_mode=once),
            pl.BlockSpec(w_out.shape, const2, pipeline_mode=once),
            pl.BlockSpec(ln_g.shape, const2),
            pl.BlockSpec(ln_b.shape, const2),
        ],
        out_specs=pl.BlockSpec((1, t, d), lambda bi, si: (bi, si, 0)),
        out_shape=jax.ShapeDtypeStruct((b, s, d), F32),
        scratch_shapes=[pltpu.VMEM((HALO + t, WIDTH_B), F32)],
        compiler_params=pltpu.CompilerParams(
            dimension_semantics=("arbitrary", "arbitrary"),
            vmem_limit_bytes=VMEM_LIMIT),
        name="mix_ln1",
    )(x, a, w_pg, b_gate, w_pool, pool_scale, w_oa, w_ob, w_out, ln_g, ln_b)


FF_CHUNK = 1024
FF_ROWS = 256


def _ffn_kernel(h_ref, w1_ref, b1_ref, w2_ref, b2_ref, g_ref, b_ref, o_ref):
    for r0 in range(0, h_ref.shape[0], FF_ROWS):
        h = h_ref[r0:r0 + FF_ROWS, :]
        hb = h.astype(BF16)
        ff = jnp.zeros(h.shape, F32)
        for c in range(D_FF // FF_CHUNK):
            cs = slice(c * FF_CHUNK, (c + 1) * FF_CHUNK)
            u = jnp.dot(hb, w1_ref[:, cs], preferred_element_type=F32) + b1_ref[:, cs]
            u = jnp.square(jnp.maximum(u, 0.0))
            ff = ff + jnp.dot(u.astype(BF16), w2_ref[cs, :], preferred_element_type=F32)
        ff = ff + b2_ref[...]
        o_ref[r0:r0 + FF_ROWS, :] = _layer_norm(DEEPNORM_ALPHA * h + ff, g_ref[...], b_ref[...])


def _ffn_ln2(h, w1, b1, w2, b2, ln_g, ln_b, t=1024):
    n, d = h.shape
    const2 = lambda i: (0, 0)
    once = pl.Buffered(1)
    return pl.pallas_call(
        _ffn_kernel,
        grid=(n // t,),
        in_specs=[
            pl.BlockSpec((t, d), lambda i: (i, 0)),
            pl.BlockSpec(w1.shape, const2, pipeline_mode=once),
            pl.BlockSpec(b1.shape, const2),
            pl.BlockSpec(w2.shape, const2, pipeline_mode=once),
            pl.BlockSpec(b2.shape, const2),
            pl.BlockSpec(ln_g.shape, const2),
            pl.BlockSpec(ln_b.shape, const2),
        ],
        out_specs=pl.BlockSpec((t, d), lambda i: (i, 0)),
        out_shape=jax.ShapeDtypeStruct((n, d), F32),
        compiler_params=pltpu.CompilerParams(
            dimension_semantics=("arbitrary",),
            vmem_limit_bytes=VMEM_LIMIT),
        name="ffn_ln2",
    )(h, w1, b1, w2, b2, ln_g, ln_b)


def kernel(x, w_in, b_gate, rel_table, w_pool, pool_scale, w_o_attn, w_o_pool, w_out,
           ln1_g, ln1_b, w_ff1, b_ff1, w_ff2, b_ff2, ln2_g, ln2_b):
    b, s, d = x.shape
    nb = s // MOBA_BLOCK
    assert s % 1024 == 0 and nb <= MAX_KEY_BLOCKS
    bias_tiles = _bias_tiles(rel_table)
    h = x
    for l in range(w_in.shape[0]):
        w_q = w_in[l, :, :WIDTH_A] * (HEAD_DIM ** -0.5)
        w_k = w_in[l, :, WIDTH_A:2 * WIDTH_A].astype(BF16)
        w_v = w_in[l, :, 2 * WIDTH_A:3 * WIDTH_A]
        w_qvt = jnp.concatenate([w_q, w_v], axis=1).T.astype(BF16)
        w_pg = w_in[l, :, 3 * WIDTH_A:].astype(BF16)
        q_t, k_aug, v_t = _qkv_proj(h, w_k, w_qvt)
        a = _moba_attention(q_t, k_aug, v_t, bias_tiles)
        h = _mix_ln1(h, a, w_pg, b_gate[l][None, :], w_pool[l].astype(BF16),
                     pool_scale[l][None, :], w_o_attn[l].astype(BF16),
                     w_o_pool[l].astype(BF16), w_out[l].astype(BF16),
                     ln1_g[l][None, :], ln1_b[l][None, :])
        h = _ffn_ln2(h.reshape(b * s, d), w_ff1[l].astype(BF16), b_ff1[l][None, :],
                     w_ff2[l].astype(BF16), b_ff2[l][None, :],
                     ln2_g[l][None, :], ln2_b[l][None, :]).reshape(b, s, d)
    return h
```

```python
import math

import jax
import jax.numpy as jnp
from jax import lax
from jax.experimental import pallas as pl
from jax.experimental.pallas import tpu as pltpu

D_MODEL = 1024
N_HEADS = 8
HEAD_DIM = 64
WIDTH_A = N_HEADS * HEAD_DIM
MOBA_BLOCK = 256
MOBA_TOPK = 3
N_POOL_GROUPS = 4
POOL_WINDOWS = (2, 4, 8, 16)
WIDTH_B = 512
POOL_GROUP_DIM = WIDTH_B // N_POOL_GROUPS
D_FF = 4 * D_MODEL
REL_BUCKETS = 32
REL_MAX_DIST = 128
DEEPNORM_ALPHA = 2.0 ** 0.25
LN_EPS = 1e-5
NEG_INF = -1e30

LANES = 128
MAX_KEY_BLOCKS = 32
HEADS_PER_STEP = 8
V_ROWS = HEAD_DIM + 16
LOG2E = math.log2(math.e)
HALO = 16
MIX_CHUNK = 256
VMEM_LIMIT = 48 * 1024 * 1024

BF16 = jnp.bfloat16
F32 = jnp.float32
NT_DIMS = (((1,), (1,)), ((), ()))


def _layer_norm(y, g, b):
    mu = jnp.mean(y, axis=-1, keepdims=True)
    d = y - mu
    var = jnp.mean(d * d, axis=-1, keepdims=True)
    return d * lax.rsqrt(var + LN_EPS) * g + b


def _qkv_kernel(x_ref, wk_ref, wqvt_ref, q_ref, k_ref, v_ref, kmean_sc):
    si = pl.program_id(1)

    @pl.when(si == 0)
    def _():
        kmean_sc[...] = jnp.zeros(kmean_sc.shape, F32)

    xb = x_ref[0].astype(BF16)
    t = xb.shape[0]
    blocks_per_tile = t // MOBA_BLOCK
    acc_k =jnp.dot(xb, wk_ref[...], preferred_element_type=F32)
    acc_q = lax.dot_general(wqvt_ref[0:WIDTH_A, :], xb, NT_DIMS,
                            preferred_element_type=F32)
    lane = lax.broadcasted_iota(jnp.int32, (t, LANES), 1)
    row = lax.broadcasted_iota(jnp.int32, (t, LANES), 0)
    lo = lane < HEAD_DIM
    onehot = jnp.where(lane == HEAD_DIM + si * blocks_per_tile + row // MOBA_BLOCK, 1.0, 0.0)
    for hp in range(N_HEADS // 2):
        kp = acc_k[:, hp * LANES:(hp + 1) * LANES] * LOG2E
        kr = pltpu.roll(kp, HEAD_DIM, 1)
        k_ref[0, 2 * hp] = jnp.where(lo, kp, onehot).astype(BF16)
        k_ref[0, 2 * hp + 1] = jnp.where(lo, kr, onehot).astype(BF16)

    blk_row =lax.broadcasted_iota(jnp.int32, (MAX_KEY_BLOCKS, LANES), 0)
    lo_row = lax.broadcasted_iota(jnp.int32, (MAX_KEY_BLOCKS, LANES), 1) < HEAD_DIM
    for c in range(blocks_per_tile):
        mean_c = jnp.mean(acc_k[c * MOBA_BLOCK:(c + 1) * MOBA_BLOCK], axis=0, keepdims=True)
        for hp in range(N_HEADS // 2):
            pair = mean_c[:, hp * LANES:(hp + 1) * LANES]
            for h, m in ((2 * hp, pair), (2 * hp + 1, pltpu.roll(pair, HEAD_DIM, 1))):
                new_row = jnp.where(lo_row, jnp.broadcast_to(m, (MAX_KEY_BLOCKS, LANES)), 0.0)
                kmean_sc[h] = jnp.where(blk_row == si * blocks_per_tile + c, new_row, kmean_sc[h])

    sel_row = lax.broadcasted_iota(jnp.int32, (MAX_KEY_BLOCKS, t), 0)
    own_blk = si * blocks_per_tile + lax.broadcasted_iota(
        jnp.int32, (MAX_KEY_BLOCKS, t), 1) // MOBA_BLOCK
    past = sel_row < own_blk
    zeros64 = jnp.zeros((LANES - HEAD_DIM, t), BF16)
    zeros32 = jnp.zeros((LANES - HEAD_DIM - MAX_KEY_BLOCKS, t), BF16)
    pad1 = jnp.ones((V_ROWS - HEAD_DIM, t), BF16)
    for h in range(N_HEADS):
        qt = acc_q[h * HEAD_DIM:(h + 1) * HEAD_DIM].astype(BF16)
        gate = jnp.dot(kmean_sc[h].astype(BF16), jnp.concatenate([qt, zeros64], axis=0),
                       preferred_element_type=F32)
        g = jnp.where(past, gate, NEG_INF)
        sel = jnp.zeros(g.shape, jnp.bool_)
        for _ in range(MOBA_TOPK):
            mx = jnp.max(g, axis=0, keepdims=True)
            first = jnp.min(jnp.where(g == mx, sel_row, MAX_KEY_BLOCKS), axis=0, keepdims=True)
            pick = sel_row == first
            sel = sel | pick
            g = jnp.where(pick, -jnp.inf, g)
        sel = (sel & past) | (sel_row == own_blk)
        selbias = jnp.where(sel, 0.0, NEG_INF).astype(BF16)
        for c in range(blocks_per_tile):
            cs = slice(c * MOBA_BLOCK, (c + 1) * MOBA_BLOCK)
            q_ref[0, h, c, 0:HEAD_DIM, :] = qt[:, cs]
            q_ref[0, h, c, HEAD_DIM:HEAD_DIM + MAX_KEY_BLOCKS, :] = selbias[:, cs]
            q_ref[0, h, c, HEAD_DIM + MAX_KEY_BLOCKS:LANES, :] = zeros32[:, cs]

    acc_v = lax.dot_general(wqvt_ref[WIDTH_A:2 * WIDTH_A, :], xb, NT_DIMS,
                            preferred_element_type=F32)
    for h in range(N_HEADS):
        vt = acc_v[h * HEAD_DIM:(h + 1) * HEAD_DIM].astype(BF16)
        for c in range(blocks_per_tile):
            cs = slice(c * MOBA_BLOCK, (c + 1) * MOBA_BLOCK)
            v_ref[0, h, c, 0:HEAD_DIM, :] = vt[:, cs]
            v_ref[0, h, c, HEAD_DIM:V_ROWS, :] = pad1[:, cs]


def _qkv_proj(x, w_k, w_qvt, t=2 * MOBA_BLOCK):
    b, s, d = x.shape
    return pl.pallas_call(
        _qkv_kernel,
        grid=(b, s // t),
        in_specs=[
            pl.BlockSpec((1, t, d), lambda bi, si: (bi, si, 0)),
            pl.BlockSpec(w_k.shape, lambda bi, si: (0, 0)),
            pl.BlockSpec(w_qvt.shape, lambda bi, si: (0, 0)),
        ],
        out_specs=[
            pl.BlockSpec((1, N_HEADS, t // MOBA_BLOCK, LANES, MOBA_BLOCK),
                         lambda bi, si: (bi, 0, si, 0, 0)),
            pl.BlockSpec((1, N_HEADS, t, LANES), lambda bi, si: (bi, 0, si, 0)),
            pl.BlockSpec((1, N_HEADS, t // MOBA_BLOCK, V_ROWS, MOBA_BLOCK),
                         lambda bi, si: (bi, 0, si, 0, 0)),
        ],
        out_shape=[
            jax.ShapeDtypeStruct((b, N_HEADS, s // MOBA_BLOCK, LANES, MOBA_BLOCK), BF16),
            jax.ShapeDtypeStruct((b, N_HEADS, s, LANES), BF16),
            jax.ShapeDtypeStruct((b, N_HEADS, s // MOBA_BLOCK, V_ROWS, MOBA_BLOCK), BF16),
        ],
        scratch_shapes=[pltpu.VMEM((N_HEADS, MAX_KEY_BLOCKS, LANES), F32)],
        compiler_params=pltpu.CompilerParams(
            dimension_semantics=("arbitrary", "arbitrary"),
            vmem_limit_bytes=VMEM_LIMIT),
        name="qkv_proj",
    )(x, w_k, w_qvt)


def _attn_kernel(q_ref, k_ref, v_ref, bias_ref, o_ref, k_all, v_all, s_a, s_b, acc_sc, m_sc,
                 max_sc):
    i = pl.program_id(2)
    t = MOBA_BLOCK
    heads = range(HEADS_PER_STEP)
    own = pl.ds(pl.multiple_of(i * t, t), t)
    for h in heads:
        k_all[h, own, :] = k_ref[0, h]
        v_all[h, :, own] = v_ref[0, h, 0]

    def key_block(h, j):
        return k_all[h, pl.ds(pl.multiple_of(j * t, t), t), :]

    def head_scores_into(s_dst, h, j):
        s = jnp.dot(key_block(h, j), q_ref[0, h, 0], preferred_element_type=F32)
        s_dst[h] = s
        return jnp.max(s, axis=0, keepdims=True)

    def weighted_values(h, j, p):
        vv = v_all[h, :, pl.ds(pl.multiple_of(j * t, t), t)]
        return jnp.dot(vv, p.astype(BF16), preferred_element_type=F32)

    def step(j, ms, s_cur, max_cur, s_nxt, bias_idx=None):
        new_ms, max_nxt = [], []
        for h in heads:
            if s_nxt is not None:
                max_nxt.append(head_scores_into(s_nxt, h, j + 1))
            s = s_cur[h]
            if bias_idx is not None:
                s = s + bias_ref[h, bias_idx]
                m_new = jnp.maximum(ms[h], jnp.max(s, axis=0, keepdims=True))
            else:
                m_new = jnp.maximum(ms[h], max_cur[h])
            alpha = jnp.exp2(ms[h] - m_new)
            acc_sc[h] = alpha * acc_sc[h] + weighted_values(h, j, jnp.exp2(s - m_new))
            new_ms.append(m_new)
        return new_ms, max_nxt

    ms = []
    for h in heads:
        s = jnp.dot(k_ref[0, h], q_ref[0, h, 0], preferred_element_type=F32) + bias_ref[h, 0]
        s_b[h] = s
        ms.append(jnp.max(s, axis=0, keepdims=True))

    max_a = []
    for h in heads:
        max_a.append(head_scores_into(s_a, h, 0))
        acc_sc[h] = jnp.dot(v_ref[0, h, 0], jnp.exp2(s_b[h] - ms[h]).astype(BF16),
                            preferred_element_type=F32)

    n_far = jnp.maximum(i - 1, 0)

    def two_steps(j, ms, max_a):
        ms, max_b = step(j, list(ms), s_a, list(max_a), s_b)
        return step(j + 1, ms, s_b, max_b, s_a)

    def four_steps(jj, carry):
        ms, max_a = two_steps(4 * jj, *carry)
        ms, max_a = two_steps(4 * jj + 2, ms, max_a)
        return tuple(ms), tuple(max_a)

    ms, max_a = lax.fori_loop(0, n_far // 4, four_steps, (tuple(ms), tuple(max_a)))
    for h in heads:
        m_sc[h] = ms[h]
        max_sc[h] = max_a[h]
    left = n_far % 4
    j_left = n_far - left

    def finish():
        for hp in range(HEADS_PER_STEP // 2):
            outs = []
            for h in (2 * hp, 2 * hp + 1):
                acc = acc_sc[h]
                outs.append(acc[0:HEAD_DIM] / acc[HEAD_DIM:HEAD_DIM + 1])
            o_ref[0, :, hp * LANES:(hp + 1) * LANES] = jnp.concatenate(
                outs, axis=0).T.astype(o_ref.dtype)

    @pl.when(i == 0)
    def _():
        finish()

    for n_left in range(4):
        @pl.when((i >= 1) & (left == n_left))
        def _(n_left=n_left):
            ms = [m_sc[h] for h in heads]
            maxes = [max_sc[h] for h in heads]
            bufs = (s_a, s_b)
            for k in range(n_left):
                ms, maxes = step(j_left + k, ms, bufs[k % 2], maxes, bufs[(k + 1) % 2])
            step(i - 1, ms, bufs[n_left % 2], None, None, bias_idx=1)
            finish()


def _moba_attention(q_t, k_aug, v_t, bias_tiles):
    b, h, s, _ = k_aug.shape
    t = MOBA_BLOCK
    g = HEADS_PER_STEP
    once = pl.Buffered(1)
    return pl.pallas_call(
        _attn_kernel,
        grid=(b, h // g, s // t),
        in_specs=[
            pl.BlockSpec((1, g, 1, LANES, t), lambda bi, hg, qi: (bi, hg, qi, 0, 0)),
            pl.BlockSpec((1, g, t, LANES), lambda bi, hg, qi: (bi, hg, qi, 0)),
            pl.BlockSpec((1, g, 1, V_ROWS, t), lambda bi, hg, qi: (bi, hg, qi, 0, 0)),
            pl.BlockSpec((g, 2, t, t), lambda bi, hg, qi: (hg, 0, 0, 0), pipeline_mode=once),
        ],
        out_specs=pl.BlockSpec((1, t, g * HEAD_DIM), lambda bi, hg, qi: (bi, qi, hg)),
        out_shape=jax.ShapeDtypeStruct((b, s, WIDTH_A), BF16),
        scratch_shapes=[
            pltpu.VMEM((g, s, LANES), BF16),
            pltpu.VMEM((g, V_ROWS, s), BF16),
            pltpu.VMEM((g, t, t), F32),
            pltpu.VMEM((g, t, t), F32),
            pltpu.VMEM((g, V_ROWS, t), F32),
            pltpu.VMEM((g, 1, t), F32),
            pltpu.VMEM((g, 1, t), F32),
        ],
        compiler_params=pltpu.CompilerParams(
            dimension_semantics=("arbitrary", "arbitrary", "arbitrary"),
            vmem_limit_bytes=VMEM_LIMIT),
        name="moba_attn",
    )(q_t, k_aug, v_t, bias_tiles)


def _rel_bucket(dist):
    n = jnp.maximum(dist, 0)
    max_exact = REL_BUCKETS // 2
    nf = jnp.maximum(n, 1).astype(F32)
    large = max_exact + (jnp.log(nf / max_exact) / math.log(REL_MAX_DIST / max_exact)
                         * (REL_BUCKETS - max_exact)).astype(jnp.int32)
    large = jnp.minimum(large, REL_BUCKETS - 1)
    return jnp.where(n < max_exact, n, large)


def _bias_tiles(rel_table):
    t = MOBA_BLOCK
    period = 3 * t + 1
    dist = jnp.arange(period)
    dist = jnp.where(dist < 2 * t, dist, dist - period)
    onehot = _rel_bucket(dist)[None, :] == jnp.arange(REL_BUCKETS)[:, None]
    far = rel_table[_rel_bucket(jnp.array(2 * t, jnp.int32))]
    shifted = (rel_table - far[None, :]).T
    per_dist = jnp.sum(jnp.where(onehot[None], shifted[:, :, None], 0.0), axis=1)
    per_dist = jnp.where(dist >= 0, per_dist * LOG2E, NEG_INF).astype(F32)
    skew = jnp.tile(per_dist, (1, t))[:, :t * (period - 1)].reshape(-1, t, period - 1)
    toeplitz = skew[:, :, :2 * t]
    return jnp.stack([toeplitz[:, :, :t], toeplitz[:, :, t:]], axis=1)


def _mix_kernel(x_ref, a_ref, wpg_ref, bg_ref, wpool_ref, pscale_ref, woa_ref, wob_ref,
                wout_ref, g_ref, b_ref, h_ref, pbuf):
    si = pl.program_id(1)
    t = x_ref.shape[1]

    @pl.when(si == 0)
    def _():
        pbuf[0:HALO, :] = jnp.zeros((HALO, WIDTH_B), F32)

    @pl.when(si > 0)
    def _():
        pbuf[0:HALO, :] = pbuf[t:t + HALO, :]

    pbuf[HALO:HALO + t, :] = jnp.dot(x_ref[0].astype(BF16), wpg_ref[:, :WIDTH_B],
                                     preferred_element_type=F32)
    rows = [slice(c * MIX_CHUNK, (c + 1) * MIX_CHUNK) for c in range(t // MIX_CHUNK)]
    gate_logits = [jnp.dot(x_ref[0, r, :].astype(BF16), wpg_ref[:, WIDTH_B:],
                           preferred_element_type=F32) for r in rows]
    for r, gl in zip(rows, gate_logits):
        n = r.stop - r.start
        pos = si * t + r.start + lax.broadcasted_iota(jnp.int32, (n, 1), 0)
        mixed = []
        for gi, w in enumerate(POOL_WINDOWS):
            cols = slice(gi * POOL_GROUP_DIM, (gi + 1) * POOL_GROUP_DIM)
            own = pbuf[HALO + r.start:HALO + r.stop, cols]
            ws = pbuf[r.start:HALO + r.stop, cols]
            span = 1
            while span < w:
                ws = ws + pltpu.roll(ws, span, 0)
                span *= 2
            ws = ws[HALO:]
            cnt = jnp.minimum(pos + 1, w).astype(F32)
            pooled = ws / cnt - own
            mixed.append(jnp.dot(pooled.astype(BF16), wpool_ref[gi],
                                 preferred_element_type=F32))
        mixed = jnp.concatenate(mixed, axis=1) * pscale_ref[...]
        branch_b = jnp.dot(mixed.astype(BF16), wob_ref[...], preferred_element_type=F32)
        branch_a = jnp.dot(a_ref[0, r, :], woa_ref[...], preferred_element_type=F32)
        gates = jax.nn.sigmoid(gl + bg_ref[...])
        mixin = gates[:, :D_MODEL] * branch_a + gates[:, D_MODEL:] * branch_b
        mix = jnp.dot(mixin.astype(BF16), wout_ref[...], preferred_element_type=F32)
        h_ref[0, r, :] = _layer_norm(DEEPNORM_ALPHA * x_ref[0, r, :] + mix,
                                     g_ref[...], b_ref[...])


def _mix_ln1(x, a, w_pg, b_gate, w_pool, pool_scale, w_oa, w_ob, w_out, ln_g, ln_b, t=1024):
    b, s, d = x.shape
    const2 = lambda bi, si: (0, 0)
    once = pl.Buffered(1)
    return pl.pallas_call(
        _mix_kernel,
        grid=(b, s // t),
        in_specs=[
            pl.BlockSpec((1, t, d), lambda bi, si: (bi, si, 0)),
            pl.BlockSpec((1, t, WIDTH_A), lambda bi, si: (bi, si, 0)),
            pl.BlockSpec(w_pg.shape, const2, pipeline_mode=once),
            pl.BlockSpec(b_gate.shape, const2),
            pl.BlockSpec(w_pool.shape, lambda bi, si: (0, 0, 0)),
            pl.BlockSpec(pool_scale.shape, const2),
            pl.BlockSpec(w_oa.shape, const2, pipeline_mode=once),
            pl.BlockSpec(w_ob.shape, const2, pipeline_mode=once),
            pl.BlockSpec(w_out.shape, const2, pipeline_mode=once),
            pl.BlockSpec(ln_g.shape, const2),
            pl.BlockSpec(ln_b.shape, const2),
        ],
        out_specs=pl.BlockSpec((1, t, d), lambda bi, si: (bi, si, 0)),
        out_shape=jax.ShapeDtypeStruct((b, s, d), F32),
        scratch_shapes=[pltpu.VMEM((HALO + t, WIDTH_B), F32)],
        compiler_params=pltpu.CompilerParams(
            dimension_semantics=("arbitrary", "arbitrary"),
            vmem_limit_bytes=VMEM_LIMIT),
        name="mix_ln1",
    )(x, a, w_pg, b_gate, w_pool, pool_scale, w_oa, w_ob, w_out, ln_g, ln_b)


FF_CHUNK = 1024
FF_ROWS = 256


def _ffn_kernel(h_ref, w1_ref, b1_ref, w2_ref, b2_ref, g_ref, b_ref, o_ref):
    for r0 in range(0, h_ref.shape[0], FF_ROWS):
        h = h_ref[r0:r0 + FF_ROWS, :]
        hb = h.astype(BF16)
        ff = jnp.zeros(h.shape, F32)
        for c in range(D_FF // FF_CHUNK):
            cs = slice(c * FF_CHUNK, (c + 1) * FF_CHUNK)
            u = jnp.dot(hb, w1_ref[:, cs], preferred_element_type=F32) + b1_ref[:, cs]
            u = jnp.square(jnp.maximum(u, 0.0))
            ff = ff + jnp.dot(u.astype(BF16), w2_ref[cs, :], preferred_element_type=F32)
        ff = ff + b2_ref[...]
        o_ref[r0:r0 + FF_ROWS, :] = _layer_norm(DEEPNORM_ALPHA * h + ff, g_ref[...], b_ref[...])


def _ffn_ln2(h, w1, b1, w2, b2, ln_g, ln_b, t=1024):
    n, d = h.shape
    const2 = lambda i: (0, 0)
    once = pl.Buffered(1)
    return pl.pallas_call(
        _ffn_kernel,
        grid=(n // t,),
        in_specs=[
            pl.BlockSpec((t, d), lambda i: (i, 0)),
            pl.BlockSpec(w1.shape, const2, pipeline_mode=once),
            pl.BlockSpec(b1.shape, const2),
            pl.BlockSpec(w2.shape, const2, pipeline_mode=once),
            pl.BlockSpec(b2.shape, const2),
            pl.BlockSpec(ln_g.shape, const2),
            pl.BlockSpec(ln_b.shape, const2),
        ],
        out_specs=pl.BlockSpec((t, d), lambda i: (i, 0)),
        out_shape=jax.ShapeDtypeStruct((n, d), F32),
        compiler_params=pltpu.CompilerParams(
            dimension_semantics=("arbitrary",),
            vmem_limit_bytes=VMEM_LIMIT),
        name="ffn_ln2",
    )(h, w1, b1, w2, b2, ln_g, ln_b)


def kernel(x, w_in, b_gate, rel_table, w_pool, pool_scale, w_o_attn, w_o_pool, w_out,
           ln1_g, ln1_b, w_ff1, b_ff1, w_ff2, b_ff2, ln2_g, ln2_b):
    b, s, d = x.shape
    nb = s // MOBA_BLOCK
    assert s % 1024 == 0 and nb <= MAX_KEY_BLOCKS
    bias_tiles = _bias_tiles(rel_table)
    h = x
    for l in range(w_in.shape[0]):
        w_q = w_in[l, :, :WIDTH_A] * (HEAD_DIM ** -0.5)
        w_k = w_in[l, :, WIDTH_A:2 * WIDTH_A].astype(BF16)
        w_v = w_in[l, :, 2 * WIDTH_A:3 * WIDTH_A]
        w_qvt = jnp.concatenate([w_q, w_v], axis=1).T.astype(BF16)
        w_pg = w_in[l, :, 3 * WIDTH_A:].astype(BF16)
        q_t, k_aug, v_t = _qkv_proj(h, w_k, w_qvt)
        a = _moba_attention(q_t, k_aug, v_t, bias_tiles)
        h = _mix_ln1(h, a, w_pg, b_gate[l][None, :], w_pool[l].astype(BF16),
                     pool_scale[l][None, :], w_o_attn[l].astype(BF16),
                     w_o_pool[l].astype(BF16), w_out[l].astype(BF16),
                     ln1_g[l][None, :], ln1_b[l][None, :])
        h = _ffn_ln2(h.reshape(b * s, d), w_ff1[l].astype(BF16), b_ff1[l][None, :],
                     w_ff2[l].astype(BF16), b_ff2[l][None, :],
                     ln2_g[l][None, :], ln2_b[l][None, :]).reshape(b, s, d)
    return h
```

```python
import math

import jax
import jax.numpy as jnp
from jax import lax
from jax.experimental import pallas as pl
from jax.experimental.pallas import tpu as pltpu

D_MODEL = 1024
N_HEADS = 8
HEAD_DIM = 64
WIDTH_A = N_HEADS * HEAD_DIM
MOBA_BLOCK = 256
MOBA_TOPK = 3
N_POOL_GROUPS = 4
POOL_WINDOWS = (2, 4, 8, 16)
WIDTH_B = 512
POOL_GROUP_DIM = WIDTH_B // N_POOL_GROUPS
D_FF = 4 * D_MODEL
REL_BUCKETS = 32
REL_MAX_DIST = 128
DEEPNORM_ALPHA = 2.0 ** 0.25
LN_EPS = 1e-5
NEG_INF = -1e30

LANES = 128
MAX_KEY_BLOCKS = 32
HEADS_PER_STEP = 8
V_ROWS = HEAD_DIM + 16
LOG2E = math.log2(math.e)
HALO = 16
MIX_CHUNK = 256
VMEM_LIMIT = 48 * 1024 * 1024

BF16 = jnp.bfloat16
F32 = jnp.float32
NT_DIMS = (((1,), (1,)), ((), ()))


def _layer_norm(y, g, b):
    mu = jnp.mean(y, axis=-1, keepdims=True)
    d = y - mu
    var = jnp.mean(d * d, axis=-1, keepdims=True)
    return d * lax.rsqrt(var + LN_EPS) * g + b


def _qkv_kernel(x_ref, wk_ref, wqvt_ref, q_ref, k_ref, v_ref, kmean_sc):
    si = pl.program_id(1)

    @pl.when(si == 0)
    def _():
        kmean_sc[...] = jnp.zeros(kmean_sc.shape, F32)

    xb = x_ref[0].astype(BF16)
    t = xb.shape[0]
    blocks_per_tile = t // MOBA_BLOCK
    acc_k =jnp.dot(xb, wk_ref[...], preferred_element_type=F32)
    acc_q = lax.dot_general(wqvt_ref[0:WIDTH_A, :], xb, NT_DIMS,
                            preferred_element_type=F32)
    lane = lax.broadcasted_iota(jnp.int32, (t, LANES), 1)
    row = lax.broadcasted_iota(jnp.int32, (t, LANES), 0)
    lo = lane < HEAD_DIM
    onehot = jnp.where(lane == HEAD_DIM + si * blocks_per_tile + row // MOBA_BLOCK, 1.0, 0.0)
    for hp in range(N_HEADS // 2):
        kp = acc_k[:, hp * LANES:(hp + 1) * LANES] * LOG2E
        kr = pltpu.roll(kp, HEAD_DIM, 1)
        k_ref[0, 2 * hp] = jnp.where(lo, kp, onehot).astype(BF16)
        k_ref[0, 2 * hp + 1] = jnp.where(lo, kr, onehot).astype(BF16)

    blk_row =lax.broadcasted_iota(jnp.int32, (MAX_KEY_BLOCKS, LANES), 0)
    lo_row = lax.broadcasted_iota(jnp.int32, (MAX_KEY_BLOCKS, LANES), 1) < HEAD_DIM
    for c in range(blocks_per_tile):
        mean_c = jnp.mean(acc_k[c * MOBA_BLOCK:(c + 1) * MOBA_BLOCK], axis=0, keepdims=True)
        for hp in range(N_HEADS // 2):
            pair = mean_c[:, hp * LANES:(hp + 1) * LANES]
            for h, m in ((2 * hp, pair), (2 * hp + 1, pltpu.roll(pair, HEAD_DIM, 1))):
                new_row = jnp.where(lo_row, jnp.broadcast_to(m, (MAX_KEY_BLOCKS, LANES)), 0.0)
                kmean_sc[h] = jnp.where(blk_row == si * blocks_per_tile + c, new_row, kmean_sc[h])

    sel_row = lax.broadcasted_iota(jnp.int32, (MAX_KEY_BLOCKS, t), 0)
    own_blk = si * blocks_per_tile + lax.broadcasted_iota(
        jnp.int32, (MAX_KEY_BLOCKS, t), 1) // MOBA_BLOCK
    past = sel_row < own_blk
    zeros64 = jnp.zeros((LANES - HEAD_DIM, t), BF16)
    zeros32 = jnp.zeros((LANES - HEAD_DIM - MAX_KEY_BLOCKS, t), BF16)
    pad1 = jnp.ones((V_ROWS - HEAD_DIM, t), BF16)
    for h in range(N_HEADS):
        qt = acc_q[h * HEAD_DIM:(h + 1) * HEAD_DIM].astype(BF16)
        gate = jnp.dot(kmean_sc[h].astype(BF16), jnp.concatenate([qt, zeros64], axis=0),
                       preferred_element_type=F32)
        g = jnp.where(past, gate, NEG_INF)
        sel = jnp.zeros(g.shape, jnp.bool_)
        for _ in range(MOBA_TOPK):
            mx = jnp.max(g, axis=0, keepdims=True)
            first = jnp.min(jnp.where(g == mx, sel_row, MAX_KEY_BLOCKS), axis=0, keepdims=True)
            pick = sel_row == first
            sel = sel | pick
            g = jnp.where(pick, -jnp.inf, g)
        sel = (sel & past) | (sel_row == own_blk)
        selbias = jnp.where(sel, 0.0, NEG_INF).astype(BF16)
        for c in range(blocks_per_tile):
            cs = slice(c * MOBA_BLOCK, (c + 1) * MOBA_BLOCK)
            q_ref[0, h, c, 0:HEAD_DIM, :] = qt[:, cs]
            q_ref[0, h, c, HEAD_DIM:HEAD_DIM + MAX_KEY_BLOCKS, :] = selbias[:, cs]
            q_ref[0, h, c, HEAD_DIM + MAX_KEY_BLOCKS:LANES, :] = zeros32[:, cs]

    acc_v = lax.dot_general(wqvt_ref[WIDTH_A:2 * WIDTH_A, :], xb, NT_DIMS,
                            preferred_element_type=F32)
    for h in range(N_HEADS):
        vt = acc_v[h * HEAD_DIM:(h + 1) * HEAD_DIM].astype(BF16)
        for c in range(blocks_per_tile):
            cs = slice(c * MOBA_BLOCK, (c + 1) * MOBA_BLOCK)
            v_ref[0, h, c, 0:HEAD_DIM, :] = vt[:, cs]
            v_ref[0, h, c, HEAD_DIM:V_ROWS, :] = pad1[:, cs]


def _qkv_proj(x, w_k, w_qvt, t=2 * MOBA_BLOCK):
    b, s, d = x.shape
    return pl.pallas_call(
        _qkv_kernel,
        grid=(b, s // t),
        in_specs=[
            pl.BlockSpec((1, t, d), lambda bi, si: (bi, si, 0)),
            pl.BlockSpec(w_k.shape, lambda bi, si: (0, 0)),
            pl.BlockSpec(w_qvt.shape, lambda bi, si: (0, 0)),
        ],
        out_specs=[
            pl.BlockSpec((1, N_HEADS, t // MOBA_BLOCK, LANES, MOBA_BLOCK),
                         lambda bi, si: (bi, 0, si, 0, 0)),
            pl.BlockSpec((1, N_HEADS, t, LANES), lambda bi, si: (bi, 0, si, 0)),
            pl.BlockSpec((1, N_HEADS, t // MOBA_BLOCK, V_ROWS, MOBA_BLOCK),
                         lambda bi, si: (bi, 0, si, 0, 0)),
        ],
        out_shape=[
            jax.ShapeDtypeStruct((b, N_HEADS, s // MOBA_BLOCK, LANES, MOBA_BLOCK), BF16),
            jax.ShapeDtypeStruct((b, N_HEADS, s, LANES), BF16),
            jax.ShapeDtypeStruct((b, N_HEADS, s // MOBA_BLOCK, V_ROWS, MOBA_BLOCK), BF16),
        ],
        scratch_shapes=[pltpu.VMEM((N_HEADS, MAX_KEY_BLOCKS, LANES), F32)],
        compiler_params=pltpu.CompilerParams(
            dimension_semantics=("arbitrary", "arbitrary"),
            vmem_limit_bytes=VMEM_LIMIT),
        name="qkv_proj",
    )(x, w_k, w_qvt)


def _attn_kernel(q_ref, k_ref, v_ref, pd_ref, o_ref, k_all, v_all, s_a, s_b, acc_sc, m_sc,
                 max_sc, bias_sc):
    i = pl.program_id(2)
    t = MOBA_BLOCK
    heads = range(HEADS_PER_STEP)

    @pl.when((pl.program_id(0) == 0) & (i == 0))
    def _():
        for h in heads:
            skew = pltpu.roll(jnp.broadcast_to(pd_ref[h], (t, 3 * t)), 0, 1,
                              stride=1, stride_axis=0)
            bias_sc[h, 0] = skew[:, 0:t]
            bias_sc[h, 1] = skew[:, t:2 * t]

    own = pl.ds(pl.multiple_of(i * t, t), t)
    for h in heads:
        k_all[h, own, :] = k_ref[0, h]
        v_all[h, :, own] = v_ref[0, h, 0]

    def key_block(h, j):
        return k_all[h, pl.ds(pl.multiple_of(j * t, t), t), :]

    def head_scores_into(s_dst, h, j):
        s = jnp.dot(key_block(h, j), q_ref[0, h, 0], preferred_element_type=F32)
        s_dst[h] = s
        return jnp.max(s, axis=0, keepdims=True)

    def weighted_values(h, j, p):
        vv = v_all[h, :, pl.ds(pl.multiple_of(j * t, t), t)]
        return jnp.dot(vv, p.astype(BF16), preferred_element_type=F32)

    def step(j, ms, s_cur, max_cur, s_nxt, bias_idx=None):
        new_ms, max_nxt = [], []
        for h in heads:
            if s_nxt is not None:
                max_nxt.append(head_scores_into(s_nxt, h, j + 1))
            s = s_cur[h]
            if bias_idx is not None:
                s = s + bias_sc[h, bias_idx]
                m_new = jnp.maximum(ms[h], jnp.max(s, axis=0, keepdims=True))
            else:
                m_new = jnp.maximum(ms[h], max_cur[h])
            alpha = jnp.exp2(ms[h] - m_new)
            acc_sc[h] = alpha * acc_sc[h] + weighted_values(h, j, jnp.exp2(s - m_new))
            new_ms.append(m_new)
        return new_ms, max_nxt

    ms = []
    for h in heads:
        s = jnp.dot(k_ref[0, h], q_ref[0, h, 0], preferred_element_type=F32) + bias_sc[h, 0]
        s_b[h] = s
        ms.append(jnp.max(s, axis=0, keepdims=True))

    max_a = []
    for h in heads:
        max_a.append(head_scores_into(s_a, h, 0))
        acc_sc[h] = jnp.dot(v_ref[0, h, 0], jnp.exp2(s_b[h] - ms[h]).astype(BF16),
                            preferred_element_type=F32)

    n_far = jnp.maximum(i - 1, 0)

    def two_steps(j, ms, max_a):
        ms, max_b = step(j, list(ms), s_a, list(max_a), s_b)
        return step(j + 1, ms, s_b, max_b, s_a)

    def four_steps(jj, carry):
        ms, max_a = two_steps(4 * jj, *carry)
        ms, max_a = two_steps(4 * jj + 2, ms, max_a)
        return tuple(ms), tuple(max_a)

    ms, max_a = lax.fori_loop(0, n_far // 4, four_steps, (tuple(ms), tuple(max_a)))
    for h in heads:
        m_sc[h] = ms[h]
        max_sc[h] = max_a[h]
    left = n_far % 4
    j_left = n_far - left

    def finish():
        for hp in range(HEADS_PER_STEP // 2):
            outs = []
            for h in (2 * hp, 2 * hp + 1):
                acc = acc_sc[h]
                outs.append(acc[0:HEAD_DIM] / acc[HEAD_DIM:HEAD_DIM + 1])
            o_ref[0, :, hp * LANES:(hp + 1) * LANES] = jnp.concatenate(
                outs, axis=0).T.astype(o_ref.dtype)

    @pl.when(i == 0)
    def _():
        finish()

    for n_left in range(4):
        @pl.when((i >= 1) & (left == n_left))
        def _(n_left=n_left):
            ms = [m_sc[h] for h in heads]
            maxes = [max_sc[h] for h in heads]
            bufs = (s_a, s_b)
            for k in range(n_left):
                ms, maxes = step(j_left + k, ms, bufs[k % 2], maxes, bufs[(k + 1) % 2])
            step(i - 1, ms, bufs[n_left % 2], None, None, bias_idx=1)
            finish()


def _moba_attention(q_t, k_aug, v_t, bias_per_dist):
    b, h, s, _ = k_aug.shape
    t = MOBA_BLOCK
    g = HEADS_PER_STEP
    once = pl.Buffered(1)
    return pl.pallas_call(
        _attn_kernel,
        grid=(b, h // g, s // t),
        in_specs=[
            pl.BlockSpec((1, g, 1, LANES, t), lambda bi, hg, qi: (bi, hg, qi, 0, 0)),
            pl.BlockSpec((1, g, t, LANES), lambda bi, hg, qi: (bi, hg, qi, 0)),
            pl.BlockSpec((1, g, 1, V_ROWS, t), lambda bi, hg, qi: (bi, hg, qi, 0, 0)),
            pl.BlockSpec((g, 1, 3 * t), lambda bi, hg, qi: (hg, 0, 0), pipeline_mode=once),
        ],
        out_specs=pl.BlockSpec((1, t, g * HEAD_DIM), lambda bi, hg, qi: (bi, qi, hg)),
        out_shape=jax.ShapeDtypeStruct((b, s, WIDTH_A), BF16),
        scratch_shapes=[
            pltpu.VMEM((g, s, LANES), BF16),
            pltpu.VMEM((g, V_ROWS, s), BF16),
            pltpu.VMEM((g, t, t), F32),
            pltpu.VMEM((g, t, t), F32),
            pltpu.VMEM((g, V_ROWS, t), F32),
            pltpu.VMEM((g, 1, t), F32),
            pltpu.VMEM((g, 1, t), F32),
            pltpu.VMEM((g, 2, t, t), F32),
        ],
        compiler_params=pltpu.CompilerParams(
            dimension_semantics=("arbitrary", "arbitrary", "arbitrary"),
            vmem_limit_bytes=VMEM_LIMIT),
        name="moba_attn",
    )(q_t, k_aug, v_t, bias_per_dist)


def _rel_bucket(dist):
    n = jnp.maximum(dist, 0)
    max_exact = REL_BUCKETS // 2
    nf = jnp.maximum(n, 1).astype(F32)
    large = max_exact + (jnp.log(nf / max_exact) / math.log(REL_MAX_DIST / max_exact)
                         * (REL_BUCKETS - max_exact)).astype(jnp.int32)
    large = jnp.minimum(large, REL_BUCKETS - 1)
    return jnp.where(n < max_exact, n, large)


def _bias_per_distance(rel_table):
    t = MOBA_BLOCK
    period = 3 * t
    dist = jnp.arange(period)
    dist = jnp.where(dist < 2 * t, dist, dist - period)
    onehot = _rel_bucket(dist)[None, :] == jnp.arange(REL_BUCKETS)[:, None]
    far = rel_table[_rel_bucket(jnp.array(2 * t, jnp.int32))]
    shifted = (rel_table - far[None, :]).T
    per_dist = jnp.sum(jnp.where(onehot[None], shifted[:, :, None], 0.0), axis=1)
    per_dist = jnp.where(dist >= 0, per_dist * LOG2E, NEG_INF).astype(F32)
    return per_dist[:, None, :]


def _mix_kernel(x_ref, a_ref, wpg_ref, bg_ref, wpool_ref, pscale_ref, woa_ref, wob_ref,
                wout_ref, g_ref, b_ref, h_ref, pbuf):
    si = pl.program_id(1)
    t = x_ref.shape[1]

    @pl.when(si == 0)
    def _():
        pbuf[0:HALO, :] = jnp.zeros((HALO, WIDTH_B), F32)

    @pl.when(si > 0)
    def _():
        pbuf[0:HALO, :] = pbuf[t:t + HALO, :]

    pbuf[HALO:HALO + t, :] = jnp.dot(x_ref[0].astype(BF16), wpg_ref[:, :WIDTH_B],
                                     preferred_element_type=F32)
    rows = [slice(c * MIX_CHUNK, (c + 1) * MIX_CHUNK) for c in range(t // MIX_CHUNK)]
    gate_logits = [jnp.dot(x_ref[0, r, :].astype(BF16), wpg_ref[:, WIDTH_B:],
                           preferred_element_type=F32) for r in rows]
    for r, gl in zip(rows, gate_logits):
        n = r.stop - r.start
        pos = si * t + r.start + lax.broadcasted_iota(jnp.int32, (n, 1), 0)
        mixed = []
        for gi, w in enumerate(POOL_WINDOWS):
            cols = slice(gi * POOL_GROUP_DIM, (gi + 1) * POOL_GROUP_DIM)
            own = pbuf[HALO + r.start:HALO + r.stop, cols]
            ws = pbuf[r.start:HALO + r.stop, cols]
            span = 1
            while span < w:
                ws = ws + pltpu.roll(ws, span, 0)
                span *= 2
            ws = ws[HALO:]
            cnt = jnp.minimum(pos + 1, w).astype(F32)
            pooled = ws / cnt - own
            mixed.append(jnp.dot(pooled.astype(BF16), wpool_ref[gi],
                                 preferred_element_type=F32))
        mixed = jnp.concatenate(mixed, axis=1) * pscale_ref[...]
        branch_b = jnp.dot(mixed.astype(BF16), wob_ref[...], preferred_element_type=F32)
        branch_a = jnp.dot(a_ref[0, r, :], woa_ref[...], preferred_element_type=F32)
        gates = jax.nn.sigmoid(gl + bg_ref[...])
        mixin = gates[:, :D_MODEL] * branch_a + gates[:, D_MODEL:] * branch_b
        mix = jnp.dot(mixin.astype(BF16), wout_ref[...], preferred_element_type=F32)
        h_ref[0, r, :] = _layer_norm(DEEPNORM_ALPHA * x_ref[0, r, :] + mix,
                                     g_ref[...], b_ref[...])


def _mix_ln1(x, a, w_pg, b_gate, w_pool, pool_scale, w_oa, w_ob, w_out, ln_g, ln_b, t=1024):
    b, s, d = x.shape
    const2 = lambda bi, si: (0, 0)
    once = pl.Buffered(1)
    return pl.pallas_call(
        _mix_kernel,
        grid=(b, s // t),
        in_specs=[
            pl.BlockSpec((1, t, d), lambda bi, si: (bi, si, 0)),
            pl.BlockSpec((1, t, WIDTH_A), lambda bi, si: (bi, si, 0)),
            pl.BlockSpec(w_pg.shape, const2, pipeline_mode=once),
            pl.BlockSpec(b_gate.shape, const2),
            pl.BlockSpec(w_pool.shape, lambda bi, si: (0, 0, 0)),
            pl.BlockSpec(pool_scale.shape, const2),
            pl.BlockSpec(w_oa.shape, const2, pipeline_mode=once),
            pl.BlockSpec(w_ob.shape, const2, pipeline_mode=once),
            pl.BlockSpec(w_out.shape, const2, pipeline_mode=once),
            pl.BlockSpec(ln_g.shape, const2),
            pl.BlockSpec(ln_b.shape, const2),
        ],
        out_specs=pl.BlockSpec((1, t, d), lambda bi, si: (bi, si, 0)),
        out_shape=jax.ShapeDtypeStruct((b, s, d), F32),
        scratch_shapes=[pltpu.VMEM((HALO + t, WIDTH_B), F32)],
        compiler_params=pltpu.CompilerParams(
            dimension_semantics=("arbitrary", "arbitrary"),
            vmem_limit_bytes=VMEM_LIMIT),
        name="mix_ln1",
    )(x, a, w_pg, b_gate, w_pool, pool_scale, w_oa, w_ob, w_out, ln_g, ln_b)


FF_CHUNK = 1024
FF_ROWS = 256


def _ffn_kernel(h_ref, w1_ref, b1_ref, w2_ref, b2_ref, g_ref, b_ref, o_ref):
    for r0 in range(0, h_ref.shape[0], FF_ROWS):
        h = h_ref[r0:r0 + FF_ROWS, :]
        hb = h.astype(BF16)
        ff = jnp.zeros(h.shape, F32)
        for c in range(D_FF // FF_CHUNK):
            cs = slice(c * FF_CHUNK, (c + 1) * FF_CHUNK)
            u = jnp.dot(hb, w1_ref[:, cs], preferred_element_type=F32) + b1_ref[:, cs]
            u = jnp.square(jnp.maximum(u, 0.0))
            ff = ff + jnp.dot(u.astype(BF16), w2_ref[cs, :], preferred_element_type=F32)
        ff = ff + b2_ref[...]
        o_ref[r0:r0 + FF_ROWS, :] = _layer_norm(DEEPNORM_ALPHA * h + ff, g_ref[...], b_ref[...])


def _ffn_ln2(h, w1, b1, w2, b2, ln_g, ln_b, t=1024):
    n, d = h.shape
    const2 = lambda i: (0, 0)
    once = pl.Buffered(1)
    return pl.pallas_call(
        _ffn_kernel,
        grid=(n // t,),
        in_specs=[
            pl.BlockSpec((t, d), lambda i: (i, 0)),
            pl.BlockSpec(w1.shape, const2, pipeline_mode=once),
            pl.BlockSpec(b1.shape, const2),
            pl.BlockSpec(w2.shape, const2, pipeline_mode=once),
            pl.BlockSpec(b2.shape, const2),
            pl.BlockSpec(ln_g.shape, const2),
            pl.BlockSpec(ln_b.shape, const2),
        ],
        out_specs=pl.BlockSpec((t, d), lambda i: (i, 0)),
        out_shape=jax.ShapeDtypeStruct((n, d), F32),
        compiler_params=pltpu.CompilerParams(
            dimension_semantics=("arbitrary",),
            vmem_limit_bytes=VMEM_LIMIT),
        name="ffn_ln2",
    )(h, w1, b1, w2, b2, ln_g, ln_b)


def kernel(x, w_in, b_gate, rel_table, w_pool, pool_scale, w_o_attn, w_o_pool, w_out,
           ln1_g, ln1_b, w_ff1, b_ff1, w_ff2, b_ff2, ln2_g, ln2_b):
    b, s, d = x.shape
    nb = s // MOBA_BLOCK
    assert s % 1024 == 0 and nb <= MAX_KEY_BLOCKS
    bias_per_dist = _bias_per_distance(rel_table)
    h = x
    for l in range(w_in.shape[0]):
        w_q = w_in[l, :, :WIDTH_A] * (HEAD_DIM ** -0.5)
        w_k = w_in[l, :, WIDTH_A:2 * WIDTH_A].astype(BF16)
        w_v = w_in[l, :, 2 * WIDTH_A:3 * WIDTH_A]
        w_qvt = jnp.concatenate([w_q, w_v], axis=1).T.astype(BF16)
        w_pg = w_in[l, :, 3 * WIDTH_A:].astype(BF16)
        q_t, k_aug, v_t = _qkv_proj(h, w_k, w_qvt)
        a = _moba_attention(q_t, k_aug, v_t, bias_per_dist)
        h = _mix_ln1(h, a, w_pg, b_gate[l][None, :], w_pool[l].astype(BF16),
                     pool_scale[l][None, :], w_o_attn[l].astype(BF16),
                     w_o_pool[l].astype(BF16), w_out[l].astype(BF16),
                     ln1_g[l][None, :], ln1_b[l][None, :])
        h = _ffn_ln2(h.reshape(b * s, d), w_ff1[l].astype(BF16), b_ff1[l][None, :],
                     w_ff2[l].astype(BF16), b_ff2[l][None, :],
                     ln2_g[l][None, :], ln2_b[l][None, :]).reshape(b, s, d)
    return h
```

```python
import math

import jax
import jax.numpy as jnp
from jax import lax
from jax.experimental import pallas as pl
from jax.experimental.pallas import tpu as pltpu

D_MODEL = 1024
N_HEADS = 8
HEAD_DIM = 64
WIDTH_A = N_HEADS * HEAD_DIM
MOBA_BLOCK = 256
MOBA_TOPK = 3
N_POOL_GROUPS = 4
POOL_WINDOWS = (2, 4, 8, 16)
WIDTH_B = 512
POOL_GROUP_DIM = WIDTH_B // N_POOL_GROUPS
D_FF = 4 * D_MODEL
REL_BUCKETS = 32
REL_MAX_DIST = 128
DEEPNORM_ALPHA = 2.0 ** 0.25
LN_EPS = 1e-5
NEG_INF = -1e30

LANES = 128
MAX_KEY_BLOCKS = 32
HEADS_PER_STEP = 8
V_ROWS = HEAD_DIM + 16
LOG2E = math.log2(math.e)
HALO = 16
MIX_CHUNK = 256
VMEM_LIMIT = 48 * 1024 * 1024

BF16 = jnp.bfloat16
F32 = jnp.float32
NT_DIMS = (((1,), (1,)), ((), ()))


def _layer_norm(y, g, b):
    mu = jnp.mean(y, axis=-1, keepdims=True)
    d = y - mu
    var = jnp.mean(d * d, axis=-1, keepdims=True)
    return d * lax.rsqrt(var + LN_EPS) * g + b


def _qkv_kernel(x_ref, wk_ref, wqvt_ref, q_ref, k_ref, v_ref, kmean_sc):
    si = pl.program_id(1)

    @pl.when(si == 0)
    def _():
        kmean_sc[...] = jnp.zeros(kmean_sc.shape, F32)

    xb = x_ref[0].astype(BF16)
    t = xb.shape[0]
    blocks_per_tile = t // MOBA_BLOCK
    acc_k =jnp.dot(xb, wk_ref[...], preferred_element_type=F32)
    acc_q = lax.dot_general(wqvt_ref[0:WIDTH_A, :], xb, NT_DIMS,
                            preferred_element_type=F32)
    lane = lax.broadcasted_iota(jnp.int32, (t, LANES), 1)
    row = lax.broadcasted_iota(jnp.int32, (t, LANES), 0)
    lo = lane < HEAD_DIM
    onehot = jnp.where(lane == HEAD_DIM + si * blocks_per_tile + row // MOBA_BLOCK, 1.0, 0.0)
    for hp in range(N_HEADS // 2):
        kp = acc_k[:, hp * LANES:(hp + 1) * LANES] * LOG2E
        kr = pltpu.roll(kp, HEAD_DIM, 1)
        k_ref[0, 2 * hp] = jnp.where(lo, kp, onehot).astype(BF16)
        k_ref[0, 2 * hp + 1] = jnp.where(lo, kr, onehot).astype(BF16)

    blk_row =lax.broadcasted_iota(jnp.int32, (MAX_KEY_BLOCKS, LANES), 0)
    lo_row = lax.broadcasted_iota(jnp.int32, (MAX_KEY_BLOCKS, LANES), 1) < HEAD_DIM
    for c in range(blocks_per_tile):
        mean_c = jnp.mean(acc_k[c * MOBA_BLOCK:(c + 1) * MOBA_BLOCK], axis=0, keepdims=True)
        for hp in range(N_HEADS // 2):
            pair = mean_c[:, hp * LANES:(hp + 1) * LANES]
            for h, m in ((2 * hp, pair), (2 * hp + 1, pltpu.roll(pair, HEAD_DIM, 1))):
                new_row = jnp.where(lo_row, jnp.broadcast_to(m, (MAX_KEY_BLOCKS, LANES)), 0.0)
                kmean_sc[h] = jnp.where(blk_row == si * blocks_per_tile + c, new_row, kmean_sc[h])

    sel_row = lax.broadcasted_iota(jnp.int32, (MAX_KEY_BLOCKS, t), 0)
    own_blk = si * blocks_per_tile + lax.broadcasted_iota(
        jnp.int32, (MAX_KEY_BLOCKS, t), 1) // MOBA_BLOCK
    past = sel_row < own_blk
    zeros64 = jnp.zeros((LANES - HEAD_DIM, t), BF16)
    zeros32 = jnp.zeros((LANES - HEAD_DIM - MAX_KEY_BLOCKS, t), BF16)
    pad1 = jnp.ones((V_ROWS - HEAD_DIM, t), BF16)
    for h in range(N_HEADS):
        qt = acc_q[h * HEAD_DIM:(h + 1) * HEAD_DIM].astype(BF16)
        gate = jnp.dot(kmean_sc[h].astype(BF16), jnp.concatenate([qt, zeros64], axis=0),
                       preferred_element_type=F32)
        g = jnp.where(past, gate, NEG_INF)
        for _ in range(MOBA_TOPK):
            mx = jnp.max(g, axis=0, keepdims=True)
            first = jnp.min(jnp.where(g == mx, sel_row, MAX_KEY_BLOCKS), axis=0, keepdims=True)
            g = jnp.where(sel_row == first, -jnp.inf, g)
        sel = ((g == -jnp.inf) & past) | (sel_row == own_blk)
        selbias = jnp.where(sel, 0.0, NEG_INF).astype(BF16)
        for c in range(blocks_per_tile):
            cs = slice(c * MOBA_BLOCK, (c + 1) * MOBA_BLOCK)
            q_ref[0, h, c, 0:HEAD_DIM, :] = qt[:, cs]
            q_ref[0, h, c, HEAD_DIM:HEAD_DIM + MAX_KEY_BLOCKS, :] = selbias[:, cs]
            q_ref[0, h, c, HEAD_DIM + MAX_KEY_BLOCKS:LANES, :] = zeros32[:, cs]

    acc_v = lax.dot_general(wqvt_ref[WIDTH_A:2 * WIDTH_A, :], xb, NT_DIMS,
                            preferred_element_type=F32)
    for h in range(N_HEADS):
        vt = acc_v[h * HEAD_DIM:(h + 1) * HEAD_DIM].astype(BF16)
        for c in range(blocks_per_tile):
            cs = slice(c * MOBA_BLOCK, (c + 1) * MOBA_BLOCK)
            v_ref[0, h, c, 0:HEAD_DIM, :] = vt[:, cs]
            v_ref[0, h, c, HEAD_DIM:V_ROWS, :] = pad1[:, cs]


def _qkv_proj(x, w_k, w_qvt, t=2 * MOBA_BLOCK):
    b, s, d = x.shape
    return pl.pallas_call(
        _qkv_kernel,
        grid=(b, s // t),
        in_specs=[
            pl.BlockSpec((1, t, d), lambda bi, si: (bi, si, 0)),
            pl.BlockSpec(w_k.shape, lambda bi, si: (0, 0)),
            pl.BlockSpec(w_qvt.shape, lambda bi, si: (0, 0)),
        ],
        out_specs=[
            pl.BlockSpec((1, N_HEADS, t // MOBA_BLOCK, LANES, MOBA_BLOCK),
                         lambda bi, si: (bi, 0, si, 0, 0)),
            pl.BlockSpec((1, N_HEADS, t, LANES), lambda bi, si: (bi, 0, si, 0)),
            pl.BlockSpec((1, N_HEADS, t // MOBA_BLOCK, V_ROWS, MOBA_BLOCK),
                         lambda bi, si: (bi, 0, si, 0, 0)),
        ],
        out_shape=[
            jax.ShapeDtypeStruct((b, N_HEADS, s // MOBA_BLOCK, LANES, MOBA_BLOCK), BF16),
            jax.ShapeDtypeStruct((b, N_HEADS, s, LANES), BF16),
            jax.ShapeDtypeStruct((b, N_HEADS, s // MOBA_BLOCK, V_ROWS, MOBA_BLOCK), BF16),
        ],
        scratch_shapes=[pltpu.VMEM((N_HEADS, MAX_KEY_BLOCKS, LANES), F32)],
        compiler_params=pltpu.CompilerParams(
            dimension_semantics=("arbitrary", "arbitrary"),
            vmem_limit_bytes=VMEM_LIMIT),
        name="qkv_proj",
    )(x, w_k, w_qvt)


def _attn_kernel(q_ref, k_ref, v_ref, pd_ref, o_ref, k_all, v_all, s_a, s_b, acc_sc, m_sc,
                 max_sc, bias_sc):
    i = pl.program_id(2)
    t = MOBA_BLOCK
    heads = range(HEADS_PER_STEP)

    @pl.when((pl.program_id(0) == 0) & (i == 0))
    def _():
        for h in heads:
            skew = pltpu.roll(jnp.broadcast_to(pd_ref[h], (t, 3 * t)), 0, 1,
                              stride=1, stride_axis=0)
            bias_sc[h, 0] = skew[:, 0:t]
            bias_sc[h, 1] = skew[:, t:2 * t]

    own = pl.ds(pl.multiple_of(i * t, t), t)
    for h in heads:
        k_all[h, own, :] = k_ref[0, h]
        v_all[h, :, own] = v_ref[0, h, 0]

    def key_block(h, j):
        return k_all[h, pl.ds(pl.multiple_of(j * t, t), t), :]

    def head_scores_into(s_dst, h, j):
        s = jnp.dot(key_block(h, j), q_ref[0, h, 0], preferred_element_type=F32)
        s_dst[h] = s
        return jnp.max(s, axis=0, keepdims=True)

    def weighted_values(h, j, p):
        vv = v_all[h, :, pl.ds(pl.multiple_of(j * t, t), t)]
        return jnp.dot(vv, p.astype(BF16), preferred_element_type=F32)

    def step(j, ms, s_cur, max_cur, s_nxt, bias_idx=None):
        new_ms, max_nxt = [], []
        for h in heads:
            if s_nxt is not None:
                max_nxt.append(head_scores_into(s_nxt, h, j + 1))
            s = s_cur[h]
            if bias_idx is not None:
                s = s + bias_sc[h, bias_idx]
                m_new = jnp.maximum(ms[h], jnp.max(s, axis=0, keepdims=True))
            else:
                m_new = jnp.maximum(ms[h], max_cur[h])
            alpha = jnp.exp2(ms[h] - m_new)
            acc_sc[h] = alpha * acc_sc[h] + weighted_values(h, j, jnp.exp2(s - m_new))
            new_ms.append(m_new)
        return new_ms, max_nxt

    ms = []
    for h in heads:
        s = jnp.dot(k_ref[0, h], q_ref[0, h, 0], preferred_element_type=F32) + bias_sc[h, 0]
        s_b[h] = s
        ms.append(jnp.max(s, axis=0, keepdims=True))

    max_a = []
    for h in heads:
        max_a.append(head_scores_into(s_a, h, 0))
        acc_sc[h] = jnp.dot(v_ref[0, h, 0], jnp.exp2(s_b[h] - ms[h]).astype(BF16),
                            preferred_element_type=F32)

    n_far = jnp.maximum(i - 1, 0)

    def two_steps(j, ms, max_a):
        ms, max_b = step(j, list(ms), s_a, list(max_a), s_b)
        return step(j + 1, ms, s_b, max_b, s_a)

    def four_steps(jj, carry):
        ms, max_a = two_steps(4 * jj, *carry)
        ms, max_a = two_steps(4 * jj + 2, ms, max_a)
        return tuple(ms), tuple(max_a)

    ms, max_a = lax.fori_loop(0, n_far // 4, four_steps, (tuple(ms), tuple(max_a)))
    for h in heads:
        m_sc[h] = ms[h]
        max_sc[h] = max_a[h]
    left = n_far % 4
    j_left = n_far - left

    def finish():
        for hp in range(HEADS_PER_STEP // 2):
            outs = []
            for h in (2 * hp, 2 * hp + 1):
                acc = acc_sc[h]
                outs.append(acc[0:HEAD_DIM] / acc[HEAD_DIM:HEAD_DIM + 1])
            o_ref[0, :, hp * LANES:(hp + 1) * LANES] = jnp.concatenate(
                outs, axis=0).T.astype(o_ref.dtype)

    @pl.when(i == 0)
    def _():
        finish()

    for n_left in range(4):
        @pl.when((i >= 1) & (left == n_left))
        def _(n_left=n_left):
            ms = [m_sc[h] for h in heads]
            maxes = [max_sc[h] for h in heads]
            bufs = (s_a, s_b)
            for k in range(n_left):
                ms, maxes = step(j_left + k, ms, bufs[k % 2], maxes, bufs[(k + 1) % 2])
            step(i - 1, ms, bufs[n_left % 2], None, None, bias_idx=1)
            finish()


def _moba_attention(q_t, k_aug, v_t, bias_per_dist):
    b, h, s, _ = k_aug.shape
    t = MOBA_BLOCK
    g = HEADS_PER_STEP
    assert g == h
    once = pl.Buffered(1)
    return pl.pallas_call(
        _attn_kernel,
        grid=(b, h // g, s // t),
        in_specs=[
            pl.BlockSpec((1, g, 1, LANES, t), lambda bi, hg, qi: (bi, hg, qi, 0, 0)),
            pl.BlockSpec((1, g, t, LANES), lambda bi, hg, qi: (bi, hg, qi, 0)),
            pl.BlockSpec((1, g, 1, V_ROWS, t), lambda bi, hg, qi: (bi, hg, qi, 0, 0)),
            pl.BlockSpec((g, 1, 3 * t), lambda bi, hg, qi: (hg, 0, 0), pipeline_mode=once),
        ],
        out_specs=pl.BlockSpec((1, t, g * HEAD_DIM), lambda bi, hg, qi: (bi, qi, hg)),
        out_shape=jax.ShapeDtypeStruct((b, s, WIDTH_A), BF16),
        scratch_shapes=[
            pltpu.VMEM((g, s, LANES), BF16),
            pltpu.VMEM((g, V_ROWS, s), BF16),
            pltpu.VMEM((g, t, t), F32),
            pltpu.VMEM((g, t, t), F32),
            pltpu.VMEM((g, V_ROWS, t), F32),
            pltpu.VMEM((g, 1, t), F32),
            pltpu.VMEM((g, 1, t), F32),
            pltpu.VMEM((g, 2, t, t), F32),
        ],
        compiler_params=pltpu.CompilerParams(
            dimension_semantics=("arbitrary", "arbitrary", "arbitrary"),
            vmem_limit_bytes=VMEM_LIMIT),
        name="moba_attn",
    )(q_t, k_aug, v_t, bias_per_dist)


def _rel_bucket(dist):
    n = jnp.maximum(dist, 0)
    max_exact = REL_BUCKETS // 2
    nf = jnp.maximum(n, 1).astype(F32)
    large = max_exact + (jnp.log(nf / max_exact) / math.log(REL_MAX_DIST / max_exact)
                         * (REL_BUCKETS - max_exact)).astype(jnp.int32)
    large = jnp.minimum(large, REL_BUCKETS - 1)
    return jnp.where(n < max_exact, n, large)


def _bias_per_distance(rel_table):
    t = MOBA_BLOCK
    period = 3 * t
    dist = jnp.arange(period)
    dist = jnp.where(dist < 2 * t, dist, dist - period)
    onehot = _rel_bucket(dist)[None, :] == jnp.arange(REL_BUCKETS)[:, None]
    far = rel_table[_rel_bucket(jnp.array(2 * t, jnp.int32))]
    shifted = (rel_table - far[None, :]).T
    per_dist = jnp.sum(jnp.where(onehot[None], shifted[:, :, None], 0.0), axis=1)
    per_dist = jnp.where(dist >= 0, per_dist * LOG2E, NEG_INF).astype(F32)
    return per_dist[:, None, :]


def _mix_kernel(x_ref, a_ref, wpg_ref, bg_ref, wpool_ref, pscale_ref, woa_ref, wob_ref,
                wout_ref, g_ref, b_ref, h_ref, pbuf):
    si = pl.program_id(1)
    t = x_ref.shape[1]

    @pl.when(si == 0)
    def _():
        pbuf[0:HALO, :] = jnp.zeros((HALO, WIDTH_B), F32)

    @pl.when(si > 0)
    def _():
        pbuf[0:HALO, :] = pbuf[t:t + HALO, :]

    pbuf[HALO:HALO + t, :] = jnp.dot(x_ref[0].astype(BF16), wpg_ref[:, :WIDTH_B],
                                     preferred_element_type=F32)
    rows = [slice(c * MIX_CHUNK, (c + 1) * MIX_CHUNK) for c in range(t // MIX_CHUNK)]
    gate_logits = [jnp.dot(x_ref[0, r, :].astype(BF16), wpg_ref[:, WIDTH_B:],
                           preferred_element_type=F32) for r in rows]
    for r, gl in zip(rows, gate_logits):
        n = r.stop - r.start
        pos = si * t + r.start + lax.broadcasted_iota(jnp.int32, (n, 1), 0)
        mixed = []
        for gi, w in enumerate(POOL_WINDOWS):
            cols = slice(gi * POOL_GROUP_DIM, (gi + 1) * POOL_GROUP_DIM)
            own = pbuf[HALO + r.start:HALO + r.stop, cols]
            ws = pbuf[r.start:HALO + r.stop, cols]
            span = 1
            while span < w:
                ws = ws + pltpu.roll(ws, span, 0)
                span *= 2
            ws = ws[HALO:]
            cnt = jnp.minimum(pos + 1, w).astype(F32)
            pooled = ws / cnt - own
            mixed.append(jnp.dot(pooled.astype(BF16), wpool_ref[gi],
                                 preferred_element_type=F32))
        mixed = jnp.concatenate(mixed, axis=1) * pscale_ref[...]
        branch_b = jnp.dot(mixed.astype(BF16), wob_ref[...], preferred_element_type=F32)
        branch_a = jnp.dot(a_ref[0, r, :], woa_ref[...], preferred_element_type=F32)
        gates = jax.nn.sigmoid(gl + bg_ref[...])
        mixin = gates[:, :D_MODEL] * branch_a + gates[:, D_MODEL:] * branch_b
        mix = jnp.dot(mixin.astype(BF16), wout_ref[...], preferred_element_type=F32)
        h_ref[0, r, :] = _layer_norm(DEEPNORM_ALPHA * x_ref[0, r, :] + mix,
                                     g_ref[...], b_ref[...])


def _mix_ln1(x, a, w_pg, b_gate, w_pool, pool_scale, w_oa, w_ob, w_out, ln_g, ln_b, t=1024):
    b, s, d = x.shape
    const2 = lambda bi, si: (0, 0)
    once = pl.Buffered(1)
    return pl.pallas_call(
        _mix_kernel,
        grid=(b, s // t),
        in_specs=[
            pl.BlockSpec((1, t, d), lambda bi, si: (bi, si, 0)),
            pl.BlockSpec((1, t, WIDTH_A), lambda bi, si: (bi, si, 0)),
            pl.BlockSpec(w_pg.shape, const2, pipeline_mode=once),
            pl.BlockSpec(b_gate.shape, const2),
            pl.BlockSpec(w_pool.shape, lambda bi, si: (0, 0, 0)),
            pl.BlockSpec(pool_scale.shape, const2),
            pl.BlockSpec(w_oa.shape, const2, pipeline_mode=once),
            pl.BlockSpec(w_ob.shape, const2, pipeline_mode=once),
            pl.BlockSpec(w_out.shape, const2, pipeline_mode=once),
            pl.BlockSpec(ln_g.shape, const2),
            pl.BlockSpec(ln_b.shape, const2),
        ],
        out_specs=pl.BlockSpec((1, t, d), lambda bi, si: (bi, si, 0)),
        out_shape=jax.ShapeDtypeStruct((b, s, d), F32),
        scratch_shapes=[pltpu.VMEM((HALO + t, WIDTH_B), F32)],
        compiler_params=pltpu.CompilerParams(
            dimension_semantics=("arbitrary", "arbitrary"),
            vmem_limit_bytes=VMEM_LIMIT),
        name="mix_ln1",
    )(x, a, w_pg, b_gate, w_pool, pool_scale, w_oa, w_ob, w_out, ln_g, ln_b)


FF_CHUNK = 1024
FF_ROWS = 256


def _ffn_kernel(h_ref, w1_ref, b1_ref, w2_ref, b2_ref, g_ref, b_ref, o_ref):
    for r0 in range(0, h_ref.shape[0], FF_ROWS):
        h = h_ref[r0:r0 + FF_ROWS, :]
        hb = h.astype(BF16)
        ff = jnp.zeros(h.shape, F32)
        for c in range(D_FF // FF_CHUNK):
            cs = slice(c * FF_CHUNK, (c + 1) * FF_CHUNK)
            u = jnp.dot(hb, w1_ref[:, cs], preferred_element_type=F32) + b1_ref[:, cs]
            u = jnp.square(jnp.maximum(u, 0.0))
            ff = ff + jnp.dot(u.astype(BF16), w2_ref[cs, :], preferred_element_type=F32)
        ff = ff + b2_ref[...]
        o_ref[r0:r0 + FF_ROWS, :] = _layer_norm(DEEPNORM_ALPHA * h + ff, g_ref[...], b_ref[...])


def _ffn_ln2(h, w1, b1, w2, b2, ln_g, ln_b, t=1024):
    n, d = h.shape
    const2 = lambda i: (0, 0)
    once = pl.Buffered(1)
    return pl.pallas_call(
        _ffn_kernel,
        grid=(n // t,),
        in_specs=[
            pl.BlockSpec((t, d), lambda i: (i, 0)),
            pl.BlockSpec(w1.shape, const2, pipeline_mode=once),
            pl.BlockSpec(b1.shape, const2),
            pl.BlockSpec(w2.shape, const2, pipeline_mode=once),
            pl.BlockSpec(b2.shape, const2),
            pl.BlockSpec(ln_g.shape, const2),
            pl.BlockSpec(ln_b.shape, const2),
        ],
        out_specs=pl.BlockSpec((t, d), lambda i: (i, 0)),
        out_shape=jax.ShapeDtypeStruct((n, d), F32),
        compiler_params=pltpu.CompilerParams(
            dimension_semantics=("arbitrary",),
            vmem_limit_bytes=VMEM_LIMIT),
        name="ffn_ln2",
    )(h, w1, b1, w2, b2, ln_g, ln_b)


def kernel(x, w_in, b_gate, rel_table, w_pool, pool_scale, w_o_attn, w_o_pool, w_out,
           ln1_g, ln1_b, w_ff1, b_ff1, w_ff2, b_ff2, ln2_g, ln2_b):
    b, s, d = x.shape
    nb = s // MOBA_BLOCK
    assert s % 1024 == 0 and nb <= MAX_KEY_BLOCKS
    bias_per_dist = _bias_per_distance(rel_table)
    h = x
    for l in range(w_in.shape[0]):
        w_q = w_in[l, :, :WIDTH_A] * (HEAD_DIM ** -0.5)
        w_k = w_in[l, :, WIDTH_A:2 * WIDTH_A].astype(BF16)
        w_v = w_in[l, :, 2 * WIDTH_A:3 * WIDTH_A]
        w_qvt = jnp.concatenate([w_q, w_v], axis=1).T.astype(BF16)
        w_pg = w_in[l, :, 3 * WIDTH_A:].astype(BF16)
        q_t, k_aug, v_t = _qkv_proj(h, w_k, w_qvt)
        a = _moba_attention(q_t, k_aug, v_t, bias_per_dist)
        h = _mix_ln1(h, a, w_pg, b_gate[l][None, :], w_pool[l].astype(BF16),
                     pool_scale[l][None, :], w_o_attn[l].astype(BF16),
                     w_o_pool[l].astype(BF16), w_out[l].astype(BF16),
                     ln1_g[l][None, :], ln1_b[l][None, :])
        h = _ffn_ln2(h.reshape(b * s, d), w_ff1[l].astype(BF16), b_ff1[l][None, :],
                     w_ff2[l].astype(BF16), b_ff2[l][None, :],
                     ln2_g[l][None, :], ln2_b[l][None, :]).reshape(b, s, d)
    return h
```

```python
import math

import jax
import jax.numpy as jnp
from jax import lax
from jax.experimental import pallas as pl
from jax.experimental.pallas import tpu as pltpu

D_MODEL = 1024
N_HEADS = 8
HEAD_DIM = 64
WIDTH_A = N_HEADS * HEAD_DIM
MOBA_BLOCK = 256
MOBA_TOPK = 3
N_POOL_GROUPS = 4
POOL_WINDOWS = (2, 4, 8, 16)
WIDTH_B = 512
POOL_GROUP_DIM = WIDTH_B // N_POOL_GROUPS
D_FF = 4 * D_MODEL
REL_BUCKETS = 32
REL_MAX_DIST = 128
DEEPNORM_ALPHA = 2.0 ** 0.25
LN_EPS = 1e-5
NEG_INF = -1e30

LANES = 128
MAX_KEY_BLOCKS = 32
HEADS_PER_STEP = 8
V_ROWS = HEAD_DIM + 16
LOG2E = math.log2(math.e)
HALO = 16
MIX_CHUNK = 256
VMEM_LIMIT = 48 * 1024 * 1024

BF16 = jnp.bfloat16
F32 = jnp.float32
NT_DIMS = (((1,), (1,)), ((), ()))


def _layer_norm(y, g, b):
    mu = jnp.mean(y, axis=-1, keepdims=True)
    d = y - mu
    var = jnp.mean(d * d, axis=-1, keepdims=True)
    return d * lax.rsqrt(var + LN_EPS) * g + b


def _qkv_kernel(x_ref, wk_ref, wqvt_ref, q_ref, k_ref, v_ref, kmean_sc):
    si = pl.program_id(1)

    @pl.when(si == 0)
    def _():
        kmean_sc[...] = jnp.zeros(kmean_sc.shape, F32)

    xb = x_ref[0].astype(BF16)
    t = xb.shape[0]
    blocks_per_tile = t // MOBA_BLOCK
    acc_k =jnp.dot(xb, wk_ref[...], preferred_element_type=F32)
    acc_q = lax.dot_general(wqvt_ref[0:WIDTH_A, :], xb, NT_DIMS,
                            preferred_element_type=F32)
    lane = lax.broadcasted_iota(jnp.int32, (t, LANES), 1)
    row = lax.broadcasted_iota(jnp.int32, (t, LANES), 0)
    lo = lane < HEAD_DIM
    onehot = jnp.where(lane == HEAD_DIM + si * blocks_per_tile + row // MOBA_BLOCK, 1.0, 0.0)
    for hp in range(N_HEADS // 2):
        kp = acc_k[:, hp * LANES:(hp + 1) * LANES] * LOG2E
        kr = pltpu.roll(kp, HEAD_DIM, 1)
        k_ref[0, 2 * hp] = jnp.where(lo, kp, onehot).astype(BF16)
        k_ref[0, 2 * hp + 1] = jnp.where(lo, kr, onehot).astype(BF16)

    blk_row =lax.broadcasted_iota(jnp.int32, (MAX_KEY_BLOCKS, LANES), 0)
    lo_row = lax.broadcasted_iota(jnp.int32, (MAX_KEY_BLOCKS, LANES), 1) < HEAD_DIM
    for c in range(blocks_per_tile):
        mean_c = jnp.mean(acc_k[c * MOBA_BLOCK:(c + 1) * MOBA_BLOCK], axis=0, keepdims=True)
        for hp in range(N_HEADS // 2):
            pair = mean_c[:, hp * LANES:(hp + 1) * LANES]
            for h, m in ((2 * hp, pair), (2 * hp + 1, pltpu.roll(pair, HEAD_DIM, 1))):
                new_row = jnp.where(lo_row, jnp.broadcast_to(m, (MAX_KEY_BLOCKS, LANES)), 0.0)
                kmean_sc[h] = jnp.where(blk_row == si * blocks_per_tile + c, new_row, kmean_sc[h])

    sel_row = lax.broadcasted_iota(jnp.int32, (MAX_KEY_BLOCKS, t), 0)
    own_blk = si * blocks_per_tile + lax.broadcasted_iota(
        jnp.int32, (MAX_KEY_BLOCKS, t), 1) // MOBA_BLOCK
    past = sel_row < own_blk
    zeros64 = jnp.zeros((LANES - HEAD_DIM, t), BF16)
    zeros32 = jnp.zeros((LANES - HEAD_DIM - MAX_KEY_BLOCKS, t), BF16)
    pad1 = jnp.ones((V_ROWS - HEAD_DIM, t), BF16)
    for h in range(N_HEADS):
        qt = acc_q[h * HEAD_DIM:(h + 1) * HEAD_DIM].astype(BF16)
        gate = jnp.dot(kmean_sc[h].astype(BF16), jnp.concatenate([qt, zeros64], axis=0),
                       preferred_element_type=F32)
        g = jnp.where(past, gate, NEG_INF)
        for _ in range(MOBA_TOPK):
            mx = jnp.max(g, axis=0, keepdims=True)
            first = jnp.min(jnp.where(g == mx, sel_row, MAX_KEY_BLOCKS), axis=0, keepdims=True)
            g = jnp.where(sel_row == first, -jnp.inf, g)
        sel = ((g == -jnp.inf) & past) | (sel_row == own_blk)
        selbias = jnp.where(sel, 0.0, NEG_INF).astype(BF16)
        for c in range(blocks_per_tile):
            cs = slice(c * MOBA_BLOCK, (c + 1) * MOBA_BLOCK)
            q_ref[0, h, c, 0:HEAD_DIM, :] = qt[:, cs]
            q_ref[0, h, c, HEAD_DIM:HEAD_DIM + MAX_KEY_BLOCKS, :] = selbias[:, cs]
            q_ref[0, h, c, HEAD_DIM + MAX_KEY_BLOCKS:LANES, :] = zeros32[:, cs]

    acc_v = lax.dot_general(wqvt_ref[WIDTH_A:2 * WIDTH_A, :], xb, NT_DIMS,
                            preferred_element_type=F32)
    for h in range(N_HEADS):
        vt = acc_v[h * HEAD_DIM:(h + 1) * HEAD_DIM].astype(BF16)
        for c in range(blocks_per_tile):
            cs = slice(c * MOBA_BLOCK, (c + 1) * MOBA_BLOCK)
            v_ref[0, h, c, 0:HEAD_DIM, :] = vt[:, cs]
            v_ref[0, h, c, HEAD_DIM:V_ROWS, :] = pad1[:, cs]


def _qkv_proj(x, w_k, w_qvt, t=2 * MOBA_BLOCK):
    b, s, d = x.shape
    return pl.pallas_call(
        _qkv_kernel,
        grid=(b, s // t),
        in_specs=[
            pl.BlockSpec((1, t, d), lambda bi, si: (bi, si, 0)),
            pl.BlockSpec(w_k.shape, lambda bi, si: (0, 0)),
            pl.BlockSpec(w_qvt.shape, lambda bi, si: (0, 0)),
        ],
        out_specs=[
            pl.BlockSpec((1, N_HEADS, t // MOBA_BLOCK, LANES, MOBA_BLOCK),
                         lambda bi, si: (bi, 0, si, 0, 0)),
            pl.BlockSpec((1, N_HEADS, t, LANES), lambda bi, si: (bi, 0, si, 0)),
            pl.BlockSpec((1, N_HEADS, t // MOBA_BLOCK, V_ROWS, MOBA_BLOCK),
                         lambda bi, si: (bi, 0, si, 0, 0)),
        ],
        out_shape=[
            jax.ShapeDtypeStruct((b, N_HEADS, s // MOBA_BLOCK, LANES, MOBA_BLOCK), BF16),
            jax.ShapeDtypeStruct((b, N_HEADS, s, LANES), BF16),
            jax.ShapeDtypeStruct((b, N_HEADS, s // MOBA_BLOCK, V_ROWS, MOBA_BLOCK), BF16),
        ],
        scratch_shapes=[pltpu.VMEM((N_HEADS, MAX_KEY_BLOCKS, LANES), F32)],
        compiler_params=pltpu.CompilerParams(
            dimension_semantics=("arbitrary", "arbitrary"),
            vmem_limit_bytes=VMEM_LIMIT),
        name="qkv_proj",
    )(x, w_k, w_qvt)


def _attn_kernel(q_ref, k_ref, v_ref, pd_ref, o_ref, k_all, v_all, s_a, s_b, acc_sc, m_sc,
                 max_sc, bias_sc):
    i = pl.program_id(2)
    t = MOBA_BLOCK
    heads = range(HEADS_PER_STEP)

    @pl.when((pl.program_id(0) == 0) & (i == 0))
    def _():
        for h in heads:
            skew = pltpu.roll(jnp.broadcast_to(pd_ref[h], (t, 3 * t)), 0, 1,
                              stride=1, stride_axis=0)
            bias_sc[h, 0] = skew[:, 0:t]
            bias_sc[h, 1] = skew[:, t:2 * t]

    own = pl.ds(pl.multiple_of(i * t, t), t)
    for h in heads:
        k_all[h, own, :] = k_ref[0, h]
        v_all[h, :, own] = v_ref[0, h, 0]

    def key_block(h, j):
        return k_all[h, pl.ds(pl.multiple_of(j * t, t), t), :]

    def head_scores_into(s_dst, h, j):
        s = jnp.dot(key_block(h, j), q_ref[0, h, 0], preferred_element_type=F32)
        s_dst[h] = s
        return jnp.max(s, axis=0, keepdims=True)

    def weighted_values(h, j, p):
        vv = v_all[h, :, pl.ds(pl.multiple_of(j * t, t), t)]
        return jnp.dot(vv, p.astype(BF16), preferred_element_type=F32)

    def step(j, ms, s_cur, max_cur, s_nxt, bias_idx=None):
        new_ms, max_nxt = [], []
        for h in heads:
            if s_nxt is not None:
                max_nxt.append(head_scores_into(s_nxt, h, j + 1))
            s = s_cur[h]
            if bias_idx is not None:
                s = s + bias_sc[h, bias_idx]
                m_new = jnp.maximum(ms[h], jnp.max(s, axis=0, keepdims=True))
            else:
                m_new = jnp.maximum(ms[h], max_cur[h])
            alpha = jnp.exp2(ms[h] - m_new)
            acc_sc[h] = alpha * acc_sc[h] + weighted_values(h, j, jnp.exp2(s - m_new))
            new_ms.append(m_new)
        return new_ms, max_nxt

    ms = []
    for h in heads:
        s = jnp.dot(k_ref[0, h], q_ref[0, h, 0], preferred_element_type=F32) + bias_sc[h, 0]
        s_b[h] = s
        ms.append(jnp.max(s, axis=0, keepdims=True))

    max_a = []
    for h in heads:
        max_a.append(head_scores_into(s_a, h, 0))
        acc_sc[h] = jnp.dot(v_ref[0, h, 0], jnp.exp2(s_b[h] - ms[h]).astype(BF16),
                            preferred_element_type=F32)

    n_far = jnp.maximum(i - 1, 0)

    def two_steps(j, ms, max_a):
        ms, max_b = step(j, list(ms), s_a, list(max_a), s_b)
        return step(j + 1, ms, s_b, max_b, s_a)

    def four_steps(j, ms, max_a):
        ms, max_a = two_steps(j, ms, max_a)
        return two_steps(j + 2, ms, max_a)

    def eight_steps(jj, carry):
        ms, max_a = four_steps(8 * jj, *carry)
        ms, max_a = four_steps(8 * jj + 4, ms, max_a)
        return tuple(ms), tuple(max_a)

    def save_state(ms, max_a):
        for h in heads:
            m_sc[h] = ms[h]
            max_sc[h] = max_a[h]

    ms, max_a = lax.fori_loop(0, n_far // 8, eight_steps, (tuple(ms), tuple(max_a)))
    save_state(ms, max_a)

    @pl.when(n_far % 8 >= 4)
    def _():
        save_state(*four_steps(n_far - n_far % 8, [m_sc[h] for h in heads],
                               [max_sc[h] for h in heads]))

    left = n_far % 4
    j_left = n_far - left

    def finish():
        for hp in range(HEADS_PER_STEP // 2):
            outs = []
            for h in (2 * hp, 2 * hp + 1):
                acc = acc_sc[h]
                outs.append(acc[0:HEAD_DIM] / acc[HEAD_DIM:HEAD_DIM + 1])
            o_ref[0, :, hp * LANES:(hp + 1) * LANES] = jnp.concatenate(
                outs, axis=0).T.astype(o_ref.dtype)

    @pl.when(i == 0)
    def _():
        finish()

    for n_left in range(4):
        @pl.when((i >= 1) & (left == n_left))
        def _(n_left=n_left):
            ms = [m_sc[h] for h in heads]
            maxes = [max_sc[h] for h in heads]
            bufs = (s_a, s_b)
            for k in range(n_left):
                ms, maxes = step(j_left + k, ms, bufs[k % 2], maxes, bufs[(k + 1) % 2])
            step(i - 1, ms, bufs[n_left % 2], None, None, bias_idx=1)
            finish()


def _moba_attention(q_t, k_aug, v_t, bias_per_dist):
    b, h, s, _ = k_aug.shape
    t = MOBA_BLOCK
    g = HEADS_PER_STEP
    assert g == h
    once = pl.Buffered(1)
    return pl.pallas_call(
        _attn_kernel,
        grid=(b, h // g, s // t),
        in_specs=[
            pl.BlockSpec((1, g, 1, LANES, t), lambda bi, hg, qi: (bi, hg, qi, 0, 0)),
            pl.BlockSpec((1, g, t, LANES), lambda bi, hg, qi: (bi, hg, qi, 0)),
            pl.BlockSpec((1, g, 1, V_ROWS, t), lambda bi, hg, qi: (bi, hg, qi, 0, 0)),
            pl.BlockSpec((g, 1, 3 * t), lambda bi, hg, qi: (hg, 0, 0), pipeline_mode=once),
        ],
        out_specs=pl.BlockSpec((1, t, g * HEAD_DIM), lambda bi, hg, qi: (bi, qi, hg)),
        out_shape=jax.ShapeDtypeStruct((b, s, WIDTH_A), BF16),
        scratch_shapes=[
            pltpu.VMEM((g, s, LANES), BF16),
            pltpu.VMEM((g, V_ROWS, s), BF16),
            pltpu.VMEM((g, t, t), F32),
            pltpu.VMEM((g, t, t), F32),
            pltpu.VMEM((g, V_ROWS, t), F32),
            pltpu.VMEM((g, 1, t), F32),
            pltpu.VMEM((g, 1, t), F32),
            pltpu.VMEM((g, 2, t, t), F32),
        ],
        compiler_params=pltpu.CompilerParams(
            dimension_semantics=("arbitrary", "arbitrary", "arbitrary"),
            vmem_limit_bytes=VMEM_LIMIT),
        name="moba_attn",
    )(q_t, k_aug, v_t, bias_per_dist)


def _rel_bucket(dist):
    n = jnp.maximum(dist, 0)
    max_exact = REL_BUCKETS // 2
    nf = jnp.maximum(n, 1).astype(F32)
    large = max_exact + (jnp.log(nf / max_exact) / math.log(REL_MAX_DIST / max_exact)
                         * (REL_BUCKETS - max_exact)).astype(jnp.int32)
    large = jnp.minimum(large, REL_BUCKETS - 1)
    return jnp.where(n < max_exact, n, large)


def _bias_per_distance(rel_table):
    t = MOBA_BLOCK
    period = 3 * t
    dist = jnp.arange(period)
    dist = jnp.where(dist < 2 * t, dist, dist - period)
    onehot = _rel_bucket(dist)[None, :] == jnp.arange(REL_BUCKETS)[:, None]
    far = rel_table[_rel_bucket(jnp.array(2 * t, jnp.int32))]
    shifted = (rel_table - far[None, :]).T
    per_dist = jnp.sum(jnp.where(onehot[None], shifted[:, :, None], 0.0), axis=1)
    per_dist = jnp.where(dist >= 0, per_dist * LOG2E, NEG_INF).astype(F32)
    return per_dist[:, None, :]


def _mix_kernel(x_ref, a_ref, wpg_ref, bg_ref, wpool_ref, pscale_ref, woa_ref, wob_ref,
                wout_ref, g_ref, b_ref, h_ref, pbuf):
    si = pl.program_id(1)
    t = x_ref.shape[1]

    @pl.when(si == 0)
    def _():
        pbuf[0:HALO, :] = jnp.zeros((HALO, WIDTH_B), F32)

    @pl.when(si > 0)
    def _():
        pbuf[0:HALO, :] = pbuf[t:t + HALO, :]

    pbuf[HALO:HALO + t, :] = jnp.dot(x_ref[0].astype(BF16), wpg_ref[:, :WIDTH_B],
                                     preferred_element_type=F32)
    rows = [slice(c * MIX_CHUNK, (c + 1) * MIX_CHUNK) for c in range(t // MIX_CHUNK)]
    gate_logits = [jnp.dot(x_ref[0, r, :].astype(BF16), wpg_ref[:, WIDTH_B:],
                           preferred_element_type=F32) for r in rows]
    for r, gl in zip(rows, gate_logits):
        n = r.stop - r.start
        pos = si * t + r.start + lax.broadcasted_iota(jnp.int32, (n, 1), 0)
        mixed = []
        for gi, w in enumerate(POOL_WINDOWS):
            cols = slice(gi * POOL_GROUP_DIM, (gi + 1) * POOL_GROUP_DIM)
            own = pbuf[HALO + r.start:HALO + r.stop, cols]
            ws = pbuf[r.start:HALO + r.stop, cols]
            span = 1
            while span < w:
                ws = ws + pltpu.roll(ws, span, 0)
                span *= 2
            ws = ws[HALO:]
            cnt = jnp.minimum(pos + 1, w).astype(F32)
            pooled = ws / cnt - own
            mixed.append(jnp.dot(pooled.astype(BF16), wpool_ref[gi],
                                 preferred_element_type=F32))
        mixed = jnp.concatenate(mixed, axis=1) * pscale_ref[...]
        branch_b = jnp.dot(mixed.astype(BF16), wob_ref[...], preferred_element_type=F32)
        branch_a = jnp.dot(a_ref[0, r, :], woa_ref[...], preferred_element_type=F32)
        gates = jax.nn.sigmoid(gl + bg_ref[...])
        mixin = gates[:, :D_MODEL] * branch_a + gates[:, D_MODEL:] * branch_b
        mix = jnp.dot(mixin.astype(BF16), wout_ref[...], preferred_element_type=F32)
        h_ref[0, r, :] = _layer_norm(DEEPNORM_ALPHA * x_ref[0, r, :] + mix,
                                     g_ref[...], b_ref[...])


def _mix_ln1(x, a, w_pg, b_gate, w_pool, pool_scale, w_oa, w_ob, w_out, ln_g, ln_b, t=1024):
    b, s, d = x.shape
    const2 = lambda bi, si: (0, 0)
    once = pl.Buffered(1)
    return pl.pallas_call(
        _mix_kernel,
        grid=(b, s // t),
        in_specs=[
            pl.BlockSpec((1, t, d), lambda bi, si: (bi, si, 0)),
            pl.BlockSpec((1, t, WIDTH_A), lambda bi, si: (bi, si, 0)),
            pl.BlockSpec(w_pg.shape, const2, pipeline_mode=once),
            pl.BlockSpec(b_gate.shape, const2),
            pl.BlockSpec(w_pool.shape, lambda bi, si: (0, 0, 0)),
            pl.BlockSpec(pool_scale.shape, const2),
            pl.BlockSpec(w_oa.shape, const2, pipeline_mode=once),
            pl.BlockSpec(w_ob.shape, const2, pipeline_mode=once),
            pl.BlockSpec(w_out.shape, const2, pipeline_mode=once),
            pl.BlockSpec(ln_g.shape, const2),
            pl.BlockSpec(ln_b.shape, const2),
        ],
        out_specs=pl.BlockSpec((1, t, d), lambda bi, si: (bi, si, 0)),
        out_shape=jax.ShapeDtypeStruct((b, s, d), F32),
        scratch_shapes=[pltpu.VMEM((HALO + t, WIDTH_B), F32)],
        compiler_params=pltpu.CompilerParams(
            dimension_semantics=("arbitrary", "arbitrary"),
            vmem_limit_bytes=VMEM_LIMIT),
        name="mix_ln1",
    )(x, a, w_pg, b_gate, w_pool, pool_scale, w_oa, w_ob, w_out, ln_g, ln_b)


FF_CHUNK = 1024
FF_ROWS = 256


def _ffn_kernel(h_ref, w1_ref, b1_ref, w2_ref, b2_ref, g_ref, b_ref, o_ref):
    for r0 in range(0, h_ref.shape[0], FF_ROWS):
        h = h_ref[r0:r0 + FF_ROWS, :]
        hb = h.astype(BF16)
        ff = jnp.zeros(h.shape, F32)
        for c in range(D_FF // FF_CHUNK):
            cs = slice(c * FF_CHUNK, (c + 1) * FF_CHUNK)
            u = jnp.dot(hb, w1_ref[:, cs], preferred_element_type=F32) + b1_ref[:, cs]
            u = jnp.square(jnp.maximum(u, 0.0))
            ff = ff + jnp.dot(u.astype(BF16), w2_ref[cs, :], preferred_element_type=F32)
        ff = ff + b2_ref[...]
        o_ref[r0:r0 + FF_ROWS, :] = _layer_norm(DEEPNORM_ALPHA * h + ff, g_ref[...], b_ref[...])


def _ffn_ln2(h, w1, b1, w2, b2, ln_g, ln_b, t=1024):
    n, d = h.shape
    const2 = lambda i: (0, 0)
    once = pl.Buffered(1)
    return pl.pallas_call(
        _ffn_kernel,
        grid=(n // t,),
        in_specs=[
            pl.BlockSpec((t, d), lambda i: (i, 0)),
            pl.BlockSpec(w1.shape, const2, pipeline_mode=once),
            pl.BlockSpec(b1.shape, const2),
            pl.BlockSpec(w2.shape, const2, pipeline_mode=once),
            pl.BlockSpec(b2.shape, const2),
            pl.BlockSpec(ln_g.shape, const2),
            pl.BlockSpec(ln_b.shape, const2),
        ],
        out_specs=pl.BlockSpec((t, d), lambda i: (i, 0)),
        out_shape=jax.ShapeDtypeStruct((n, d), F32),
        compiler_params=pltpu.CompilerParams(
            dimension_semantics=("arbitrary",),
            vmem_limit_bytes=VMEM_LIMIT),
        name="ffn_ln2",
    )(h, w1, b1, w2, b2, ln_g, ln_b)


def kernel(x, w_in, b_gate, rel_table, w_pool, pool_scale, w_o_attn, w_o_pool, w_out,
           ln1_g, ln1_b, w_ff1, b_ff1, w_ff2, b_ff2, ln2_g, ln2_b):
    b, s, d = x.shape
    nb = s // MOBA_BLOCK
    assert s % 1024 == 0 and nb <= MAX_KEY_BLOCKS
    bias_per_dist = _bias_per_distance(rel_table)
    h = x
    for l in range(w_in.shape[0]):
        w_q = w_in[l, :, :WIDTH_A] * (HEAD_DIM ** -0.5)
        w_k = w_in[l, :, WIDTH_A:2 * WIDTH_A].astype(BF16)
        w_v = w_in[l, :, 2 * WIDTH_A:3 * WIDTH_A]
        w_qvt = jnp.concatenate([w_q, w_v], axis=1).T.astype(BF16)
        w_pg = w_in[l, :, 3 * WIDTH_A:].astype(BF16)
        q_t, k_aug, v_t = _qkv_proj(h, w_k, w_qvt)
        a = _moba_attention(q_t, k_aug, v_t, bias_per_dist)
        h = _mix_ln1(h, a, w_pg, b_gate[l][None, :], w_pool[l].astype(BF16),
                     pool_scale[l][None, :], w_o_attn[l].astype(BF16),
                     w_o_pool[l].astype(BF16), w_out[l].astype(BF16),
                     ln1_g[l][None, :], ln1_b[l][None, :])
        h = _ffn_ln2(h.reshape(b * s, d), w_ff1[l].astype(BF16), b_ff1[l][None, :],
                     w_ff2[l].astype(BF16), b_ff2[l][None, :],
                     ln2_g[l][None, :], ln2_b[l][None, :]).reshape(b, s, d)
    return h
```

```python
import math

import jax
import jax.numpy as jnp
from jax import lax
from jax.experimental import pallas as pl
from jax.experimental.pallas import tpu as pltpu

D_MODEL = 1024
N_HEADS = 8
HEAD_DIM = 64
WIDTH_A = N_HEADS * HEAD_DIM
MOBA_BLOCK = 256
MOBA_TOPK = 3
N_POOL_GROUPS = 4
POOL_WINDOWS = (2, 4, 8, 16)
WIDTH_B = 512
POOL_GROUP_DIM = WIDTH_B // N_POOL_GROUPS
D_FF = 4 * D_MODEL
REL_BUCKETS = 32
REL_MAX_DIST = 128
DEEPNORM_ALPHA = 2.0 ** 0.25
LN_EPS = 1e-5
NEG_INF = -1e30

LANES = 128
MAX_KEY_BLOCKS = 32
HEADS_PER_STEP = 8
V_ROWS = HEAD_DIM + 16
LOG2E = math.log2(math.e)
HALO = 16
MIX_CHUNK = 256
VMEM_LIMIT = 48 * 1024 * 1024

BF16 = jnp.bfloat16
F32 = jnp.float32
NT_DIMS = (((1,), (1,)), ((), ()))


def _layer_norm(y, g, b):
    mu = jnp.mean(y, axis=-1, keepdims=True)
    d = y - mu
    var = jnp.mean(d * d, axis=-1, keepdims=True)
    return d * lax.rsqrt(var + LN_EPS) * g + b


def _qkv_kernel(x_ref, wk_ref, wqvt_ref, q_ref, k_ref, v_ref, kmean_sc):
    si = pl.program_id(1)

    @pl.when(si == 0)
    def _():
        kmean_sc[...] = jnp.zeros(kmean_sc.shape, F32)

    xb = x_ref[0].astype(BF16)
    t = xb.shape[0]
    blocks_per_tile = t // MOBA_BLOCK
    acc_k =jnp.dot(xb, wk_ref[...], preferred_element_type=F32)
    acc_q = lax.dot_general(wqvt_ref[0:WIDTH_A, :], xb, NT_DIMS,
                            preferred_element_type=F32)
    lane = lax.broadcasted_iota(jnp.int32, (t, LANES), 1)
    row = lax.broadcasted_iota(jnp.int32, (t, LANES), 0)
    lo = lane < HEAD_DIM
    onehot = jnp.where(lane == HEAD_DIM + si * blocks_per_tile + row // MOBA_BLOCK, 1.0, 0.0)
    for hp in range(N_HEADS // 2):
        kp = acc_k[:, hp * LANES:(hp + 1) * LANES] * LOG2E
        kr = pltpu.roll(kp, HEAD_DIM, 1)
        k_ref[0, 2 * hp] = jnp.where(lo, kp, onehot).astype(BF16)
        k_ref[0, 2 * hp + 1] = jnp.where(lo, kr, onehot).astype(BF16)

    blk_row =lax.broadcasted_iota(jnp.int32, (MAX_KEY_BLOCKS, LANES), 0)
    lo_row = lax.broadcasted_iota(jnp.int32, (MAX_KEY_BLOCKS, LANES), 1) < HEAD_DIM
    for c in range(blocks_per_tile):
        mean_c = jnp.mean(acc_k[c * MOBA_BLOCK:(c + 1) * MOBA_BLOCK], axis=0, keepdims=True)
        for hp in range(N_HEADS // 2):
            pair = mean_c[:, hp * LANES:(hp + 1) * LANES]
            for h, m in ((2 * hp, pair), (2 * hp + 1, pltpu.roll(pair, HEAD_DIM, 1))):
                new_row = jnp.where(lo_row, jnp.broadcast_to(m, (MAX_KEY_BLOCKS, LANES)), 0.0)
                kmean_sc[h] = jnp.where(blk_row == si * blocks_per_tile + c, new_row, kmean_sc[h])

    sel_row = lax.broadcasted_iota(jnp.int32, (MAX_KEY_BLOCKS, t), 0)
    own_blk = si * blocks_per_tile + lax.broadcasted_iota(
        jnp.int32, (MAX_KEY_BLOCKS, t), 1) // MOBA_BLOCK
    past = sel_row < own_blk
    zeros64 = jnp.zeros((LANES - HEAD_DIM, t), BF16)
    zeros32 = jnp.zeros((LANES - HEAD_DIM - MAX_KEY_BLOCKS, t), BF16)
    pad1 = jnp.ones((V_ROWS - HEAD_DIM, t), BF16)
    for h in range(N_HEADS):
        qt = acc_q[h * HEAD_DIM:(h + 1) * HEAD_DIM].astype(BF16)
        gate = jnp.dot(kmean_sc[h].astype(BF16), jnp.concatenate([qt, zeros64], axis=0),
                       preferred_element_type=F32)
        g = jnp.where(past, gate, NEG_INF)
        for _ in range(MOBA_TOPK):
            mx = jnp.max(g, axis=0, keepdims=True)
            first = jnp.min(jnp.where(g == mx, sel_row, MAX_KEY_BLOCKS), axis=0, keepdims=True)
            g = jnp.where(sel_row == first, -jnp.inf, g)
        sel = ((g == -jnp.inf) & past) | (sel_row == own_blk)
        selbias = jnp.where(sel, 0.0, NEG_INF).astype(BF16)
        for c in range(blocks_per_tile):
            cs = slice(c * MOBA_BLOCK, (c + 1) * MOBA_BLOCK)
            q_ref[0, h, c, 0:HEAD_DIM, :] = qt[:, cs]
            q_ref[0, h, c, HEAD_DIM:HEAD_DIM + MAX_KEY_BLOCKS, :] = selbias[:, cs]
            q_ref[0, h, c, HEAD_DIM + MAX_KEY_BLOCKS:LANES, :] = zeros32[:, cs]

    acc_v = lax.dot_general(wqvt_ref[WIDTH_A:2 * WIDTH_A, :], xb, NT_DIMS,
                            preferred_element_type=F32)
    for h in range(N_HEADS):
        vt = acc_v[h * HEAD_DIM:(h + 1) * HEAD_DIM].astype(BF16)
        for c in range(blocks_per_tile):
            cs = slice(c * MOBA_BLOCK, (c + 1) * MOBA_BLOCK)
            v_ref[0, h, c, 0:HEAD_DIM, :] = vt[:, cs]
            v_ref[0, h, c, HEAD_DIM:V_ROWS, :] = pad1[:, cs]


def _qkv_proj(x, w_k, w_qvt, t=2 * MOBA_BLOCK):
    b, s, d = x.shape
    return pl.pallas_call(
        _qkv_kernel,
        grid=(b, s // t),
        in_specs=[
            pl.BlockSpec((1, t, d), lambda bi, si: (bi, si, 0)),
            pl.BlockSpec(w_k.shape, lambda bi, si: (0, 0)),
            pl.BlockSpec(w_qvt.shape, lambda bi, si: (0, 0)),
        ],
        out_specs=[
            pl.BlockSpec((1, N_HEADS, t // MOBA_BLOCK, LANES, MOBA_BLOCK),
                         lambda bi, si: (bi, 0, si, 0, 0)),
            pl.BlockSpec((1, N_HEADS, t, LANES), lambda bi, si: (bi, 0, si, 0)),
            pl.BlockSpec((1, N_HEADS, t // MOBA_BLOCK, V_ROWS, MOBA_BLOCK),
                         lambda bi, si: (bi, 0, si, 0, 0)),
        ],
        out_shape=[
            jax.ShapeDtypeStruct((b, N_HEADS, s // MOBA_BLOCK, LANES, MOBA_BLOCK), BF16),
            jax.ShapeDtypeStruct((b, N_HEADS, s, LANES), BF16),
            jax.ShapeDtypeStruct((b, N_HEADS, s // MOBA_BLOCK, V_ROWS, MOBA_BLOCK), BF16),
        ],
        scratch_shapes=[pltpu.VMEM((N_HEADS, MAX_KEY_BLOCKS, LANES), F32)],
        compiler_params=pltpu.CompilerParams(
            dimension_semantics=("arbitrary", "arbitrary"),
            vmem_limit_bytes=VMEM_LIMIT),
        name="qkv_proj",
    )(x, w_k, w_qvt)


def _attn_kernel(q_ref, k_ref, v_ref, pd_ref, o_ref, k_all, v_all, s_a, s_b, acc_sc, m_sc,
                 max_sc, bias_sc):
    i = pl.program_id(2)
    t = MOBA_BLOCK
    heads = range(HEADS_PER_STEP)

    @pl.when((pl.program_id(0) == 0) & (i == 0))
    def _():
        for h in heads:
            skew = pltpu.roll(jnp.broadcast_to(pd_ref[h], (t, 3 * t)), 0, 1,
                              stride=1, stride_axis=0)
            bias_sc[h, 0] = skew[:, 0:t]
            bias_sc[h, 1] = skew[:, t:2 * t]

    own = pl.ds(pl.multiple_of(i * t, t), t)
    for h in heads:
        k_all[h, own, :] = k_ref[0, h]
        v_all[h, :, own] = v_ref[0, h, 0]

    def key_block(h, j):
        return k_all[h, pl.ds(pl.multiple_of(j * t, t), t), :]

    def head_scores_into(s_dst, h, j):
        s = jnp.dot(key_block(h, j), q_ref[0, h, 0], preferred_element_type=F32)
        s_dst[h] = s
        return jnp.max(s, axis=0, keepdims=True)

    def weighted_values(h, j, p):
        vv = v_all[h, :, pl.ds(pl.multiple_of(j * t, t), t)]
        return jnp.dot(vv, p.astype(BF16), preferred_element_type=F32)

    def step(j, ms, s_cur, max_cur, s_nxt, bias_idx=None):
        new_ms, max_nxt = [], []
        for h in heads:
            if s_nxt is not None:
                max_nxt.append(head_scores_into(s_nxt, h, j + 1))
            s = s_cur[h]
            if bias_idx is not None:
                s = s + bias_sc[h, bias_idx]
                m_new = jnp.maximum(ms[h], jnp.max(s, axis=0, keepdims=True))
            else:
                m_new = jnp.maximum(ms[h], max_cur[h])
            alpha = jnp.exp2(ms[h] - m_new)
            acc_sc[h] = alpha * acc_sc[h] + weighted_values(h, j, jnp.exp2(s - m_new))
            new_ms.append(m_new)
        return new_ms, max_nxt

    ms = []
    for h in heads:
        s = jnp.dot(k_ref[0, h], q_ref[0, h, 0], preferred_element_type=F32) + bias_sc[h, 0]
        s_b[h] = s
        ms.append(jnp.max(s, axis=0, keepdims=True))

    max_a = []
    for h in heads:
        max_a.append(head_scores_into(s_a, h, 0))
        acc_sc[h] = jnp.dot(v_ref[0, h, 0], jnp.exp2(s_b[h] - ms[h]).astype(BF16),
                            preferred_element_type=F32)

    n_far = jnp.maximum(i - 1, 0)

    def two_steps(j, ms, max_a):
        ms, max_b = step(j, list(ms), s_a, list(max_a), s_b)
        return step(j + 1, ms, s_b, max_b, s_a)

    def four_steps(j, ms, max_a):
        ms, max_a = two_steps(j, ms, max_a)
        return two_steps(j + 2, ms, max_a)

    def save_state(ms, max_a):
        for h in heads:
            m_sc[h] = ms[h]
            max_sc[h] = max_a[h]

    save_state(ms, max_a)
    for size in (16, 8, 4):
        @pl.when(n_far % (2 * size) >= size)
        def _(size=size):
            ms = [m_sc[h] for h in heads]
            maxes = [max_sc[h] for h in heads]
            j0 = n_far - n_far % (2 * size)
            for k in range(0, size, 4):
                ms, maxes = four_steps(j0 + k, ms, maxes)
            save_state(ms, maxes)

    left = n_far % 4
    j_left = n_far - left

    def finish():
        for hp in range(HEADS_PER_STEP // 2):
            outs = []
            for h in (2 * hp, 2 * hp + 1):
                acc = acc_sc[h]
                outs.append(acc[0:HEAD_DIM] / acc[HEAD_DIM:HEAD_DIM + 1])
            o_ref[0, :, hp * LANES:(hp + 1) * LANES] = jnp.concatenate(
                outs, axis=0).T.astype(o_ref.dtype)

    @pl.when(i == 0)
    def _():
        finish()

    for n_left in range(4):
        @pl.when((i >= 1) & (left == n_left))
        def _(n_left=n_left):
            ms = [m_sc[h] for h in heads]
            maxes = [max_sc[h] for h in heads]
            bufs = (s_a, s_b)
            for k in range(n_left):
                ms, maxes = step(j_left + k, ms, bufs[k % 2], maxes, bufs[(k + 1) % 2])
            step(i - 1, ms, bufs[n_left % 2], None, None, bias_idx=1)
            finish()


def _moba_attention(q_t, k_aug, v_t, bias_per_dist):
    b, h, s, _ = k_aug.shape
    t = MOBA_BLOCK
    g = HEADS_PER_STEP
    assert g == h
    once = pl.Buffered(1)
    return pl.pallas_call(
        _attn_kernel,
        grid=(b, h // g, s // t),
        in_specs=[
            pl.BlockSpec((1, g, 1, LANES, t), lambda bi, hg, qi: (bi, hg, qi, 0, 0)),
            pl.BlockSpec((1, g, t, LANES), lambda bi, hg, qi: (bi, hg, qi, 0)),
            pl.BlockSpec((1, g, 1, V_ROWS, t), lambda bi, hg, qi: (bi, hg, qi, 0, 0)),
            pl.BlockSpec((g, 1, 3 * t), lambda bi, hg, qi: (hg, 0, 0), pipeline_mode=once),
        ],
        out_specs=pl.BlockSpec((1, t, g * HEAD_DIM), lambda bi, hg, qi: (bi, qi, hg)),
        out_shape=jax.ShapeDtypeStruct((b, s, WIDTH_A), BF16),
        scratch_shapes=[
            pltpu.VMEM((g, s, LANES), BF16),
            pltpu.VMEM((g, V_ROWS, s), BF16),
            pltpu.VMEM((g, t, t), F32),
            pltpu.VMEM((g, t, t), F32),
            pltpu.VMEM((g, V_ROWS, t), F32),
            pltpu.VMEM((g, 1, t), F32),
            pltpu.VMEM((g, 1, t), F32),
            pltpu.VMEM((g, 2, t, t), F32),
        ],
        compiler_params=pltpu.CompilerParams(
            dimension_semantics=("arbitrary", "arbitrary", "arbitrary"),
            vmem_limit_bytes=VMEM_LIMIT),
        name="moba_attn",
    )(q_t, k_aug, v_t, bias_per_dist)


def _rel_bucket(dist):
    n = jnp.maximum(dist, 0)
    max_exact = REL_BUCKETS // 2
    nf = jnp.maximum(n, 1).astype(F32)
    large = max_exact + (jnp.log(nf / max_exact) / math.log(REL_MAX_DIST / max_exact)
                         * (REL_BUCKETS - max_exact)).astype(jnp.int32)
    large = jnp.minimum(large, REL_BUCKETS - 1)
    return jnp.where(n < max_exact, n, large)


def _bias_per_distance(rel_table):
    t = MOBA_BLOCK
    period = 3 * t
    dist = jnp.arange(period)
    dist = jnp.where(dist < 2 * t, dist, dist - period)
    onehot = _rel_bucket(dist)[None, :] == jnp.arange(REL_BUCKETS)[:, None]
    far = rel_table[_rel_bucket(jnp.array(2 * t, jnp.int32))]
    shifted = (rel_table - far[None, :]).T
    per_dist = jnp.sum(jnp.where(onehot[None], shifted[:, :, None], 0.0), axis=1)
    per_dist = jnp.where(dist >= 0, per_dist * LOG2E, NEG_INF).astype(F32)
    return per_dist[:, None, :]


def _mix_kernel(x_ref, a_ref, wpg_ref, bg_ref, wpool_ref, pscale_ref, woa_ref, wob_ref,
                wout_ref, g_ref, b_ref, h_ref, pbuf):
    si = pl.program_id(1)
    t = x_ref.shape[1]

    @pl.when(si == 0)
    def _():
        pbuf[0:HALO, :] = jnp.zeros((HALO, WIDTH_B), F32)

    @pl.when(si > 0)
    def _():
        pbuf[0:HALO, :] = pbuf[t:t + HALO, :]

    pbuf[HALO:HALO + t, :] = jnp.dot(x_ref[0].astype(BF16), wpg_ref[:, :WIDTH_B],
                                     preferred_element_type=F32)
    rows = [slice(c * MIX_CHUNK, (c + 1) * MIX_CHUNK) for c in range(t // MIX_CHUNK)]
    gate_logits = [jnp.dot(x_ref[0, r, :].astype(BF16), wpg_ref[:, WIDTH_B:],
                           preferred_element_type=F32) for r in rows]
    for r, gl in zip(rows, gate_logits):
        n = r.stop - r.start
        pos = si * t + r.start + lax.broadcasted_iota(jnp.int32, (n, 1), 0)
        mixed = []
        for gi, w in enumerate(POOL_WINDOWS):
            cols = slice(gi * POOL_GROUP_DIM, (gi + 1) * POOL_GROUP_DIM)
            own = pbuf[HALO + r.start:HALO + r.stop, cols]
            ws = pbuf[r.start:HALO + r.stop, cols]
            span = 1
            while span < w:
                ws = ws + pltpu.roll(ws, span, 0)
                span *= 2
            ws = ws[HALO:]
            cnt = jnp.minimum(pos + 1, w).astype(F32)
            pooled = ws / cnt - own
            mixed.append(jnp.dot(pooled.astype(BF16), wpool_ref[gi],
                                 preferred_element_type=F32))
        mixed = jnp.concatenate(mixed, axis=1) * pscale_ref[...]
        branch_b = jnp.dot(mixed.astype(BF16), wob_ref[...], preferred_element_type=F32)
        branch_a = jnp.dot(a_ref[0, r, :], woa_ref[...], preferred_element_type=F32)
        gates = jax.nn.sigmoid(gl + bg_ref[...])
        mixin = gates[:, :D_MODEL] * branch_a + gates[:, D_MODEL:] * branch_b
        mix = jnp.dot(mixin.astype(BF16), wout_ref[...], preferred_element_type=F32)
        h_ref[0, r, :] = _layer_norm(DEEPNORM_ALPHA * x_ref[0, r, :] + mix,
                                     g_ref[...], b_ref[...])


def _mix_ln1(x, a, w_pg, b_gate, w_pool, pool_scale, w_oa, w_ob, w_out, ln_g, ln_b, t=1024):
    b, s, d = x.shape
    const2 = lambda bi, si: (0, 0)
    once = pl.Buffered(1)
    return pl.pallas_call(
        _mix_kernel,
        grid=(b, s // t),
        in_specs=[
            pl.BlockSpec((1, t, d), lambda bi, si: (bi, si, 0)),
            pl.BlockSpec((1, t, WIDTH_A), lambda bi, si: (bi, si, 0)),
            pl.BlockSpec(w_pg.shape, const2, pipeline_mode=once),
            pl.BlockSpec(b_gate.shape, const2),
            pl.BlockSpec(w_pool.shape, lambda bi, si: (0, 0, 0)),
            pl.BlockSpec(pool_scale.shape, const2),
            pl.BlockSpec(w_oa.shape, const2, pipeline_mode=once),
            pl.BlockSpec(w_ob.shape, const2, pipeline_mode=once),
            pl.BlockSpec(w_out.shape, const2, pipeline_mode=once),
            pl.BlockSpec(ln_g.shape, const2),
            pl.BlockSpec(ln_b.shape, const2),
        ],
        out_specs=pl.BlockSpec((1, t, d), lambda bi, si: (bi, si, 0)),
        out_shape=jax.ShapeDtypeStruct((b, s, d), F32),
        scratch_shapes=[pltpu.VMEM((HALO + t, WIDTH_B), F32)],
        compiler_params=pltpu.CompilerParams(
            dimension_semantics=("arbitrary", "arbitrary"),
            vmem_limit_bytes=VMEM_LIMIT),
        name="mix_ln1",
    )(x, a, w_pg, b_gate, w_pool, pool_scale, w_oa, w_ob, w_out, ln_g, ln_b)


FF_CHUNK = 1024
FF_ROWS = 256


def _ffn_kernel(h_ref, w1_ref, b1_ref, w2_ref, b2_ref, g_ref, b_ref, o_ref):
    for r0 in range(0, h_ref.shape[0], FF_ROWS):
        h = h_ref[r0:r0 + FF_ROWS, :]
        hb = h.astype(BF16)
        ff = jnp.zeros(h.shape, F32)
        for c in range(D_FF // FF_CHUNK):
            cs = slice(c * FF_CHUNK, (c + 1) * FF_CHUNK)
            u = jnp.dot(hb, w1_ref[:, cs], preferred_element_type=F32) + b1_ref[:, cs]
            u = jnp.square(jnp.maximum(u, 0.0))
            ff = ff + jnp.dot(u.astype(BF16), w2_ref[cs, :], preferred_element_type=F32)
        ff = ff + b2_ref[...]
        o_ref[r0:r0 + FF_ROWS, :] = _layer_norm(DEEPNORM_ALPHA * h + ff, g_ref[...], b_ref[...])


def _ffn_ln2(h, w1, b1, w2, b2, ln_g, ln_b, t=1024):
    n, d = h.shape
    const2 = lambda i: (0, 0)
    once = pl.Buffered(1)
    return pl.pallas_call(
        _ffn_kernel,
        grid=(n // t,),
        in_specs=[
            pl.BlockSpec((t, d), lambda i: (i, 0)),
            pl.BlockSpec(w1.shape, const2, pipeline_mode=once),
            pl.BlockSpec(b1.shape, const2),
            pl.BlockSpec(w2.shape, const2, pipeline_mode=once),
            pl.BlockSpec(b2.shape, const2),
            pl.BlockSpec(ln_g.shape, const2),
            pl.BlockSpec(ln_b.shape, const2),
        ],
        out_specs=pl.BlockSpec((t, d), lambda i: (i, 0)),
        out_shape=jax.ShapeDtypeStruct((n, d), F32),
        compiler_params=pltpu.CompilerParams(
            dimension_semantics=("arbitrary",),
            vmem_limit_bytes=VMEM_LIMIT),
        name="ffn_ln2",
    )(h, w1, b1, w2, b2, ln_g, ln_b)


def kernel(x, w_in, b_gate, rel_table, w_pool, pool_scale, w_o_attn, w_o_pool, w_out,
           ln1_g, ln1_b, w_ff1, b_ff1, w_ff2, b_ff2, ln2_g, ln2_b):
    b, s, d = x.shape
    nb = s // MOBA_BLOCK
    assert s % 1024 == 0 and nb <= MAX_KEY_BLOCKS
    bias_per_dist = _bias_per_distance(rel_table)
    h = x
    for l in range(w_in.shape[0]):
        w_q = w_in[l, :, :WIDTH_A] * (HEAD_DIM ** -0.5)
        w_k = w_in[l, :, WIDTH_A:2 * WIDTH_A].astype(BF16)
        w_v = w_in[l, :, 2 * WIDTH_A:3 * WIDTH_A]
        w_qvt = jnp.concatenate([w_q, w_v], axis=1).T.astype(BF16)
        w_pg = w_in[l, :, 3 * WIDTH_A:].astype(BF16)
        q_t, k_aug, v_t = _qkv_proj(h, w_k, w_qvt)
        a = _moba_attention(q_t, k_aug, v_t, bias_per_dist)
        h = _mix_ln1(h, a, w_pg, b_gate[l][None, :], w_pool[l].astype(BF16),
                     pool_scale[l][None, :], w_o_attn[l].astype(BF16),
                     w_o_pool[l].astype(BF16), w_out[l].astype(BF16),
                     ln1_g[l][None, :], ln1_b[l][None, :])
        h = _ffn_ln2(h.reshape(b * s, d), w_ff1[l].astype(BF16), b_ff1[l][None, :],
                     w_ff2[l].astype(BF16), b_ff2[l][None, :],
                     ln2_g[l][None, :], ln2_b[l][None, :]).reshape(b, s, d)
    return h
```

```python
import math

import jax
import jax.numpy as jnp
from jax import lax
from jax.experimental import pallas as pl
from jax.experimental.pallas import tpu as pltpu

D_MODEL = 1024
N_HEADS = 8
HEAD_DIM = 64
WIDTH_A = N_HEADS * HEAD_DIM
MOBA_BLOCK = 256
MOBA_TOPK = 3
N_POOL_GROUPS = 4
POOL_WINDOWS = (2, 4, 8, 16)
WIDTH_B = 512
POOL_GROUP_DIM = WIDTH_B // N_POOL_GROUPS
D_FF = 4 * D_MODEL
REL_BUCKETS = 32
REL_MAX_DIST = 128
DEEPNORM_ALPHA = 2.0 ** 0.25
LN_EPS = 1e-5
NEG_INF = -1e30

LANES = 128
MAX_KEY_BLOCKS = 32
HEADS_PER_STEP = 8
BF16_SUBLANES = 16
V_ROWS = HEAD_DIM + BF16_SUBLANES
LOG2E = math.log2(math.e)
HALO = 16
MIX_CHUNK = 256
VMEM_LIMIT = 48 * 1024 * 1024

BF16 = jnp.bfloat16
F32 = jnp.float32
NT_DIMS = (((1,), (1,)), ((), ()))


def _layer_norm(y, g, b):
    mu = jnp.mean(y, axis=-1, keepdims=True)
    d = y - mu
    var = jnp.mean(d * d, axis=-1, keepdims=True)
    return d * lax.rsqrt(var + LN_EPS) * g + b


def _qkv_kernel(x_ref, wk_ref, wqvt_ref, q_ref, k_ref, v_ref, kmean_sc):
    si = pl.program_id(1)

    @pl.when(si == 0)
    def _():
        kmean_sc[...] = jnp.zeros(kmean_sc.shape, F32)

    xb = x_ref[0].astype(BF16)
    t = xb.shape[0]
    blocks_per_tile = t // MOBA_BLOCK
    acc_k = jnp.dot(xb, wk_ref[...], preferred_element_type=F32)
    acc_q = lax.dot_general(wqvt_ref[0:WIDTH_A, :], xb, NT_DIMS,
                            preferred_element_type=F32)
    lane = lax.broadcasted_iota(jnp.int32, (t, LANES), 1)
    row = lax.broadcasted_iota(jnp.int32, (t, LANES), 0)
    lo = lane < HEAD_DIM
    onehot = jnp.where(lane == HEAD_DIM + si * blocks_per_tile + row // MOBA_BLOCK, 1.0, 0.0)
    for hp in range(N_HEADS // 2):
        kp = acc_k[:, hp * LANES:(hp + 1) * LANES] * LOG2E
        kr = pltpu.roll(kp, HEAD_DIM, 1)
        k_ref[0, 2 * hp] = jnp.where(lo, kp, onehot).astype(BF16)
        k_ref[0, 2 * hp + 1] = jnp.where(lo, kr, onehot).astype(BF16)

    blk_row = lax.broadcasted_iota(jnp.int32, (MAX_KEY_BLOCKS, LANES), 0)
    lo_row = lax.broadcasted_iota(jnp.int32, (MAX_KEY_BLOCKS, LANES), 1) < HEAD_DIM
    for c in range(blocks_per_tile):
        mean_c = jnp.mean(acc_k[c * MOBA_BLOCK:(c + 1) * MOBA_BLOCK], axis=0, keepdims=True)
        for hp in range(N_HEADS // 2):
            pair = mean_c[:, hp * LANES:(hp + 1) * LANES]
            for h, m in ((2 * hp, pair), (2 * hp + 1, pltpu.roll(pair, HEAD_DIM, 1))):
                new_row = jnp.where(lo_row, jnp.broadcast_to(m, (MAX_KEY_BLOCKS, LANES)), 0.0)
                kmean_sc[h] = jnp.where(blk_row == si * blocks_per_tile + c, new_row, kmean_sc[h])

    sel_row = lax.broadcasted_iota(jnp.int32, (MAX_KEY_BLOCKS, t), 0)
    own_blk = si * blocks_per_tile + lax.broadcasted_iota(
        jnp.int32, (MAX_KEY_BLOCKS, t), 1) // MOBA_BLOCK
    past = sel_row < own_blk
    zeros64 = jnp.zeros((LANES - HEAD_DIM, t), BF16)
    zeros32 = jnp.zeros((LANES - HEAD_DIM - MAX_KEY_BLOCKS, t), BF16)
    pad1 = jnp.ones((V_ROWS - HEAD_DIM, t), BF16)
    for h in range(N_HEADS):
        qt = acc_q[h * HEAD_DIM:(h + 1) * HEAD_DIM].astype(BF16)
        gate = jnp.dot(kmean_sc[h].astype(BF16), jnp.concatenate([qt, zeros64], axis=0),
                       preferred_element_type=F32)
        g = jnp.where(past, gate, NEG_INF)
        for _ in range(MOBA_TOPK):
            mx = jnp.max(g, axis=0, keepdims=True)
            first = jnp.min(jnp.where(g == mx, sel_row, MAX_KEY_BLOCKS), axis=0, keepdims=True)
            g = jnp.where(sel_row == first, -jnp.inf, g)
        sel = ((g == -jnp.inf) & past) | (sel_row == own_blk)
        selbias = jnp.where(sel, 0.0, NEG_INF).astype(BF16)
        for c in range(blocks_per_tile):
            cs = slice(c * MOBA_BLOCK, (c + 1) * MOBA_BLOCK)
            q_ref[0, h, c, 0:HEAD_DIM, :] = qt[:, cs]
            q_ref[0, h, c, HEAD_DIM:HEAD_DIM + MAX_KEY_BLOCKS, :] = selbias[:, cs]
            q_ref[0, h, c, HEAD_DIM + MAX_KEY_BLOCKS:LANES, :] = zeros32[:, cs]

    acc_v = lax.dot_general(wqvt_ref[WIDTH_A:2 * WIDTH_A, :], xb, NT_DIMS,
                            preferred_element_type=F32)
    for h in range(N_HEADS):
        vt = acc_v[h * HEAD_DIM:(h + 1) * HEAD_DIM].astype(BF16)
        for c in range(blocks_per_tile):
            cs = slice(c * MOBA_BLOCK, (c + 1) * MOBA_BLOCK)
            v_ref[0, h, c, 0:HEAD_DIM, :] = vt[:, cs]
            v_ref[0, h, c, HEAD_DIM:V_ROWS, :] = pad1[:, cs]


def _qkv_proj(x, w_k, w_qvt, t=2 * MOBA_BLOCK):
    b, s, d = x.shape
    return pl.pallas_call(
        _qkv_kernel,
        grid=(b, s // t),
        in_specs=[
            pl.BlockSpec((1, t, d), lambda bi, si: (bi, si, 0)),
            pl.BlockSpec(w_k.shape, lambda bi, si: (0, 0)),
            pl.BlockSpec(w_qvt.shape, lambda bi, si: (0, 0)),
        ],
        out_specs=[
            pl.BlockSpec((1, N_HEADS, t // MOBA_BLOCK, LANES, MOBA_BLOCK),
                         lambda bi, si: (bi, 0, si, 0, 0)),
            pl.BlockSpec((1, N_HEADS, t, LANES), lambda bi, si: (bi, 0, si, 0)),
            pl.BlockSpec((1, N_HEADS, t // MOBA_BLOCK, V_ROWS, MOBA_BLOCK),
                         lambda bi, si: (bi, 0, si, 0, 0)),
        ],
        out_shape=[
            jax.ShapeDtypeStruct((b, N_HEADS, s // MOBA_BLOCK, LANES, MOBA_BLOCK), BF16),
            jax.ShapeDtypeStruct((b, N_HEADS, s, LANES), BF16),
            jax.ShapeDtypeStruct((b, N_HEADS, s // MOBA_BLOCK, V_ROWS, MOBA_BLOCK), BF16),
        ],
        scratch_shapes=[pltpu.VMEM((N_HEADS, MAX_KEY_BLOCKS, LANES), F32)],
        compiler_params=pltpu.CompilerParams(
            dimension_semantics=("arbitrary", "arbitrary"),
            vmem_limit_bytes=VMEM_LIMIT),
        name="qkv_proj",
    )(x, w_k, w_qvt)


def _attn_kernel(q_ref, k_ref, v_ref, pd_ref, o_ref, k_all, v_all, s_a, s_b, acc_sc, m_sc,
                 max_sc, bias_sc):
    i = pl.program_id(2)
    t = MOBA_BLOCK
    heads = range(HEADS_PER_STEP)

    @pl.when((pl.program_id(0) == 0) & (i == 0))
    def _():
        for h in heads:
            skew = pltpu.roll(jnp.broadcast_to(pd_ref[h], (t, 3 * t)), 0, 1,
                              stride=1, stride_axis=0)
            bias_sc[h, 0] = skew[:, 0:t]
            bias_sc[h, 1] = skew[:, t:2 * t]

    own = pl.ds(pl.multiple_of(i * t, t), t)
    for h in heads:
        k_all[h, own, :] = k_ref[0, h]
        v_all[h, :, own] = v_ref[0, h, 0]

    def key_block(h, j):
        return k_all[h, pl.ds(pl.multiple_of(j * t, t), t), :]

    def head_scores_into(s_dst, h, j):
        s = jnp.dot(key_block(h, j), q_ref[0, h, 0], preferred_element_type=F32)
        s_dst[h] = s
        return jnp.max(s, axis=0, keepdims=True)

    def weighted_values(h, j, p):
        vv = v_all[h, :, pl.ds(pl.multiple_of(j * t, t), t)]
        return jnp.dot(vv, p.astype(BF16), preferred_element_type=F32)

    def step(j, ms, s_cur, max_cur, s_nxt, bias_idx=None):
        new_ms, max_nxt = [], []
        for h in heads:
            if s_nxt is not None:
                max_nxt.append(head_scores_into(s_nxt, h, j + 1))
            s = s_cur[h]
            if bias_idx is not None:
                s = s + bias_sc[h, bias_idx]
                m_new = jnp.maximum(ms[h], jnp.max(s, axis=0, keepdims=True))
            else:
                m_new = jnp.maximum(ms[h], max_cur[h])
            alpha = jnp.exp2(ms[h] - m_new)
            acc_sc[h] = alpha * acc_sc[h] + weighted_values(h, j, jnp.exp2(s - m_new))
            new_ms.append(m_new)
        return new_ms, max_nxt

    ms = []
    for h in heads:
        s = jnp.dot(k_ref[0, h], q_ref[0, h, 0], preferred_element_type=F32) + bias_sc[h, 0]
        s_b[h] = s
        ms.append(jnp.max(s, axis=0, keepdims=True))

    max_a = []
    for h in heads:
        max_a.append(head_scores_into(s_a, h, 0))
        acc_sc[h] = jnp.dot(v_ref[0, h, 0], jnp.exp2(s_b[h] - ms[h]).astype(BF16),
                            preferred_element_type=F32)

    n_far = jnp.maximum(i - 1, 0)

    def two_steps(j, ms, max_a):
        ms, max_b = step(j, list(ms), s_a, list(max_a), s_b)
        return step(j + 1, ms, s_b, max_b, s_a)

    def four_steps(j, ms, max_a):
        ms, max_a = two_steps(j, ms, max_a)
        return two_steps(j + 2, ms, max_a)

    def eight_steps(jj, carry):
        ms, max_a = four_steps(8 * jj, *carry)
        ms, max_a = four_steps(8 * jj + 4, ms, max_a)
        return tuple(ms), tuple(max_a)

    def save_state(ms, max_a):
        for h in heads:
            m_sc[h] = ms[h]
            max_sc[h] = max_a[h]

    ms, max_a = lax.fori_loop(0, n_far // 8, eight_steps, (tuple(ms), tuple(max_a)))
    save_state(ms, max_a)

    @pl.when(n_far % 8 >= 4)
    def _():
        save_state(*four_steps(n_far - n_far % 8, [m_sc[h] for h in heads],
                               [max_sc[h] for h in heads]))

    left = n_far % 4
    j_left = n_far - left

    def finish():
        for hp in range(HEADS_PER_STEP // 2):
            outs = []
            for h in (2 * hp, 2 * hp + 1):
                acc = acc_sc[h]
                outs.append(acc[0:HEAD_DIM] / acc[HEAD_DIM:HEAD_DIM + 1])
            o_ref[0, :, hp * LANES:(hp + 1) * LANES] = jnp.concatenate(
                outs, axis=0).T.astype(o_ref.dtype)

    @pl.when(i == 0)
    def _():
        finish()

    for n_left in range(4):
        @pl.when((i >= 1) & (left == n_left))
        def _(n_left=n_left):
            ms = [m_sc[h] for h in heads]
            maxes = [max_sc[h] for h in heads]
            bufs = (s_a, s_b)
            for k in range(n_left):
                ms, maxes = step(j_left + k, ms, bufs[k % 2], maxes, bufs[(k + 1) % 2])
            step(i - 1, ms, bufs[n_left % 2], None, None, bias_idx=1)
            finish()


def _moba_attention(q_t, k_aug, v_t, bias_per_dist):
    b, h, s, _ = k_aug.shape
    t = MOBA_BLOCK
    g = HEADS_PER_STEP
    assert g == h
    once = pl.Buffered(1)
    return pl.pallas_call(
        _attn_kernel,
        grid=(b, h // g, s // t),
        in_specs=[
            pl.BlockSpec((1, g, 1, LANES, t), lambda bi, hg, qi: (bi, hg, qi, 0, 0)),
            pl.BlockSpec((1, g, t, LANES), lambda bi, hg, qi: (bi, hg, qi, 0)),
            pl.BlockSpec((1, g, 1, V_ROWS, t), lambda bi, hg, qi: (bi, hg, qi, 0, 0)),
            pl.BlockSpec((g, 1, 3 * t), lambda bi, hg, qi: (hg, 0, 0), pipeline_mode=once),
        ],
        out_specs=pl.BlockSpec((1, t, g * HEAD_DIM), lambda bi, hg, qi: (bi, qi, hg)),
        out_shape=jax.ShapeDtypeStruct((b, s, WIDTH_A), BF16),
        scratch_shapes=[
            pltpu.VMEM((g, s, LANES), BF16),
            pltpu.VMEM((g, V_ROWS, s), BF16),
            pltpu.VMEM((g, t, t), F32),
            pltpu.VMEM((g, t, t), F32),
            pltpu.VMEM((g, V_ROWS, t), F32),
            pltpu.VMEM((g, 1, t), F32),
            pltpu.VMEM((g, 1, t), F32),
            pltpu.VMEM((g, 2, t, t), F32),
        ],
        compiler_params=pltpu.CompilerParams(
            dimension_semantics=("arbitrary", "arbitrary", "arbitrary"),
            vmem_limit_bytes=VMEM_LIMIT),
        name="moba_attn",
    )(q_t, k_aug, v_t, bias_per_dist)


def _rel_bucket(dist):
    n = jnp.maximum(dist, 0)
    max_exact = REL_BUCKETS // 2
    nf = jnp.maximum(n, 1).astype(F32)
    large = max_exact + (jnp.log(nf / max_exact) / math.log(REL_MAX_DIST / max_exact)
                         * (REL_BUCKETS - max_exact)).astype(jnp.int32)
    large = jnp.minimum(large, REL_BUCKETS - 1)
    return jnp.where(n < max_exact, n, large)


def _bias_per_distance(rel_table):
    t = MOBA_BLOCK
    period = 3 * t
    dist = jnp.arange(period)
    dist = jnp.where(dist < 2 * t, dist, dist - period)
    onehot = _rel_bucket(dist)[None, :] == jnp.arange(REL_BUCKETS)[:, None]
    far = rel_table[_rel_bucket(jnp.array(2 * t, jnp.int32))]
    shifted = (rel_table - far[None, :]).T
    per_dist = jnp.sum(jnp.where(onehot[None], shifted[:, :, None], 0.0), axis=1)
    per_dist = jnp.where(dist >= 0, per_dist * LOG2E, NEG_INF).astype(F32)
    return per_dist[:, None, :]


def _mix_kernel(x_ref, a_ref, wpg_ref, bg_ref, wpool_ref, pscale_ref, woa_ref, wob_ref,
                wout_ref, g_ref, b_ref, h_ref, pbuf):
    si = pl.program_id(1)
    t = x_ref.shape[1]

    @pl.when(si == 0)
    def _():
        pbuf[0:HALO, :] = jnp.zeros((HALO, WIDTH_B), F32)

    @pl.when(si > 0)
    def _():
        pbuf[0:HALO, :] = pbuf[t:t + HALO, :]

    pbuf[HALO:HALO + t, :] = jnp.dot(x_ref[0].astype(BF16), wpg_ref[:, :WIDTH_B],
                                     preferred_element_type=F32)
    rows = [slice(c * MIX_CHUNK, (c + 1) * MIX_CHUNK) for c in range(t // MIX_CHUNK)]
    gate_logits = [jnp.dot(x_ref[0, r, :].astype(BF16), wpg_ref[:, WIDTH_B:],
                           preferred_element_type=F32) for r in rows]
    for r, gl in zip(rows, gate_logits):
        n = r.stop - r.start
        pos = si * t + r.start + lax.broadcasted_iota(jnp.int32, (n, 1), 0)
        mixed = []
        for gi, w in enumerate(POOL_WINDOWS):
            cols = slice(gi * POOL_GROUP_DIM, (gi + 1) * POOL_GROUP_DIM)
            own = pbuf[HALO + r.start:HALO + r.stop, cols]
            ws = pbuf[r.start:HALO + r.stop, cols]
            span = 1
            while span < w:
                ws = ws + pltpu.roll(ws, span, 0)
                span *= 2
            ws = ws[HALO:]
            cnt = jnp.minimum(pos + 1, w).astype(F32)
            pooled = ws / cnt - own
            mixed.append(jnp.dot(pooled.astype(BF16), wpool_ref[gi],
                                 preferred_element_type=F32))
        mixed = jnp.concatenate(mixed, axis=1) * pscale_ref[...]
        branch_b = jnp.dot(mixed.astype(BF16), wob_ref[...], preferred_element_type=F32)
        branch_a = jnp.dot(a_ref[0, r, :], woa_ref[...], preferred_element_type=F32)
        gates = jax.nn.sigmoid(gl + bg_ref[...])
        mixin = gates[:, :D_MODEL] * branch_a + gates[:, D_MODEL:] * branch_b
        mix = jnp.dot(mixin.astype(BF16), wout_ref[...], preferred_element_type=F32)
        h_ref[0, r, :] = _layer_norm(DEEPNORM_ALPHA * x_ref[0, r, :] + mix,
                                     g_ref[...], b_ref[...])


def _mix_ln1(x, a, w_pg, b_gate, w_pool, pool_scale, w_oa, w_ob, w_out, ln_g, ln_b, t=1024):
    b, s, d = x.shape
    const2 = lambda bi, si: (0, 0)
    once = pl.Buffered(1)
    return pl.pallas_call(
        _mix_kernel,
        grid=(b, s // t),
        in_specs=[
            pl.BlockSpec((1, t, d), lambda bi, si: (bi, si, 0)),
            pl.BlockSpec((1, t, WIDTH_A), lambda bi, si: (bi, si, 0)),
            pl.BlockSpec(w_pg.shape, const2, pipeline_mode=once),
            pl.BlockSpec(b_gate.shape, const2),
            pl.BlockSpec(w_pool.shape, lambda bi, si: (0, 0, 0)),
            pl.BlockSpec(pool_scale.shape, const2),
            pl.BlockSpec(w_oa.shape, const2, pipeline_mode=once),
            pl.BlockSpec(w_ob.shape, const2, pipeline_mode=once),
            pl.BlockSpec(w_out.shape, const2, pipeline_mode=once),
            pl.BlockSpec(ln_g.shape, const2),
            pl.BlockSpec(ln_b.shape, const2),
        ],
        out_specs=pl.BlockSpec((1, t, d), lambda bi, si: (bi, si, 0)),
        out_shape=jax.ShapeDtypeStruct((b, s, d), F32),
        scratch_shapes=[pltpu.VMEM((HALO + t, WIDTH_B), F32)],
        compiler_params=pltpu.CompilerParams(
            dimension_semantics=("arbitrary", "arbitrary"),
            vmem_limit_bytes=VMEM_LIMIT),
        name="mix_ln1",
    )(x, a, w_pg, b_gate, w_pool, pool_scale, w_oa, w_ob, w_out, ln_g, ln_b)


FF_CHUNK = 1024
FF_ROWS = 256


def _ffn_kernel(h_ref, w1_ref, b1_ref, w2_ref, b2_ref, g_ref, b_ref, o_ref):
    for r0 in range(0, h_ref.shape[0], FF_ROWS):
        h = h_ref[r0:r0 + FF_ROWS, :]
        hb = h.astype(BF16)
        ff = jnp.zeros(h.shape, F32)
        for c in range(D_FF // FF_CHUNK):
            cs = slice(c * FF_CHUNK, (c + 1) * FF_CHUNK)
            u = jnp.dot(hb, w1_ref[:, cs], preferred_element_type=F32) + b1_ref[:, cs]
            u = jnp.square(jnp.maximum(u, 0.0))
            ff = ff + jnp.dot(u.astype(BF16), w2_ref[cs, :], preferred_element_type=F32)
        ff = ff + b2_ref[...]
        o_ref[r0:r0 + FF_ROWS, :] = _layer_norm(DEEPNORM_ALPHA * h + ff, g_ref[...], b_ref[...])


def _ffn_ln2(h, w1, b1, w2, b2, ln_g, ln_b, t=1024):
    n, d = h.shape
    const2 = lambda i: (0, 0)
    once = pl.Buffered(1)
    return pl.pallas_call(
        _ffn_kernel,
        grid=(n // t,),
        in_specs=[
            pl.BlockSpec((t, d), lambda i: (i, 0)),
            pl.BlockSpec(w1.shape, const2, pipeline_mode=once),
            pl.BlockSpec(b1.shape, const2),
            pl.BlockSpec(w2.shape, const2, pipeline_mode=once),
            pl.BlockSpec(b2.shape, const2),
            pl.BlockSpec(ln_g.shape, const2),
            pl.BlockSpec(ln_b.shape, const2),
        ],
        out_specs=pl.BlockSpec((t, d), lambda i: (i, 0)),
        out_shape=jax.ShapeDtypeStruct((n, d), F32),
        compiler_params=pltpu.CompilerParams(
            dimension_semantics=("arbitrary",),
            vmem_limit_bytes=VMEM_LIMIT),
        name="ffn_ln2",
    )(h, w1, b1, w2, b2, ln_g, ln_b)


def kernel(x, w_in, b_gate, rel_table, w_pool, pool_scale, w_o_attn, w_o_pool, w_out,
           ln1_g, ln1_b, w_ff1, b_ff1, w_ff2, b_ff2, ln2_g, ln2_b):
    b, s, d = x.shape
    nb = s // MOBA_BLOCK
    assert s % 1024 == 0 and nb <= MAX_KEY_BLOCKS
    bias_per_dist = _bias_per_distance(rel_table)
    h = x
    for l in range(w_in.shape[0]):
        w_q = w_in[l, :, :WIDTH_A] * (HEAD_DIM ** -0.5)
        w_k = w_in[l, :, WIDTH_A:2 * WIDTH_A].astype(BF16)
        w_v = w_in[l, :, 2 * WIDTH_A:3 * WIDTH_A]
        w_qvt = jnp.concatenate([w_q, w_v], axis=1).T.astype(BF16)
        w_pg = w_in[l, :, 3 * WIDTH_A:].astype(BF16)
        q_t, k_aug, v_t = _qkv_proj(h, w_k, w_qvt)
        a = _moba_attention(q_t, k_aug, v_t, bias_per_dist)
        h = _mix_ln1(h, a, w_pg, b_gate[l][None, :], w_pool[l].astype(BF16),
                     pool_scale[l][None, :], w_o_attn[l].astype(BF16),
                     w_o_pool[l].astype(BF16), w_out[l].astype(BF16),
                     ln1_g[l][None, :], ln1_b[l][None, :])
        h = _ffn_ln2(h.reshape(b * s, d), w_ff1[l].astype(BF16), b_ff1[l][None, :],
                     w_ff2[l].astype(BF16), b_ff2[l][None, :],
                     ln2_g[l][None, :], ln2_b[l][None, :]).reshape(b, s, d)
    return h
```

```python
import math

import jax
import jax.numpy as jnp
from jax import lax
from jax.experimental import pallas as pl
from jax.experimental.pallas import tpu as pltpu

D_MODEL = 1024
N_HEADS = 8
HEAD_DIM = 64
WIDTH_A = N_HEADS * HEAD_DIM
MOBA_BLOCK = 256
MOBA_TOPK = 3
N_POOL_GROUPS = 4
POOL_WINDOWS = (2, 4, 8, 16)
WIDTH_B = 512
POOL_GROUP_DIM = WIDTH_B // N_POOL_GROUPS
D_FF = 4 * D_MODEL
REL_BUCKETS = 32
REL_MAX_DIST = 128
DEEPNORM_ALPHA = 2.0 ** 0.25
LN_EPS = 1e-5
NEG_INF = -1e30

LANES = 128
MAX_KEY_BLOCKS = 32
HEADS_PER_STEP = 8
BF16_SUBLANES = 16
V_ROWS = HEAD_DIM + BF16_SUBLANES
LOG2E = math.log2(math.e)
HALO = 16
MIX_CHUNK = 256
VMEM_LIMIT = 48 * 1024 * 1024

BF16 = jnp.bfloat16
F32 = jnp.float32
NT_DIMS = (((1,), (1,)), ((), ()))


def _layer_norm(y, g, b):
    mu = jnp.mean(y, axis=-1, keepdims=True)
    d = y - mu
    var = jnp.mean(d * d, axis=-1, keepdims=True)
    return d * lax.rsqrt(var + LN_EPS) * g + b


def _qkv_kernel(x_ref, wk_ref, wqvt_ref, q_ref, k_ref, v_ref, kmean_sc):
    si = pl.program_id(1)

    @pl.when(si == 0)
    def _():
        kmean_sc[...] = jnp.zeros(kmean_sc.shape, F32)

    xb = x_ref[0].astype(BF16)
    t = xb.shape[0]
    blocks_per_tile = t // MOBA_BLOCK
    acc_k = jnp.dot(xb, wk_ref[...], preferred_element_type=F32)
    acc_q = lax.dot_general(wqvt_ref[0:WIDTH_A, :], xb, NT_DIMS,
                            preferred_element_type=F32)
    lane = lax.broadcasted_iota(jnp.int32, (t, LANES), 1)
    row = lax.broadcasted_iota(jnp.int32, (t, LANES), 0)
    lo = lane < HEAD_DIM
    onehot = jnp.where(lane == HEAD_DIM + si * blocks_per_tile + row // MOBA_BLOCK, 1.0, 0.0)
    for hp in range(N_HEADS // 2):
        kp = acc_k[:, hp * LANES:(hp + 1) * LANES] * LOG2E
        kr = pltpu.roll(kp, HEAD_DIM, 1)
        k_ref[0, 2 * hp] = jnp.where(lo, kp, onehot).astype(BF16)
        k_ref[0, 2 * hp + 1] = jnp.where(lo, kr, onehot).astype(BF16)

    blk_row = lax.broadcasted_iota(jnp.int32, (MAX_KEY_BLOCKS, LANES), 0)
    lo_row = lax.broadcasted_iota(jnp.int32, (MAX_KEY_BLOCKS, LANES), 1) < HEAD_DIM
    for c in range(blocks_per_tile):
        mean_c = jnp.mean(acc_k[c * MOBA_BLOCK:(c + 1) * MOBA_BLOCK], axis=0, keepdims=True)
        for hp in range(N_HEADS // 2):
            pair = mean_c[:, hp * LANES:(hp + 1) * LANES]
            for h, m in ((2 * hp, pair), (2 * hp + 1, pltpu.roll(pair, HEAD_DIM, 1))):
                new_row = jnp.where(lo_row, jnp.broadcast_to(m, (MAX_KEY_BLOCKS, LANES)), 0.0)
                kmean_sc[h] = jnp.where(blk_row == si * blocks_per_tile + c, new_row, kmean_sc[h])

    sel_row = lax.broadcasted_iota(jnp.int32, (MAX_KEY_BLOCKS, t), 0)
    own_blk = si * blocks_per_tile + lax.broadcasted_iota(
        jnp.int32, (MAX_KEY_BLOCKS, t), 1) // MOBA_BLOCK
    past = sel_row < own_blk
    zeros64 = jnp.zeros((LANES - HEAD_DIM, t), BF16)
    zeros32 = jnp.zeros((LANES - HEAD_DIM - MAX_KEY_BLOCKS, t), BF16)
    pad1 = jnp.ones((V_ROWS - HEAD_DIM, t), BF16)
    for h in range(N_HEADS):
        qt = acc_q[h * HEAD_DIM:(h + 1) * HEAD_DIM].astype(BF16)
        gate = jnp.dot(kmean_sc[h].astype(BF16), jnp.concatenate([qt, zeros64], axis=0),
                       preferred_element_type=F32)
        g = jnp.where(past, gate, NEG_INF)
        for _ in range(MOBA_TOPK):
            mx = jnp.max(g, axis=0, keepdims=True)
            first = jnp.min(jnp.where(g == mx, sel_row, MAX_KEY_BLOCKS), axis=0, keepdims=True)
            g = jnp.where(sel_row == first, -jnp.inf, g)
        sel = ((g == -jnp.inf) & past) | (sel_row == own_blk)
        selbias = jnp.where(sel, 0.0, NEG_INF).astype(BF16)
        for c in range(blocks_per_tile):
            cs = slice(c * MOBA_BLOCK, (c + 1) * MOBA_BLOCK)
            q_ref[0, h, c, 0:HEAD_DIM, :] = qt[:, cs]
            q_ref[0, h, c, HEAD_DIM:HEAD_DIM + MAX_KEY_BLOCKS, :] = selbias[:, cs]
            q_ref[0, h, c, HEAD_DIM + MAX_KEY_BLOCKS:LANES, :] = zeros32[:, cs]

    acc_v = lax.dot_general(wqvt_ref[WIDTH_A:2 * WIDTH_A, :], xb, NT_DIMS,
                            preferred_element_type=F32)
    for h in range(N_HEADS):
        vt = acc_v[h * HEAD_DIM:(h + 1) * HEAD_DIM].astype(BF16)
        for c in range(blocks_per_tile):
            cs = slice(c * MOBA_BLOCK, (c + 1) * MOBA_BLOCK)
            v_ref[0, h, c, 0:HEAD_DIM, :] = vt[:, cs]
            v_ref[0, h, c, HEAD_DIM:V_ROWS, :] = pad1[:, cs]


def _qkv_proj(x, w_k, w_qvt, t=2 * MOBA_BLOCK):
    b, s, d = x.shape
    return pl.pallas_call(
        _qkv_kernel,
        grid=(b, s // t),
        in_specs=[
            pl.BlockSpec((1, t, d), lambda bi, si: (bi, si, 0)),
            pl.BlockSpec(w_k.shape, lambda bi, si: (0, 0)),
            pl.BlockSpec(w_qvt.shape, lambda bi, si: (0, 0)),
        ],
        out_specs=[
            pl.BlockSpec((1, N_HEADS, t // MOBA_BLOCK, LANES, MOBA_BLOCK),
                         lambda bi, si: (bi, 0, si, 0, 0)),
            pl.BlockSpec((1, N_HEADS, t, LANES), lambda bi, si: (bi, 0, si, 0)),
            pl.BlockSpec((1, N_HEADS, t // MOBA_BLOCK, V_ROWS, MOBA_BLOCK),
                         lambda bi, si: (bi, 0, si, 0, 0)),
        ],
        out_shape=[
            jax.ShapeDtypeStruct((b, N_HEADS, s // MOBA_BLOCK, LANES, MOBA_BLOCK), BF16),
            jax.ShapeDtypeStruct((b, N_HEADS, s, LANES), BF16),
            jax.ShapeDtypeStruct((b, N_HEADS, s // MOBA_BLOCK, V_ROWS, MOBA_BLOCK), BF16),
        ],
        scratch_shapes=[pltpu.VMEM((N_HEADS, MAX_KEY_BLOCKS, LANES), F32)],
        compiler_params=pltpu.CompilerParams(
            dimension_semantics=("arbitrary", "arbitrary"),
            vmem_limit_bytes=VMEM_LIMIT),
        name="qkv_proj",
    )(x, w_k, w_qvt)


def _attn_kernel(q_ref, k_ref, v_ref, pd_ref, o_ref, k_all, v_all, s_a, s_b, acc_sc, m_sc,
                 max_sc, bias_sc):
    i = pl.program_id(2)
    t = MOBA_BLOCK
    heads = range(HEADS_PER_STEP)

    @pl.when((pl.program_id(0) == 0) & (i == 0))
    def _():
        for h in heads:
            skew = pltpu.roll(jnp.broadcast_to(pd_ref[h], (t, 3 * t)), 0, 1,
                              stride=1, stride_axis=0)
            bias_sc[h, 0] = skew[:, 0:t]
            bias_sc[h, 1] = skew[:, t:2 * t]

    own = pl.ds(pl.multiple_of(i * t, t), t)
    for h in heads:
        k_all[h, own, :] = k_ref[0, h]
        v_all[h, :, own] = v_ref[0, h, 0]

    def key_block(h, j):
        return k_all[h, pl.ds(pl.multiple_of(j * t, t), t), :]

    def head_scores_into(s_dst, h, j):
        s = jnp.dot(key_block(h, j), q_ref[0, h, 0], preferred_element_type=F32)
        s_dst[h] = s
        return jnp.max(s, axis=0, keepdims=True)

    def weighted_values(h, j, p):
        vv = v_all[h, :, pl.ds(pl.multiple_of(j * t, t), t)]
        return jnp.dot(vv, p.astype(BF16), preferred_element_type=F32)

    def step(j, ms, s_cur, max_cur, s_nxt, bias_idx=None):
        new_ms, max_nxt = [], []
        for h in heads:
            if s_nxt is not None:
                max_nxt.append(head_scores_into(s_nxt, h, j + 1))
            s = s_cur[h]
            if bias_idx is not None:
                s = s + bias_sc[h, bias_idx]
                m_new = jnp.maximum(ms[h], jnp.max(s, axis=0, keepdims=True))
            else:
                m_new = jnp.maximum(ms[h], max_cur[h])
            alpha = jnp.exp2(ms[h] - m_new)
            acc_sc[h] = alpha * acc_sc[h] + weighted_values(h, j, jnp.exp2(s - m_new))
            new_ms.append(m_new)
        return new_ms, max_nxt

    ms = []
    for h in heads:
        s = jnp.dot(k_ref[0, h], q_ref[0, h, 0], preferred_element_type=F32) + bias_sc[h, 0]
        s_b[h] = s
        ms.append(jnp.max(s, axis=0, keepdims=True))

    max_a = []
    for h in heads:
        max_a.append(head_scores_into(s_a, h, 0))
        acc_sc[h] = jnp.dot(v_ref[0, h, 0], jnp.exp2(s_b[h] - ms[h]).astype(BF16),
                            preferred_element_type=F32)

    n_far = jnp.maximum(i - 1, 0)

    def two_steps(j, ms, max_a):
        ms, max_b = step(j, list(ms), s_a, list(max_a), s_b)
        return step(j + 1, ms, s_b, max_b, s_a)

    def four_steps(j, ms, max_a):
        ms, max_a = two_steps(j, ms, max_a)
        return two_steps(j + 2, ms, max_a)

    def save_state(ms, max_a):
        for h in heads:
            m_sc[h] = ms[h]
            max_sc[h] = max_a[h]

    def eight_steps(jj, carry):
        ms, max_a = four_steps(8 * jj, [m_sc[h] for h in heads], [max_sc[h] for h in heads])
        save_state(*four_steps(8 * jj + 4, ms, max_a))
        return carry

    save_state(ms, max_a)
    lax.fori_loop(0, n_far // 8, eight_steps, 0)

    @pl.when(n_far % 8 >= 4)
    def _():
        save_state(*four_steps(n_far - n_far % 8, [m_sc[h] for h in heads],
                               [max_sc[h] for h in heads]))

    left = n_far % 4
    j_left = n_far - left

    def finish():
        for hp in range(HEADS_PER_STEP // 2):
            outs = []
            for h in (2 * hp, 2 * hp + 1):
                acc = acc_sc[h]
                outs.append(acc[0:HEAD_DIM] / acc[HEAD_DIM:HEAD_DIM + 1])
            o_ref[0, :, hp * LANES:(hp + 1) * LANES] = jnp.concatenate(
                outs, axis=0).T.astype(o_ref.dtype)

    @pl.when(i == 0)
    def _():
        finish()

    for n_left in range(4):
        @pl.when((i >= 1) & (left == n_left))
        def _(n_left=n_left):
            ms = [m_sc[h] for h in heads]
            maxes = [max_sc[h] for h in heads]
            bufs = (s_a, s_b)
            for k in range(n_left):
                ms, maxes = step(j_left + k, ms, bufs[k % 2], maxes, bufs[(k + 1) % 2])
            step(i - 1, ms, bufs[n_left % 2], None, None, bias_idx=1)
            finish()


def _moba_attention(q_t, k_aug, v_t, bias_per_dist):
    b, h, s, _ = k_aug.shape
    t = MOBA_BLOCK
    g = HEADS_PER_STEP
    assert g == h
    once = pl.Buffered(1)
    return pl.pallas_call(
        _attn_kernel,
        grid=(b, h // g, s // t),
        in_specs=[
            pl.BlockSpec((1, g, 1, LANES, t), lambda bi, hg, qi: (bi, hg, qi, 0, 0)),
            pl.BlockSpec((1, g, t, LANES), lambda bi, hg, qi: (bi, hg, qi, 0)),
            pl.BlockSpec((1, g, 1, V_ROWS, t), lambda bi, hg, qi: (bi, hg, qi, 0, 0)),
            pl.BlockSpec((g, 1, 3 * t), lambda bi, hg, qi: (hg, 0, 0), pipeline_mode=once),
        ],
        out_specs=pl.BlockSpec((1, t, g * HEAD_DIM), lambda bi, hg, qi: (bi, qi, hg)),
        out_shape=jax.ShapeDtypeStruct((b, s, WIDTH_A), BF16),
        scratch_shapes=[
            pltpu.VMEM((g, s, LANES), BF16),
            pltpu.VMEM((g, V_ROWS, s), BF16),
            pltpu.VMEM((g, t, t), F32),
            pltpu.VMEM((g, t, t), F32),
            pltpu.VMEM((g, V_ROWS, t), F32),
            pltpu.VMEM((g, 1, t), F32),
            pltpu.VMEM((g, 1, t), F32),
            pltpu.VMEM((g, 2, t, t), F32),
        ],
        compiler_params=pltpu.CompilerParams(
            dimension_semantics=("arbitrary", "arbitrary", "arbitrary"),
            vmem_limit_bytes=VMEM_LIMIT),
        name="moba_attn",
    )(q_t, k_aug, v_t, bias_per_dist)


def _rel_bucket(dist):
    n = jnp.maximum(dist, 0)
    max_exact = REL_BUCKETS // 2
    nf = jnp.maximum(n, 1).astype(F32)
    large = max_exact + (jnp.log(nf / max_exact) / math.log(REL_MAX_DIST / max_exact)
                         * (REL_BUCKETS - max_exact)).astype(jnp.int32)
    large = jnp.minimum(large, REL_BUCKETS - 1)
    return jnp.where(n < max_exact, n, large)


def _bias_per_distance(rel_table):
    t = MOBA_BLOCK
    period = 3 * t
    dist = jnp.arange(period)
    dist = jnp.where(dist < 2 * t, dist, dist - period)
    onehot = _rel_bucket(dist)[None, :] == jnp.arange(REL_BUCKETS)[:, None]
    far = rel_table[_rel_bucket(jnp.array(2 * t, jnp.int32))]
    shifted = (rel_table - far[None, :]).T
    per_dist = jnp.sum(jnp.where(onehot[None], shifted[:, :, None], 0.0), axis=1)
    per_dist = jnp.where(dist >= 0, per_dist * LOG2E, NEG_INF).astype(F32)
    return per_dist[:, None, :]


def _mix_kernel(x_ref, a_ref, wpg_ref, bg_ref, wpool_ref, pscale_ref, woa_ref, wob_ref,
                wout_ref, g_ref, b_ref, h_ref, pbuf):
    si = pl.program_id(1)
    t = x_ref.shape[1]

    @pl.when(si == 0)
    def _():
        pbuf[0:HALO, :] = jnp.zeros((HALO, WIDTH_B), F32)

    @pl.when(si > 0)
    def _():
        pbuf[0:HALO, :] = pbuf[t:t + HALO, :]

    pbuf[HALO:HALO + t, :] = jnp.dot(x_ref[0].astype(BF16), wpg_ref[:, :WIDTH_B],
                                     preferred_element_type=F32)
    rows = [slice(c * MIX_CHUNK, (c + 1) * MIX_CHUNK) for c in range(t // MIX_CHUNK)]
    gate_logits = [jnp.dot(x_ref[0, r, :].astype(BF16), wpg_ref[:, WIDTH_B:],
                           preferred_element_type=F32) for r in rows]
    for r, gl in zip(rows, gate_logits):
        n = r.stop - r.start
        pos = si * t + r.start + lax.broadcasted_iota(jnp.int32, (n, 1), 0)
        mixed = []
        for gi, w in enumerate(POOL_WINDOWS):
            cols = slice(gi * POOL_GROUP_DIM, (gi + 1) * POOL_GROUP_DIM)
            own = pbuf[HALO + r.start:HALO + r.stop, cols]
            ws = pbuf[r.start:HALO + r.stop, cols]
            span = 1
            while span < w:
                ws = ws + pltpu.roll(ws, span, 0)
                span *= 2
            ws = ws[HALO:]
            cnt = jnp.minimum(pos + 1, w).astype(F32)
            pooled = ws / cnt - own
            mixed.append(jnp.dot(pooled.astype(BF16), wpool_ref[gi],
                                 preferred_element_type=F32))
        mixed = jnp.concatenate(mixed, axis=1) * pscale_ref[...]
        branch_b = jnp.dot(mixed.astype(BF16), wob_ref[...], preferred_element_type=F32)
        branch_a = jnp.dot(a_ref[0, r, :], woa_ref[...], preferred_element_type=F32)
        gates = jax.nn.sigmoid(gl + bg_ref[...])
        mixin = gates[:, :D_MODEL] * branch_a + gates[:, D_MODEL:] * branch_b
        mix = jnp.dot(mixin.astype(BF16), wout_ref[...], preferred_element_type=F32)
        h_ref[0, r, :] = _layer_norm(DEEPNORM_ALPHA * x_ref[0, r, :] + mix,
                                     g_ref[...], b_ref[...])


def _mix_ln1(x, a, w_pg, b_gate, w_pool, pool_scale, w_oa, w_ob, w_out, ln_g, ln_b, t=1024):
    b, s, d = x.shape
    const2 = lambda bi, si: (0, 0)
    once = pl.Buffered(1)
    return pl.pallas_call(
        _mix_kernel,
        grid=(b, s // t),
        in_specs=[
            pl.BlockSpec((1, t, d), lambda bi, si: (bi, si, 0)),
            pl.BlockSpec((1, t, WIDTH_A), lambda bi, si: (bi, si, 0)),
            pl.BlockSpec(w_pg.shape, const2, pipeline_mode=once),
            pl.BlockSpec(b_gate.shape, const2),
            pl.BlockSpec(w_pool.shape, lambda bi, si: (0, 0, 0)),
            pl.BlockSpec(pool_scale.shape, const2),
            pl.BlockSpec(w_oa.shape, const2, pipeline_mode=once),
            pl.BlockSpec(w_ob.shape, const2, pipeline_mode=once),
            pl.BlockSpec(w_out.shape, const2, pipeline_mode=once),
            pl.BlockSpec(ln_g.shape, const2),
            pl.BlockSpec(ln_b.shape, const2),
        ],
        out_specs=pl.BlockSpec((1, t, d), lambda bi, si: (bi, si, 0)),
        out_shape=jax.ShapeDtypeStruct((b, s, d), F32),
        scratch_shapes=[pltpu.VMEM((HALO + t, WIDTH_B), F32)],
        compiler_params=pltpu.CompilerParams(
            dimension_semantics=("arbitrary", "arbitrary"),
            vmem_limit_bytes=VMEM_LIMIT),
        name="mix_ln1",
    )(x, a, w_pg, b_gate, w_pool, pool_scale, w_oa, w_ob, w_out, ln_g, ln_b)


FF_CHUNK = 1024
FF_ROWS = 256


def _ffn_kernel(h_ref, w1_ref, b1_ref, w2_ref, b2_ref, g_ref, b_ref, o_ref):
    for r0 in range(0, h_ref.shape[0], FF_ROWS):
        h = h_ref[r0:r0 + FF_ROWS, :]
        hb = h.astype(BF16)
        ff = jnp.zeros(h.shape, F32)
        for c in range(D_FF // FF_CHUNK):
            cs = slice(c * FF_CHUNK, (c + 1) * FF_CHUNK)
            u = jnp.dot(hb, w1_ref[:, cs], preferred_element_type=F32) + b1_ref[:, cs]
            u = jnp.square(jnp.maximum(u, 0.0))
            ff = ff + jnp.dot(u.astype(BF16), w2_ref[cs, :], preferred_element_type=F32)
        ff = ff + b2_ref[...]
        o_ref[r0:r0 + FF_ROWS, :] = _layer_norm(DEEPNORM_ALPHA * h + ff, g_ref[...], b_ref[...])


def _ffn_ln2(h, w1, b1, w2, b2, ln_g, ln_b, t=1024):
    n, d = h.shape
    const2 = lambda i: (0, 0)
    once = pl.Buffered(1)
    return pl.pallas_call(
        _ffn_kernel,
        grid=(n // t,),
        in_specs=[
            pl.BlockSpec((t, d), lambda i: (i, 0)),
            pl.BlockSpec(w1.shape, const2, pipeline_mode=once),
            pl.BlockSpec(b1.shape, const2),
            pl.BlockSpec(w2.shape, const2, pipeline_mode=once),
            pl.BlockSpec(b2.shape, const2),
            pl.BlockSpec(ln_g.shape, const2),
            pl.BlockSpec(ln_b.shape, const2),
        ],
        out_specs=pl.BlockSpec((t, d), lambda i: (i, 0)),
        out_shape=jax.ShapeDtypeStruct((n, d), F32),
        compiler_params=pltpu.CompilerParams(
            dimension_semantics=("arbitrary",),
            vmem_limit_bytes=VMEM_LIMIT),
        name="ffn_ln2",
    )(h, w1, b1, w2, b2, ln_g, ln_b)


def kernel(x, w_in, b_gate, rel_table, w_pool, pool_scale, w_o_attn, w_o_pool, w_out,
           ln1_g, ln1_b, w_ff1, b_ff1, w_ff2, b_ff2, ln2_g, ln2_b):
    b, s, d = x.shape
    nb = s // MOBA_BLOCK
    assert s % 1024 == 0 and nb <= MAX_KEY_BLOCKS
    bias_per_dist = _bias_per_distance(rel_table)
    h = x
    for l in range(w_in.shape[0]):
        w_q = w_in[l, :, :WIDTH_A] * (HEAD_DIM ** -0.5)
        w_k = w_in[l, :, WIDTH_A:2 * WIDTH_A].astype(BF16)
        w_v = w_in[l, :, 2 * WIDTH_A:3 * WIDTH_A]
        w_qvt = jnp.concatenate([w_q, w_v], axis=1).T.astype(BF16)
        w_pg = w_in[l, :, 3 * WIDTH_A:].astype(BF16)
        q_t, k_aug, v_t = _qkv_proj(h, w_k, w_qvt)
        a = _moba_attention(q_t, k_aug, v_t, bias_per_dist)
        h = _mix_ln1(h, a, w_pg, b_gate[l][None, :], w_pool[l].astype(BF16),
                     pool_scale[l][None, :], w_o_attn[l].astype(BF16),
                     w_o_pool[l].astype(BF16), w_out[l].astype(BF16),
                     ln1_g[l][None, :], ln1_b[l][None, :])
        h = _ffn_ln2(h.reshape(b * s, d), w_ff1[l].astype(BF16), b_ff1[l][None, :],
                     w_ff2[l].astype(BF16), b_ff2[l][None, :],
                     ln2_g[l][None, :], ln2_b[l][None, :]).reshape(b, s, d)
    return h
```

```python
import math

import jax
import jax.numpy as jnp
from jax import lax
from jax.experimental import pallas as pl
from jax.experimental.pallas import tpu as pltpu

D_MODEL = 1024
N_HEADS = 8
HEAD_DIM = 64
WIDTH_A = N_HEADS * HEAD_DIM
MOBA_BLOCK = 256
MOBA_TOPK = 3
N_POOL_GROUPS = 4
POOL_WINDOWS = (2, 4, 8, 16)
WIDTH_B = 512
POOL_GROUP_DIM = WIDTH_B // N_POOL_GROUPS
D_FF = 4 * D_MODEL
REL_BUCKETS = 32
REL_MAX_DIST = 128
DEEPNORM_ALPHA = 2.0 ** 0.25
LN_EPS = 1e-5
NEG_INF = -1e30

LANES = 128
MAX_KEY_BLOCKS = 32
HEADS_PER_STEP = 8
BF16_SUBLANES = 16
V_ROWS = HEAD_DIM + BF16_SUBLANES
LOG2E = math.log2(math.e)
HALO = 16
MIX_CHUNK = 256
VMEM_LIMIT = 48 * 1024 * 1024

BF16 = jnp.bfloat16
F32 = jnp.float32
NT_DIMS = (((1,), (1,)), ((), ()))


def _layer_norm(y, g, b):
    mu = jnp.mean(y, axis=-1, keepdims=True)
    d = y - mu
    var = jnp.mean(d * d, axis=-1, keepdims=True)
    return d * lax.rsqrt(var + LN_EPS) * g + b


def _qkv_kernel(x_ref, wk_ref, wqvt_ref, q_ref, k_ref, v_ref, kmean_sc):
    si = pl.program_id(1)

    @pl.when(si == 0)
    def _():
        kmean_sc[...] = jnp.zeros(kmean_sc.shape, F32)

    xb = x_ref[0].astype(BF16)
    t = xb.shape[0]
    blocks_per_tile = t // MOBA_BLOCK
    acc_k = jnp.dot(xb, wk_ref[...], preferred_element_type=F32)
    acc_q = lax.dot_general(wqvt_ref[0:WIDTH_A, :], xb, NT_DIMS,
                            preferred_element_type=F32)
    lane = lax.broadcasted_iota(jnp.int32, (t, LANES), 1)
    row = lax.broadcasted_iota(jnp.int32, (t, LANES), 0)
    lo = lane < HEAD_DIM
    onehot = jnp.where(lane == HEAD_DIM + si * blocks_per_tile + row // MOBA_BLOCK, 1.0, 0.0)
    for hp in range(N_HEADS // 2):
        kp = acc_k[:, hp * LANES:(hp + 1) * LANES] * LOG2E
        kr = pltpu.roll(kp, HEAD_DIM, 1)
        k_ref[0, 2 * hp] = jnp.where(lo, kp, onehot).astype(BF16)
        k_ref[0, 2 * hp + 1] = jnp.where(lo, kr, onehot).astype(BF16)

    blk_row = lax.broadcasted_iota(jnp.int32, (MAX_KEY_BLOCKS, LANES), 0)
    lo_row = lax.broadcasted_iota(jnp.int32, (MAX_KEY_BLOCKS, LANES), 1) < HEAD_DIM
    for c in range(blocks_per_tile):
        mean_c = jnp.mean(acc_k[c * MOBA_BLOCK:(c + 1) * MOBA_BLOCK], axis=0, keepdims=True)
        for hp in range(N_HEADS // 2):
            pair = mean_c[:, hp * LANES:(hp + 1) * LANES]
            for h, m in ((2 * hp, pair), (2 * hp + 1, pltpu.roll(pair, HEAD_DIM, 1))):
                new_row = jnp.where(lo_row, jnp.broadcast_to(m, (MAX_KEY_BLOCKS, LANES)), 0.0)
                kmean_sc[h] = jnp.where(blk_row == si * blocks_per_tile + c, new_row, kmean_sc[h])

    sel_row = lax.broadcasted_iota(jnp.int32, (MAX_KEY_BLOCKS, t), 0)
    own_blk = si * blocks_per_tile + lax.broadcasted_iota(
        jnp.int32, (MAX_KEY_BLOCKS, t), 1) // MOBA_BLOCK
    past = sel_row < own_blk
    zeros64 = jnp.zeros((LANES - HEAD_DIM, t), BF16)
    zeros32 = jnp.zeros((LANES - HEAD_DIM - MAX_KEY_BLOCKS, t), BF16)
    pad1 = jnp.ones((V_ROWS - HEAD_DIM, t), BF16)
    for h in range(N_HEADS):
        qt = acc_q[h * HEAD_DIM:(h + 1) * HEAD_DIM].astype(BF16)
        gate = jnp.dot(kmean_sc[h].astype(BF16), jnp.concatenate([qt, zeros64], axis=0),
                       preferred_element_type=F32)
        g = jnp.where(past, gate, NEG_INF)
        for _ in range(MOBA_TOPK):
            mx = jnp.max(g, axis=0, keepdims=True)
            first = jnp.min(jnp.where(g == mx, sel_row, MAX_KEY_BLOCKS), axis=0, keepdims=True)
            g = jnp.where(sel_row == first, -jnp.inf, g)
        sel = ((g == -jnp.inf) & past) | (sel_row == own_blk)
        selbias = jnp.where(sel, 0.0, NEG_INF).astype(BF16)
        for c in range(blocks_per_tile):
            cs = slice(c * MOBA_BLOCK, (c + 1) * MOBA_BLOCK)
            q_ref[0, h, c, 0:HEAD_DIM, :] = qt[:, cs]
            q_ref[0, h, c, HEAD_DIM:HEAD_DIM + MAX_KEY_BLOCKS, :] = selbias[:, cs]
            q_ref[0, h, c, HEAD_DIM + MAX_KEY_BLOCKS:LANES, :] = zeros32[:, cs]

    acc_v = lax.dot_general(wqvt_ref[WIDTH_A:2 * WIDTH_A, :], xb, NT_DIMS,
                            preferred_element_type=F32)
    for h in range(N_HEADS):
        vt = acc_v[h * HEAD_DIM:(h + 1) * HEAD_DIM].astype(BF16)
        for c in range(blocks_per_tile):
            cs = slice(c * MOBA_BLOCK, (c + 1) * MOBA_BLOCK)
            v_ref[0, h, c, 0:HEAD_DIM, :] = vt[:, cs]
            v_ref[0, h, c, HEAD_DIM:V_ROWS, :] = pad1[:, cs]


def _qkv_proj(x, w_k, w_qvt, t=4 * MOBA_BLOCK):
    b, s, d = x.shape
    return pl.pallas_call(
        _qkv_kernel,
        grid=(b, s // t),
        in_specs=[
            pl.BlockSpec((1, t, d), lambda bi, si: (bi, si, 0)),
            pl.BlockSpec(w_k.shape, lambda bi, si: (0, 0)),
            pl.BlockSpec(w_qvt.shape, lambda bi, si: (0, 0)),
        ],
        out_specs=[
            pl.BlockSpec((1, N_HEADS, t // MOBA_BLOCK, LANES, MOBA_BLOCK),
                         lambda bi, si: (bi, 0, si, 0, 0)),
            pl.BlockSpec((1, N_HEADS, t, LANES), lambda bi, si: (bi, 0, si, 0)),
            pl.BlockSpec((1, N_HEADS, t // MOBA_BLOCK, V_ROWS, MOBA_BLOCK),
                         lambda bi, si: (bi, 0, si, 0, 0)),
        ],
        out_shape=[
            jax.ShapeDtypeStruct((b, N_HEADS, s // MOBA_BLOCK, LANES, MOBA_BLOCK), BF16),
            jax.ShapeDtypeStruct((b, N_HEADS, s, LANES), BF16),
            jax.ShapeDtypeStruct((b, N_HEADS, s // MOBA_BLOCK, V_ROWS, MOBA_BLOCK), BF16),
        ],
        scratch_shapes=[pltpu.VMEM((N_HEADS, MAX_KEY_BLOCKS, LANES), F32)],
        compiler_params=pltpu.CompilerParams(
            dimension_semantics=("arbitrary", "arbitrary"),
            vmem_limit_bytes=VMEM_LIMIT),
        name="qkv_proj",
    )(x, w_k, w_qvt)


def _attn_kernel(q_ref, k_ref, v_ref, pd_ref, o_ref, k_all, v_all, s_a, s_b, acc_sc, m_sc,
                 max_sc, bias_sc):
    i = pl.program_id(2)
    t = MOBA_BLOCK
    heads = range(HEADS_PER_STEP)

    @pl.when((pl.program_id(0) == 0) & (i == 0))
    def _():
        for h in heads:
            skew = pltpu.roll(jnp.broadcast_to(pd_ref[h], (t, 3 * t)), 0, 1,
                              stride=1, stride_axis=0)
            bias_sc[h, 0] = skew[:, 0:t]
            bias_sc[h, 1] = skew[:, t:2 * t]

    own = pl.ds(pl.multiple_of(i * t, t), t)
    for h in heads:
        k_all[h, own, :] = k_ref[0, h]
        v_all[h, :, own] = v_ref[0, h, 0]

    def key_block(h, j):
        return k_all[h, pl.ds(pl.multiple_of(j * t, t), t), :]

    def head_scores_into(s_dst, h, j):
        s = jnp.dot(key_block(h, j), q_ref[0, h, 0], preferred_element_type=F32)
        s_dst[h] = s
        return jnp.max(s, axis=0, keepdims=True)

    def weighted_values(h, j, p):
        vv = v_all[h, :, pl.ds(pl.multiple_of(j * t, t), t)]
        return jnp.dot(vv, p.astype(BF16), preferred_element_type=F32)

    def step(j, ms, s_cur, max_cur, s_nxt, bias_idx=None):
        new_ms, max_nxt = [], []
        for h in heads:
            if s_nxt is not None:
                max_nxt.append(head_scores_into(s_nxt, h, j + 1))
            s = s_cur[h]
            if bias_idx is not None:
                s = s + bias_sc[h, bias_idx]
                m_new = jnp.maximum(ms[h], jnp.max(s, axis=0, keepdims=True))
            else:
                m_new = jnp.maximum(ms[h], max_cur[h])
            alpha = jnp.exp2(ms[h] - m_new)
            acc_sc[h] = alpha * acc_sc[h] + weighted_values(h, j, jnp.exp2(s - m_new))
            new_ms.append(m_new)
        return new_ms, max_nxt

    ms = []
    for h in heads:
        s = jnp.dot(k_ref[0, h], q_ref[0, h, 0], preferred_element_type=F32) + bias_sc[h, 0]
        s_b[h] = s
        ms.append(jnp.max(s, axis=0, keepdims=True))

    max_a = []
    for h in heads:
        max_a.append(head_scores_into(s_a, h, 0))
        acc_sc[h] = jnp.dot(v_ref[0, h, 0], jnp.exp2(s_b[h] - ms[h]).astype(BF16),
                            preferred_element_type=F32)

    n_far = jnp.maximum(i - 1, 0)

    def two_steps(j, ms, max_a):
        ms, max_b = step(j, list(ms), s_a, list(max_a), s_b)
        return step(j + 1, ms, s_b, max_b, s_a)

    def four_steps(j, ms, max_a):
        ms, max_a = two_steps(j, ms, max_a)
        return two_steps(j + 2, ms, max_a)

    def eight_steps(jj, carry):
        ms, max_a = four_steps(8 * jj, *carry)
        ms, max_a = four_steps(8 * jj + 4, ms, max_a)
        return tuple(ms), tuple(max_a)

    def save_state(ms, max_a):
        for h in heads:
            m_sc[h] = ms[h]
            max_sc[h] = max_a[h]

    ms, max_a = lax.fori_loop(0, n_far // 8, eight_steps, (tuple(ms), tuple(max_a)))
    save_state(ms, max_a)

    @pl.when(n_far % 8 >= 4)
    def _():
        save_state(*four_steps(n_far - n_far % 8, [m_sc[h] for h in heads],
                               [max_sc[h] for h in heads]))

    left = n_far % 4
    j_left = n_far - left

    def finish():
        for hp in range(HEADS_PER_STEP // 2):
            outs = []
            for h in (2 * hp, 2 * hp + 1):
                acc = acc_sc[h]
                outs.append(acc[0:HEAD_DIM] / acc[HEAD_DIM:HEAD_DIM + 1])
            o_ref[0, :, hp * LANES:(hp + 1) * LANES] = jnp.concatenate(
                outs, axis=0).T.astype(o_ref.dtype)

    @pl.when(i == 0)
    def _():
        finish()

    for n_left in range(4):
        @pl.when((i >= 1) & (left == n_left))
        def _(n_left=n_left):
            ms = [m_sc[h] for h in heads]
            maxes = [max_sc[h] for h in heads]
            bufs = (s_a, s_b)
            for k in range(n_left):
                ms, maxes = step(j_left + k, ms, bufs[k % 2], maxes, bufs[(k + 1) % 2])
            step(i - 1, ms, bufs[n_left % 2], None, None, bias_idx=1)
            finish()


def _moba_attention(q_t, k_aug, v_t, bias_per_dist):
    b, h, s, _ = k_aug.shape
    t = MOBA_BLOCK
    g = HEADS_PER_STEP
    assert g == h
    once = pl.Buffered(1)
    return pl.pallas_call(
        _attn_kernel,
        grid=(b, h // g, s // t),
        in_specs=[
            pl.BlockSpec((1, g, 1, LANES, t), lambda bi, hg, qi: (bi, hg, qi, 0, 0)),
            pl.BlockSpec((1, g, t, LANES), lambda bi, hg, qi: (bi, hg, qi, 0)),
            pl.BlockSpec((1, g, 1, V_ROWS, t), lambda bi, hg, qi: (bi, hg, qi, 0, 0)),
            pl.BlockSpec((g, 1, 3 * t), lambda bi, hg, qi: (hg, 0, 0), pipeline_mode=once),
        ],
        out_specs=pl.BlockSpec((1, t, g * HEAD_DIM), lambda bi, hg, qi: (bi, qi, hg)),
        out_shape=jax.ShapeDtypeStruct((b, s, WIDTH_A), BF16),
        scratch_shapes=[
            pltpu.VMEM((g, s, LANES), BF16),
            pltpu.VMEM((g, V_ROWS, s), BF16),
            pltpu.VMEM((g, t, t), F32),
            pltpu.VMEM((g, t, t), F32),
            pltpu.VMEM((g, V_ROWS, t), F32),
            pltpu.VMEM((g, 1, t), F32),
            pltpu.VMEM((g, 1, t), F32),
            pltpu.VMEM((g, 2, t, t), F32),
        ],
        compiler_params=pltpu.CompilerParams(
            dimension_semantics=("arbitrary", "arbitrary", "arbitrary"),
            vmem_limit_bytes=VMEM_LIMIT),
        name="moba_attn",
    )(q_t, k_aug, v_t, bias_per_dist)


def _rel_bucket(dist):
    n = jnp.maximum(dist, 0)
    max_exact = REL_BUCKETS // 2
    nf = jnp.maximum(n, 1).astype(F32)
    large = max_exact + (jnp.log(nf / max_exact) / math.log(REL_MAX_DIST / max_exact)
                         * (REL_BUCKETS - max_exact)).astype(jnp.int32)
    large = jnp.minimum(large, REL_BUCKETS - 1)
    return jnp.where(n < max_exact, n, large)


def _bias_per_distance(rel_table):
    t = MOBA_BLOCK
    period = 3 * t
    dist = jnp.arange(period)
    dist = jnp.where(dist < 2 * t, dist, dist - period)
    onehot = _rel_bucket(dist)[None, :] == jnp.arange(REL_BUCKETS)[:, None]
    far = rel_table[_rel_bucket(jnp.array(2 * t, jnp.int32))]
    shifted = (rel_table - far[None, :]).T
    per_dist = jnp.sum(jnp.where(onehot[None], shifted[:, :, None], 0.0), axis=1)
    per_dist = jnp.where(dist >= 0, per_dist * LOG2E, NEG_INF).astype(F32)
    return per_dist[:, None, :]


def _mix_kernel(x_ref, a_ref, wpg_ref, bg_ref, wpool_ref, pscale_ref, woa_ref, wob_ref,
                wout_ref, g_ref, b_ref, h_ref, pbuf):
    si = pl.program_id(1)
    t = x_ref.shape[1]

    @pl.when(si == 0)
    def _():
        pbuf[0:HALO, :] = jnp.zeros((HALO, WIDTH_B), F32)

    @pl.when(si > 0)
    def _():
        pbuf[0:HALO, :] = pbuf[t:t + HALO, :]

    pbuf[HALO:HALO + t, :] = jnp.dot(x_ref[0].astype(BF16), wpg_ref[:, :WIDTH_B],
                                     preferred_element_type=F32)
    rows = [slice(c * MIX_CHUNK, (c + 1) * MIX_CHUNK) for c in range(t // MIX_CHUNK)]
    gate_logits = [jnp.dot(x_ref[0, r, :].astype(BF16), wpg_ref[:, WIDTH_B:],
                           preferred_element_type=F32) for r in rows]
    for r, gl in zip(rows, gate_logits):
        n = r.stop - r.start
        pos = si * t + r.start + lax.broadcasted_iota(jnp.int32, (n, 1), 0)
        mixed = []
        for gi, w in enumerate(POOL_WINDOWS):
            cols = slice(gi * POOL_GROUP_DIM, (gi + 1) * POOL_GROUP_DIM)
            own = pbuf[HALO + r.start:HALO + r.stop, cols]
            ws = pbuf[r.start:HALO + r.stop, cols]
            span = 1
            while span < w:
                ws = ws + pltpu.roll(ws, span, 0)
                span *= 2
            ws = ws[HALO:]
            cnt = jnp.minimum(pos + 1, w).astype(F32)
            pooled = ws / cnt - own
            mixed.append(jnp.dot(pooled.astype(BF16), wpool_ref[gi],
                                 preferred_element_type=F32))
        mixed = jnp.concatenate(mixed, axis=1) * pscale_ref[...]
        branch_b = jnp.dot(mixed.astype(BF16), wob_ref[...], preferred_element_type=F32)
        branch_a = jnp.dot(a_ref[0, r, :], woa_ref[...], preferred_element_type=F32)
        gates = jax.nn.sigmoid(gl + bg_ref[...])
        mixin = gates[:, :D_MODEL] * branch_a + gates[:, D_MODEL:] * branch_b
        mix = jnp.dot(mixin.astype(BF16), wout_ref[...], preferred_element_type=F32)
        h_ref[0, r, :] = _layer_norm(DEEPNORM_ALPHA * x_ref[0, r, :] + mix,
                                     g_ref[...], b_ref[...])


def _mix_ln1(x, a, w_pg, b_gate, w_pool, pool_scale, w_oa, w_ob, w_out, ln_g, ln_b, t=1024):
    b, s, d = x.shape
    const2 = lambda bi, si: (0, 0)
    once = pl.Buffered(1)
    return pl.pallas_call(
        _mix_kernel,
        grid=(b, s // t),
        in_specs=[
            pl.BlockSpec((1, t, d), lambda bi, si: (bi, si, 0)),
            pl.BlockSpec((1, t, WIDTH_A), lambda bi, si: (bi, si, 0)),
            pl.BlockSpec(w_pg.shape, const2, pipeline_mode=once),
            pl.BlockSpec(b_gate.shape, const2),
            pl.BlockSpec(w_pool.shape, lambda bi, si: (0, 0, 0)),
            pl.BlockSpec(pool_scale.shape, const2),
            pl.BlockSpec(w_oa.shape, const2, pipeline_mode=once),
            pl.BlockSpec(w_ob.shape, const2, pipeline_mode=once),
            pl.BlockSpec(w_out.shape, const2, pipeline_mode=once),
            pl.BlockSpec(ln_g.shape, const2),
            pl.BlockSpec(ln_b.shape, const2),
        ],
        out_specs=pl.BlockSpec((1, t, d), lambda bi, si: (bi, si, 0)),
        out_shape=jax.ShapeDtypeStruct((b, s, d), F32),
        scratch_shapes=[pltpu.VMEM((HALO + t, WIDTH_B), F32)],
        compiler_params=pltpu.CompilerParams(
            dimension_semantics=("arbitrary", "arbitrary"),
            vmem_limit_bytes=VMEM_LIMIT),
        name="mix_ln1",
    )(x, a, w_pg, b_gate, w_pool, pool_scale, w_oa, w_ob, w_out, ln_g, ln_b)


FF_CHUNK = 1024
FF_ROWS = 256


def _ffn_kernel(h_ref, w1_ref, b1_ref, w2_ref, b2_ref, g_ref, b_ref, o_ref):
    for r0 in range(0, h_ref.shape[0], FF_ROWS):
        h = h_ref[r0:r0 + FF_ROWS, :]
        hb = h.astype(BF16)
        ff = jnp.zeros(h.shape, F32)
        for c in range(D_FF // FF_CHUNK):
            cs = slice(c * FF_CHUNK, (c + 1) * FF_CHUNK)
            u = jnp.dot(hb, w1_ref[:, cs], preferred_element_type=F32) + b1_ref[:, cs]
            u = jnp.square(jnp.maximum(u, 0.0))
            ff = ff + jnp.dot(u.astype(BF16), w2_ref[cs, :], preferred_element_type=F32)
        ff = ff + b2_ref[...]
        o_ref[r0:r0 + FF_ROWS, :] = _layer_norm(DEEPNORM_ALPHA * h + ff, g_ref[...], b_ref[...])


def _ffn_ln2(h, w1, b1, w2, b2, ln_g, ln_b, t=1024):
    n, d = h.shape
    const2 = lambda i: (0, 0)
    once = pl.Buffered(1)
    return pl.pallas_call(
        _ffn_kernel,
        grid=(n // t,),
        in_specs=[
            pl.BlockSpec((t, d), lambda i: (i, 0)),
            pl.BlockSpec(w1.shape, const2, pipeline_mode=once),
            pl.BlockSpec(b1.shape, const2),
            pl.BlockSpec(w2.shape, const2, pipeline_mode=once),
            pl.BlockSpec(b2.shape, const2),
            pl.BlockSpec(ln_g.shape, const2),
            pl.BlockSpec(ln_b.shape, const2),
        ],
        out_specs=pl.BlockSpec((t, d), lambda i: (i, 0)),
        out_shape=jax.ShapeDtypeStruct((n, d), F32),
        compiler_params=pltpu.CompilerParams(
            dimension_semantics=("arbitrary",),
            vmem_limit_bytes=VMEM_LIMIT),
        name="ffn_ln2",
    )(h, w1, b1, w2, b2, ln_g, ln_b)


def kernel(x, w_in, b_gate, rel_table, w_pool, pool_scale, w_o_attn, w_o_pool, w_out,
           ln1_g, ln1_b, w_ff1, b_ff1, w_ff2, b_ff2, ln2_g, ln2_b):
    b, s, d = x.shape
    nb = s // MOBA_BLOCK
    assert s % 1024 == 0 and nb <= MAX_KEY_BLOCKS
    bias_per_dist = _bias_per_distance(rel_table)
    h = x
    for l in range(w_in.shape[0]):
        w_q = w_in[l, :, :WIDTH_A] * (HEAD_DIM ** -0.5)
        w_k = w_in[l, :, WIDTH_A:2 * WIDTH_A].astype(BF16)
        w_v = w_in[l, :, 2 * WIDTH_A:3 * WIDTH_A]
        w_qvt = jnp.concatenate([w_q, w_v], axis=1).T.astype(BF16)
        w_pg = w_in[l, :, 3 * WIDTH_A:].astype(BF16)
        q_t, k_aug, v_t = _qkv_proj(h, w_k, w_qvt)
        a = _moba_attention(q_t, k_aug, v_t, bias_per_dist)
        h = _mix_ln1(h, a, w_pg, b_gate[l][None, :], w_pool[l].astype(BF16),
                     pool_scale[l][None, :], w_o_attn[l].astype(BF16),
                     w_o_pool[l].astype(BF16), w_out[l].astype(BF16),
                     ln1_g[l][None, :], ln1_b[l][None, :])
        h = _ffn_ln2(h.reshape(b * s, d), w_ff1[l].astype(BF16), b_ff1[l][None, :],
                     w_ff2[l].astype(BF16), b_ff2[l][None, :],
                     ln2_g[l][None, :], ln2_b[l][None, :]).reshape(b, s, d)
    return h
```

```python
import math

import jax
import jax.numpy as jnp
from jax import lax
from jax.experimental import pallas as pl
from jax.experimental.pallas import tpu as pltpu

D_MODEL = 1024
N_HEADS = 8
HEAD_DIM = 64
WIDTH_A = N_HEADS * HEAD_DIM
MOBA_BLOCK = 256
MOBA_TOPK = 3
N_POOL_GROUPS = 4
POOL_WINDOWS = (2, 4, 8, 16)
WIDTH_B = 512
POOL_GROUP_DIM = WIDTH_B // N_POOL_GROUPS
D_FF = 4 * D_MODEL
REL_BUCKETS = 32
REL_MAX_DIST = 128
DEEPNORM_ALPHA = 2.0 ** 0.25
LN_EPS = 1e-5
NEG_INF = -1e30

LANES = 128
MAX_KEY_BLOCKS = 32
HEADS_PER_STEP = 8
BF16_SUBLANES = 16
V_ROWS = HEAD_DIM + BF16_SUBLANES
LOG2E = math.log2(math.e)
HALO = 16
MIX_CHUNK = 256
VMEM_LIMIT = 48 * 1024 * 1024

BF16 = jnp.bfloat16
F32 = jnp.float32
NT_DIMS = (((1,), (1,)), ((), ()))


def _layer_norm(y, g, b):
    mu = jnp.mean(y, axis=-1, keepdims=True)
    d = y - mu
    var = jnp.mean(d * d, axis=-1, keepdims=True)
    return d * lax.rsqrt(var + LN_EPS) * g + b


def _qkv_kernel(x_ref, wk_ref, wqvt_ref, q_ref, k_ref, v_ref, kmean_sc):
    si = pl.program_id(1)

    @pl.when(si == 0)
    def _():
        kmean_sc[...] = jnp.zeros(kmean_sc.shape, F32)

    xb = x_ref[0].astype(BF16)
    t = xb.shape[0]
    blocks_per_tile = t // MOBA_BLOCK
    acc_k = jnp.dot(xb, wk_ref[...], preferred_element_type=F32)
    acc_q = lax.dot_general(wqvt_ref[0:WIDTH_A, :], xb, NT_DIMS,
                            preferred_element_type=F32)
    lane = lax.broadcasted_iota(jnp.int32, (t, LANES), 1)
    row = lax.broadcasted_iota(jnp.int32, (t, LANES), 0)
    lo = lane < HEAD_DIM
    onehot = jnp.where(lane == HEAD_DIM + si * blocks_per_tile + row // MOBA_BLOCK, 1.0, 0.0)
    for hp in range(N_HEADS // 2):
        kp = acc_k[:, hp * LANES:(hp + 1) * LANES] * LOG2E
        kr = pltpu.roll(kp, HEAD_DIM, 1)
        k_ref[0, 2 * hp] = jnp.where(lo, kp, onehot).astype(BF16)
        k_ref[0, 2 * hp + 1] = jnp.where(lo, kr, onehot).astype(BF16)

    blk_row = lax.broadcasted_iota(jnp.int32, (MAX_KEY_BLOCKS, LANES), 0)
    lo_row = lax.broadcasted_iota(jnp.int32, (MAX_KEY_BLOCKS, LANES), 1) < HEAD_DIM
    for c in range(blocks_per_tile):
        mean_c = jnp.mean(acc_k[c * MOBA_BLOCK:(c + 1) * MOBA_BLOCK], axis=0, keepdims=True)
        for hp in range(N_HEADS // 2):
            pair = mean_c[:, hp * LANES:(hp + 1) * LANES]
            for h, m in ((2 * hp, pair), (2 * hp + 1, pltpu.roll(pair, HEAD_DIM, 1))):
                new_row = jnp.where(lo_row, jnp.broadcast_to(m, (MAX_KEY_BLOCKS, LANES)), 0.0)
                kmean_sc[h] = jnp.where(blk_row == si * blocks_per_tile + c, new_row, kmean_sc[h])

    sel_row = lax.broadcasted_iota(jnp.int32, (MAX_KEY_BLOCKS, t), 0)
    own_blk = si * blocks_per_tile + lax.broadcasted_iota(
        jnp.int32, (MAX_KEY_BLOCKS, t), 1) // MOBA_BLOCK
    past = sel_row < own_blk
    zeros64 = jnp.zeros((LANES - HEAD_DIM, t), BF16)
    zeros32 = jnp.zeros((LANES - HEAD_DIM - MAX_KEY_BLOCKS, t), BF16)
    pad1 = jnp.ones((V_ROWS - HEAD_DIM, t), BF16)
    for h in range(N_HEADS):
        qt = acc_q[h * HEAD_DIM:(h + 1) * HEAD_DIM].astype(BF16)
        gate = jnp.dot(kmean_sc[h].astype(BF16), jnp.concatenate([qt, zeros64], axis=0),
                       preferred_element_type=F32)
        g = jnp.where(past, gate, NEG_INF)
        for _ in range(MOBA_TOPK):
            mx = jnp.max(g, axis=0, keepdims=True)
            first = jnp.min(jnp.where(g == mx, sel_row, MAX_KEY_BLOCKS), axis=0, keepdims=True)
            g = jnp.where(sel_row == first, -jnp.inf, g)
        sel = ((g == -jnp.inf) & past) | (sel_row == own_blk)
        selbias = jnp.where(sel, 0.0, NEG_INF).astype(BF16)
        for c in range(blocks_per_tile):
            cs = slice(c * MOBA_BLOCK, (c + 1) * MOBA_BLOCK)
            q_ref[0, h, c, 0:HEAD_DIM, :] = qt[:, cs]
            q_ref[0, h, c, HEAD_DIM:HEAD_DIM + MAX_KEY_BLOCKS, :] = selbias[:, cs]
            q_ref[0, h, c, HEAD_DIM + MAX_KEY_BLOCKS:LANES, :] = zeros32[:, cs]

    acc_v = lax.dot_general(wqvt_ref[WIDTH_A:2 * WIDTH_A, :], xb, NT_DIMS,
                            preferred_element_type=F32)
    for h in range(N_HEADS):
        vt = acc_v[h * HEAD_DIM:(h + 1) * HEAD_DIM].astype(BF16)
        for c in range(blocks_per_tile):
            cs = slice(c * MOBA_BLOCK, (c + 1) * MOBA_BLOCK)
            v_ref[0, h, c, 0:HEAD_DIM, :] = vt[:, cs]
            v_ref[0, h, c, HEAD_DIM:V_ROWS, :] = pad1[:, cs]


def _qkv_proj(x, w_k, w_qvt, t=4 * MOBA_BLOCK):
    b, s, d = x.shape
    return pl.pallas_call(
        _qkv_kernel,
        grid=(b, s // t),
        in_specs=[
            pl.BlockSpec((1, t, d), lambda bi, si: (bi, si, 0)),
            pl.BlockSpec(w_k.shape, lambda bi, si: (0, 0)),
            pl.BlockSpec(w_qvt.shape, lambda bi, si: (0, 0)),
        ],
        out_specs=[
            pl.BlockSpec((1, N_HEADS, t // MOBA_BLOCK, LANES, MOBA_BLOCK),
                         lambda bi, si: (bi, 0, si, 0, 0)),
            pl.BlockSpec((1, N_HEADS, t, LANES), lambda bi, si: (bi, 0, si, 0)),
            pl.BlockSpec((1, N_HEADS, t // MOBA_BLOCK, V_ROWS, MOBA_BLOCK),
                         lambda bi, si: (bi, 0, si, 0, 0)),
        ],
        out_shape=[
            jax.ShapeDtypeStruct((b, N_HEADS, s // MOBA_BLOCK, LANES, MOBA_BLOCK), BF16),
            jax.ShapeDtypeStruct((b, N_HEADS, s, LANES), BF16),
            jax.ShapeDtypeStruct((b, N_HEADS, s // MOBA_BLOCK, V_ROWS, MOBA_BLOCK), BF16),
        ],
        scratch_shapes=[pltpu.VMEM((N_HEADS, MAX_KEY_BLOCKS, LANES), F32)],
        compiler_params=pltpu.CompilerParams(
            dimension_semantics=("arbitrary", "arbitrary"),
            vmem_limit_bytes=VMEM_LIMIT),
        name="qkv_proj",
    )(x, w_k, w_qvt)


def _attn_kernel(q_ref, k_ref, v_ref, pd_ref, o_ref, k_all, v_all, s_a, s_b, acc_sc, m_sc,
                 max_sc, bias_sc):
    i = pl.program_id(2)
    t = MOBA_BLOCK
    heads = range(HEADS_PER_STEP)

    @pl.when((pl.program_id(0) == 0) & (i == 0))
    def _():
        for h in heads:
            skew = pltpu.roll(jnp.broadcast_to(pd_ref[h], (t, 3 * t)), 0, 1,
                              stride=1, stride_axis=0)
            bias_sc[h, 0] = skew[:, 0:t]
            bias_sc[h, 1] = skew[:, t:2 * t]

    own = pl.ds(pl.multiple_of(i * t, t), t)
    for h in heads:
        k_all[h, own, :] = k_ref[0, h]
        v_all[h, :, own] = v_ref[0, h, 0]

    def key_block(h, j):
        return k_all[h, pl.ds(pl.multiple_of(j * t, t), t), :]

    def head_scores_into(s_dst, h, j):
        s = jnp.dot(key_block(h, j), q_ref[0, h, 0], preferred_element_type=F32)
        s_dst[h] = s
        return jnp.max(s, axis=0, keepdims=True)

    def weighted_values(h, j, p):
        vv = v_all[h, :, pl.ds(pl.multiple_of(j * t, t), t)]
        return jnp.dot(vv, p.astype(BF16), preferred_element_type=F32)

    def step(j, ms, s_cur, max_cur, s_nxt, bias_idx=None):
        new_ms, max_nxt = [], []
        for h in heads:
            if s_nxt is not None:
                max_nxt.append(head_scores_into(s_nxt, h, j + 1))
            s = s_cur[h]
            if bias_idx is not None:
                s = s + bias_sc[h, bias_idx]
                m_new = jnp.maximum(ms[h], jnp.max(s, axis=0, keepdims=True))
            else:
                m_new = jnp.maximum(ms[h], max_cur[h])
            alpha = jnp.exp2(ms[h] - m_new)
            acc_sc[h] = alpha * acc_sc[h] + weighted_values(h, j, jnp.exp2(s - m_new))
            new_ms.append(m_new)
        return new_ms, max_nxt

    ms = []
    for h in heads:
        s = jnp.dot(k_ref[0, h], q_ref[0, h, 0], preferred_element_type=F32) + bias_sc[h, 0]
        s_b[h] = s
        ms.append(jnp.max(s, axis=0, keepdims=True))

    max_a = []
    for h in heads:
        max_a.append(head_scores_into(s_a, h, 0))
        acc_sc[h] = jnp.dot(v_ref[0, h, 0], jnp.exp2(s_b[h] - ms[h]).astype(BF16),
                            preferred_element_type=F32)

    n_far = jnp.maximum(i - 1, 0)

    def two_steps(j, ms, max_a):
        ms, max_b = step(j, list(ms), s_a, list(max_a), s_b)
        return step(j + 1, ms, s_b, max_b, s_a)

    def four_steps(j, ms, max_a):
        ms, max_a = two_steps(j, ms, max_a)
        return two_steps(j + 2, ms, max_a)

    def eight_steps(jj, carry):
        ms, max_a = four_steps(8 * jj, *carry)
        ms, max_a = four_steps(8 * jj + 4, ms, max_a)
        return tuple(ms), tuple(max_a)

    def save_state(ms, max_a):
        for h in heads:
            m_sc[h] = ms[h]
            max_sc[h] = max_a[h]

    ms, max_a = lax.fori_loop(0, n_far // 8, eight_steps, (tuple(ms), tuple(max_a)))
    save_state(ms, max_a)

    @pl.when(n_far % 8 >= 4)
    def _():
        save_state(*four_steps(n_far - n_far % 8, [m_sc[h] for h in heads],
                               [max_sc[h] for h in heads]))

    left = n_far % 4
    j_left = n_far - left

    def finish():
        for hp in range(HEADS_PER_STEP // 2):
            outs = []
            for h in (2 * hp, 2 * hp + 1):
                acc = acc_sc[h]
                outs.append(acc[0:HEAD_DIM] / acc[HEAD_DIM:HEAD_DIM + 1])
            o_ref[0, :, hp * LANES:(hp + 1) * LANES] = jnp.concatenate(
                outs, axis=0).T.astype(o_ref.dtype)

    @pl.when(i == 0)
    def _():
        finish()

    for n_left in range(4):
        @pl.when((i >= 1) & (left == n_left))
        def _(n_left=n_left):
            ms = [m_sc[h] for h in heads]
            maxes = [max_sc[h] for h in heads]
            bufs = (s_a, s_b)
            for k in range(n_left):
                ms, maxes = step(j_left + k, ms, bufs[k % 2], maxes, bufs[(k + 1) % 2])
            step(i - 1, ms, bufs[n_left % 2], None, None, bias_idx=1)
            finish()


def _moba_attention(q_t, k_aug, v_t, bias_per_dist):
    b, h, s, _ = k_aug.shape
    t = MOBA_BLOCK
    g = HEADS_PER_STEP
    assert g == h
    once = pl.Buffered(1)
    return pl.pallas_call(
        _attn_kernel,
        grid=(b, h // g, s // t),
        in_specs=[
            pl.BlockSpec((1, g, 1, LANES, t), lambda bi, hg, qi: (bi, hg, qi, 0, 0)),
            pl.BlockSpec((1, g, t, LANES), lambda bi, hg, qi: (bi, hg, qi, 0)),
            pl.BlockSpec((1, g, 1, V_ROWS, t), lambda bi, hg, qi: (bi, hg, qi, 0, 0)),
            pl.BlockSpec((g, 1, 3 * t), lambda bi, hg, qi: (hg, 0, 0), pipeline_mode=once),
        ],
        out_specs=pl.BlockSpec((1, t, g * HEAD_DIM), lambda bi, hg, qi: (bi, qi, hg)),
        out_shape=jax.ShapeDtypeStruct((b, s, WIDTH_A), BF16),
        scratch_shapes=[
            pltpu.VMEM((g, s, LANES), BF16),
            pltpu.VMEM((g, V_ROWS, s), BF16),
            pltpu.VMEM((g, t, t), F32),
            pltpu.VMEM((g, t, t), F32),
            pltpu.VMEM((g, V_ROWS, t), F32),
            pltpu.VMEM((g, 1, t), F32),
            pltpu.VMEM((g, 1, t), F32),
            pltpu.VMEM((g, 2, t, t), F32),
        ],
        compiler_params=pltpu.CompilerParams(
            dimension_semantics=("arbitrary", "arbitrary", "arbitrary"),
            vmem_limit_bytes=VMEM_LIMIT),
        name="moba_attn",
    )(q_t, k_aug, v_t, bias_per_dist)


def _rel_bucket(dist):
    n = jnp.maximum(dist, 0)
    max_exact = REL_BUCKETS // 2
    nf = jnp.maximum(n, 1).astype(F32)
    large = max_exact + (jnp.log(nf / max_exact) / math.log(REL_MAX_DIST / max_exact)
                         * (REL_BUCKETS - max_exact)).astype(jnp.int32)
    large = jnp.minimum(large, REL_BUCKETS - 1)
    return jnp.where(n < max_exact, n, large)


def _bias_per_distance(rel_table):
    t = MOBA_BLOCK
    period = 3 * t
    dist = jnp.arange(period)
    dist = jnp.where(dist < 2 * t, dist, dist - period)
    onehot = _rel_bucket(dist)[None, :] == jnp.arange(REL_BUCKETS)[:, None]
    far = rel_table[_rel_bucket(jnp.array(2 * t, jnp.int32))]
    shifted = (rel_table - far[None, :]).T
    per_dist = jnp.sum(jnp.where(onehot[None], shifted[:, :, None], 0.0), axis=1)
    per_dist = jnp.where(dist >= 0, per_dist * LOG2E, NEG_INF).astype(F32)
    return per_dist[:, None, :]


def _mix_kernel(x_ref, a_ref, wpg_ref, bg_ref, wpool_ref, pscale_ref, woa_ref, wob_ref,
                wout_ref, g_ref, b_ref, w1_ref, b1_ref, w2_ref, b2_ref, g2_ref, bln2_ref,
                h_ref, pbuf):
    si = pl.program_id(1)
    t = x_ref.shape[1]

    @pl.when(si == 0)
    def _():
        pbuf[0:HALO, :] = jnp.zeros((HALO, WIDTH_B), F32)

    @pl.when(si > 0)
    def _():
        pbuf[0:HALO, :] = pbuf[t:t + HALO, :]

    pbuf[HALO:HALO + t, :] = jnp.dot(x_ref[0].astype(BF16), wpg_ref[:, :WIDTH_B],
                                     preferred_element_type=F32)
    rows = [slice(c * MIX_CHUNK, (c + 1) * MIX_CHUNK) for c in range(t // MIX_CHUNK)]
    gate_logits = [jnp.dot(x_ref[0, r, :].astype(BF16), wpg_ref[:, WIDTH_B:],
                           preferred_element_type=F32) for r in rows]
    for r, gl in zip(rows, gate_logits):
        n = r.stop - r.start
        pos = si * t + r.start + lax.broadcasted_iota(jnp.int32, (n, 1), 0)
        mixed = []
        for gi, w in enumerate(POOL_WINDOWS):
            cols = slice(gi * POOL_GROUP_DIM, (gi + 1) * POOL_GROUP_DIM)
            own = pbuf[HALO + r.start:HALO + r.stop, cols]
            ws = pbuf[r.start:HALO + r.stop, cols]
            span = 1
            while span < w:
                ws = ws + pltpu.roll(ws, span, 0)
                span *= 2
            ws = ws[HALO:]
            cnt = jnp.minimum(pos + 1, w).astype(F32)
            pooled = ws / cnt - own
            mixed.append(jnp.dot(pooled.astype(BF16), wpool_ref[gi],
                                 preferred_element_type=F32))
        mixed = jnp.concatenate(mixed, axis=1) * pscale_ref[...]
        branch_b = jnp.dot(mixed.astype(BF16), wob_ref[...], preferred_element_type=F32)
        branch_a = jnp.dot(a_ref[0, r, :], woa_ref[...], preferred_element_type=F32)
        gates = jax.nn.sigmoid(gl + bg_ref[...])
        mixin = gates[:, :D_MODEL] * branch_a + gates[:, D_MODEL:] * branch_b
        mix = jnp.dot(mixin.astype(BF16), wout_ref[...], preferred_element_type=F32)
        h = _layer_norm(DEEPNORM_ALPHA * x_ref[0, r, :] + mix, g_ref[...], b_ref[...])
        hb = h.astype(BF16)
        ff = jnp.zeros(h.shape, F32)
        for c in range(D_FF // FF_CHUNK):
            cs = slice(c * FF_CHUNK, (c + 1) * FF_CHUNK)
            u = jnp.dot(hb, w1_ref[:, cs], preferred_element_type=F32) + b1_ref[:, cs]
            u = jnp.square(jnp.maximum(u, 0.0))
            ff = ff + jnp.dot(u.astype(BF16), w2_ref[cs, :], preferred_element_type=F32)
        ff = ff + b2_ref[...]
        h_ref[0, r, :] = _layer_norm(DEEPNORM_ALPHA * h + ff, g2_ref[...], bln2_ref[...])


def _mix_ln1(x, a, w_pg, b_gate, w_pool, pool_scale, w_oa, w_ob, w_out, ln_g, ln_b,
             w1, b1, w2, b2, ln2_g, ln2_b, t=512):
    b, s, d = x.shape
    const2 = lambda bi, si: (0, 0)
    once = pl.Buffered(1)
    return pl.pallas_call(
        _mix_kernel,
        grid=(b, s // t),
        in_specs=[
            pl.BlockSpec((1, t, d), lambda bi, si: (bi, si, 0)),
            pl.BlockSpec((1, t, WIDTH_A), lambda bi, si: (bi, si, 0)),
            pl.BlockSpec(w_pg.shape, const2, pipeline_mode=once),
            pl.BlockSpec(b_gate.shape, const2),
            pl.BlockSpec(w_pool.shape, lambda bi, si: (0, 0, 0)),
            pl.BlockSpec(pool_scale.shape, const2),
            pl.BlockSpec(w_oa.shape, const2, pipeline_mode=once),
            pl.BlockSpec(w_ob.shape, const2, pipeline_mode=once),
            pl.BlockSpec(w_out.shape, const2, pipeline_mode=once),
            pl.BlockSpec(ln_g.shape, const2),
            pl.BlockSpec(ln_b.shape, const2),
            pl.BlockSpec(w1.shape, const2, pipeline_mode=once),
            pl.BlockSpec(b1.shape, const2),
            pl.BlockSpec(w2.shape, const2, pipeline_mode=once),
            pl.BlockSpec(b2.shape, const2),
            pl.BlockSpec(ln2_g.shape, const2),
            pl.BlockSpec(ln2_b.shape, const2),
        ],
        out_specs=pl.BlockSpec((1, t, d), lambda bi, si: (bi, si, 0)),
        out_shape=jax.ShapeDtypeStruct((b, s, d), F32),
        scratch_shapes=[pltpu.VMEM((HALO + t, WIDTH_B), F32)],
        compiler_params=pltpu.CompilerParams(
            dimension_semantics=("arbitrary", "arbitrary"),
            vmem_limit_bytes=VMEM_LIMIT),
        name="mix_ffn",
    )(x, a, w_pg, b_gate, w_pool, pool_scale, w_oa, w_ob, w_out, ln_g, ln_b,
      w1, b1, w2, b2, ln2_g, ln2_b)


FF_CHUNK = 1024
FF_ROWS = 256


def _ffn_kernel(h_ref, w1_ref, b1_ref, w2_ref, b2_ref, g_ref, b_ref, o_ref):
    for r0 in range(0, h_ref.shape[0], FF_ROWS):
        h = h_ref[r0:r0 + FF_ROWS, :]
        hb = h.astype(BF16)
        ff = jnp.zeros(h.shape, F32)
        for c in range(D_FF // FF_CHUNK):
            cs = slice(c * FF_CHUNK, (c + 1) * FF_CHUNK)
            u = jnp.dot(hb, w1_ref[:, cs], preferred_element_type=F32) + b1_ref[:, cs]
            u = jnp.square(jnp.maximum(u, 0.0))
            ff = ff + jnp.dot(u.astype(BF16), w2_ref[cs, :], preferred_element_type=F32)
        ff = ff + b2_ref[...]
        o_ref[r0:r0 + FF_ROWS, :] = _layer_norm(DEEPNORM_ALPHA * h + ff, g_ref[...], b_ref[...])


def _ffn_ln2(h, w1, b1, w2, b2, ln_g, ln_b, t=1024):
    n, d = h.shape
    const2 = lambda i: (0, 0)
    once = pl.Buffered(1)
    return pl.pallas_call(
        _ffn_kernel,
        grid=(n // t,),
        in_specs=[
            pl.BlockSpec((t, d), lambda i: (i, 0)),
            pl.BlockSpec(w1.shape, const2, pipeline_mode=once),
            pl.BlockSpec(b1.shape, const2),
            pl.BlockSpec(w2.shape, const2, pipeline_mode=once),
            pl.BlockSpec(b2.shape, const2),
            pl.BlockSpec(ln_g.shape, const2),
            pl.BlockSpec(ln_b.shape, const2),
        ],
        out_specs=pl.BlockSpec((t, d), lambda i: (i, 0)),
        out_shape=jax.ShapeDtypeStruct((n, d), F32),
        compiler_params=pltpu.CompilerParams(
            dimension_semantics=("arbitrary",),
            vmem_limit_bytes=VMEM_LIMIT),
        name="ffn_ln2",
    )(h, w1, b1, w2, b2, ln_g, ln_b)


def kernel(x, w_in, b_gate, rel_table, w_pool, pool_scale, w_o_attn, w_o_pool, w_out,
           ln1_g, ln1_b, w_ff1, b_ff1, w_ff2, b_ff2, ln2_g, ln2_b):
    b, s, d = x.shape
    nb = s // MOBA_BLOCK
    assert s % 1024 == 0 and nb <= MAX_KEY_BLOCKS
    bias_per_dist = _bias_per_distance(rel_table)
    h = x
    for l in range(w_in.shape[0]):
        w_q = w_in[l, :, :WIDTH_A] * (HEAD_DIM ** -0.5)
        w_k = w_in[l, :, WIDTH_A:2 * WIDTH_A].astype(BF16)
        w_v = w_in[l, :, 2 * WIDTH_A:3 * WIDTH_A]
        w_qvt = jnp.concatenate([w_q, w_v], axis=1).T.astype(BF16)
        w_pg = w_in[l, :, 3 * WIDTH_A:].astype(BF16)
        q_t, k_aug, v_t = _qkv_proj(h, w_k, w_qvt)
        a = _moba_attention(q_t, k_aug, v_t, bias_per_dist)
        h = _mix_ln1(h, a, w_pg, b_gate[l][None, :], w_pool[l].astype(BF16),
                     pool_scale[l][None, :], w_o_attn[l].astype(BF16),
                     w_o_pool[l].astype(BF16), w_out[l].astype(BF16),
                     ln1_g[l][None, :], ln1_b[l][None, :],
                     w_ff1[l].astype(BF16), b_ff1[l][None, :], w_ff2[l].astype(BF16),
                     b_ff2[l][None, :], ln2_g[l][None, :], ln2_b[l][None, :])
    return h
```
